```python
import math
import jax, jax.numpy as jnp
from jax import lax
import numpy as np

D_MODEL = 4096
BATCH = 4
SEQ = 4096
DEPTH = 4

HEAD_DIM = 128
BRANCH_WIDTH = 1024
N_BRANCH = 4
GLA_HEADS = 8
GLA_DK = 64
GLA_DV = 128
GLA_GATE_RANK = 16
GLA_TAU = 16.0
GLA_CHUNK = 64
MLSTM_HEADS = 8
MLSTM_DQK = 64
MLSTM_DV = 128
MLSTM_CONV = 4
MLSTM_CHUNK = 64
MLSTM_IGATE_CAP = 15.0
DIL_PATTERNS = ((128, 1), (512, 4), (2048, 16))
N_DIL = 3
DIL_HEADS = 8
DIL_BLOCK = 128
NSA_HEADS = 8
NSA_KV_GROUPS = 2
NSA_HPG = 4
NSA_CMP_BLOCK = 32
NSA_CMP_STRIDE = 16
NSA_CMP_HIDDEN = 256
NSA_SEL_BLOCK = 64
NSA_N_SEL = 16
NSA_WINDOW = 512
NSA_WIN_BLOCK = 128
NSA_QBLOCK = 64
REL_BUCKETS = 32
REL_MAX_DIST = 2048
REL_HEADS = N_DIL * DIL_HEADS + NSA_HEADS
N_EXPERTS = 32
TOP_K = 4
EXPERT_FF = 128
SWIGLU_LIMIT = 7.0
SWIGLU_ALPHA = 1.702
DEEPNORM_ALPHA = (2 * DEPTH) ** 0.25
DEEPNORM_BETA = (8 * DEPTH) ** -0.25
LN_EPS = 1e-5
NEG = -1e30
BIG = 1e9

IN_SPLIT_SIZES = (
    GLA_HEADS * GLA_DK, GLA_HEADS * GLA_DK, GLA_HEADS * GLA_DV, GLA_HEADS * GLA_DV, GLA_GATE_RANK,
    2 * MLSTM_HEADS * MLSTM_DQK, MLSTM_HEADS * MLSTM_DV, MLSTM_HEADS * MLSTM_DV, MLSTM_HEADS, MLSTM_HEADS,
    N_DIL * DIL_HEADS * HEAD_DIM, DIL_HEADS * HEAD_DIM, DIL_HEADS * HEAD_DIM,
    NSA_HEADS * HEAD_DIM,
    NSA_KV_GROUPS * HEAD_DIM, NSA_KV_GROUPS * HEAD_DIM, NSA_KV_GROUPS * HEAD_DIM,
    NSA_KV_GROUPS * HEAD_DIM, NSA_KV_GROUPS * HEAD_DIM, NSA_KV_GROUPS * HEAD_DIM,
    NSA_HEADS * 3,
    N_BRANCH * D_MODEL,
)
IN_WIDTH = sum(IN_SPLIT_SIZES)

kernel_name = 'hybrid_gla_mlstm_dilated_nsa_moe_deepnorm'


def layer_norm(z, g, b):
    z = z.astype(jnp.float32)
    mu = jnp.mean(z, -1, keepdims=True)
    var = jnp.mean(jnp.square(z - mu), -1, keepdims=True)
    return (z - mu) * lax.rsqrt(var + LN_EPS) * g + b


def head_norm(z, g):
    mu = jnp.mean(z, -1, keepdims=True)
    var = jnp.mean(jnp.square(z - mu), -1, keepdims=True)
    return (z - mu) * lax.rsqrt(var + LN_EPS) * g.reshape(z.shape[-2:])


def t5_bucket(dist):
    d = jnp.maximum(dist, 0)
    exact = REL_BUCKETS // 2
    df = jnp.maximum(d, 1).astype(jnp.float32)
    large = exact + (jnp.log(df / exact) / math.log(REL_MAX_DIST / exact) * (REL_BUCKETS - exact)).astype(jnp.int32)
    return jnp.where(d < exact, d, jnp.minimum(large, REL_BUCKETS - 1))


def _split_in(u):
    points = np.cumsum(IN_SPLIT_SIZES)[:-1].tolist()
    return jnp.split(u, points, axis=-1)


def gla_mixer(q, k, v, r, a_lr, gate_w, gate_b, norm_g):
    B, S = q.shape[:2]
    H, L = GLA_HEADS, GLA_CHUNK
    N = S // L

    def chunk(t):
        return t.reshape(B, N, L, H, -1).transpose(0, 3, 1, 2, 4)

    q = chunk(q * GLA_DK ** -0.5)
    k = chunk(k)
    v = chunk(v)
    log_a = chunk(jax.nn.log_sigmoid(a_lr @ gate_w + gate_b) / GLA_TAU)
    b = jnp.cumsum(log_a, axis=3)
    b_last = b[:, :, :, -1:, :]
    q_dec = q * jnp.exp(b)
    causal = jnp.tril(jnp.ones((L, L), bool))
    att = jnp.where(causal, jnp.einsum('bhnik,bhnjk->bhnij', q_dec, k * jnp.exp(-b)), 0.0)
    o_intra = jnp.einsum('bhnij,bhnjv->bhniv', att, v)
    kv_upd = jnp.einsum('bhnjk,bhnjv->bhnkv', k * jnp.exp(b_last - b), v)
    decay = jnp.exp(b_last[:, :, :, 0, :])

    def step(state, inp):
        dec, upd = inp
        return dec[..., None] * state + upd, state

    init = jnp.zeros((B, H, GLA_DK, GLA_DV), jnp.float32)
    _, states = lax.scan(step, init, (jnp.moveaxis(decay, 2, 0), jnp.moveaxis(kv_upd, 2, 0)))
    states = jnp.moveaxis(states, 0, 2)
    o = o_intra + jnp.einsum('bhnik,bhnkv->bhniv', q_dec, states)
    o = head_norm(o.transpose(0, 2, 3, 1, 4).reshape(B, S, H, GLA_DV), norm_g)
    return jax.nn.silu(r) * o.reshape(B, S, H * GLA_DV)


def mlstm_mixer(qk_pre, v, o_pre, i_pre, f_pre, conv_w, conv_b, igate_b, fgate_b, norm_g):
    B, S, C = qk_pre.shape
    H, L = MLSTM_HEADS, MLSTM_CHUNK
    N = S // L
    qk = lax.conv_general_dilated(qk_pre, conv_w.astype(qk_pre.dtype)[:, None, :], window_strides=(1,),
                                  padding=[(MLSTM_CONV - 1, 0)], dimension_numbers=('NWC', 'WIO', 'NWC'),
                                  feature_group_count=C) + conv_b
    qk = jax.nn.silu(qk)
    q, k = qk[..., :C // 2], qk[..., C // 2:]

    def chunk(t):
        return t.reshape(B, N, L, H, -1).transpose(0, 3, 1, 2, 4)

    def chunk_s(t):
        return t.reshape(B, N, L, H).transpose(0, 3, 1, 2)

    q = chunk(q)
    k = chunk(k * MLSTM_DQK ** -0.5)
    v = chunk(v)
    i_t = chunk_s(MLSTM_IGATE_CAP * jnp.tanh((i_pre + igate_b) / MLSTM_IGATE_CAP))
    log_f = chunk_s(jax.nn.log_sigmoid(f_pre + fgate_b))
    b = jnp.cumsum(log_f, axis=-1)
    b_last = b[..., -1]
    causal = jnp.tril(jnp.ones((L, L), bool))
    dmat = jnp.where(causal, b[..., :, None] - b[..., None, :] + i_t[..., None, :], -jnp.inf)
    a_end = b_last[..., None] - b + i_t
    m_loc = jnp.max(a_end, -1)

    def step(carry, inp):
        c_s, n_s, m_s = carry
        g, ml, a, kc, vc = inp
        m_new = jnp.maximum(g + m_s, ml)
        w = jnp.exp(a - m_new[..., None])
        sc = jnp.exp(g + m_s - m_new)
        c_new = sc[..., None, None] * c_s + jnp.einsum('bhl,bhlk,bhlv->bhkv', w, kc, vc)
        n_new = sc[..., None] * n_s + jnp.einsum('bhl,bhlk->bhk', w, kc)
        return (c_new, n_new, m_new), (c_s, n_s, m_s)

    init = (jnp.zeros((B, H, MLSTM_DQK, MLSTM_DV), jnp.float32),
            jnp.zeros((B, H, MLSTM_DQK), jnp.float32),
            jnp.zeros((B, H), jnp.float32))
    xs = tuple(jnp.moveaxis(t, 2, 0) for t in (b_last, m_loc, a_end, k, v))
    _, (c_st, n_st, m_st) = lax.scan(step, init, xs)
    c_st, n_st, m_st = jnp.moveaxis(c_st, 0, 2), jnp.moveaxis(n_st, 0, 2), jnp.moveaxis(m_st, 0, 2)
    inter_log = b + m_st[..., None]
    m_i = jnp.maximum(inter_log, jnp.max(dmat, -1))
    w_inter = jnp.exp(inter_log - m_i)
    qk_s = jnp.einsum('bhnid,bhnjd->bhnij', q, k) * jnp.exp(dmat - m_i[..., None])
    num = w_inter[..., None] * jnp.einsum('bhnik,bhnkv->bhniv', q, c_st) + jnp.einsum('bhnij,bhnjv->bhniv', qk_s, v)
    den = w_inter * jnp.einsum('bhnik,bhnk->bhni', q, n_st) + jnp.sum(qk_s, -1)
    h = num / jnp.maximum(jnp.abs(den), jnp.exp(-m_i))[..., None]
    h = head_norm(h.transpose(0, 2, 3, 1, 4).reshape(B, S, H, MLSTM_DV), norm_g)
    return jax.nn.sigmoid(o_pre) * h.reshape(B, S, H * MLSTM_DV)


def dilated_mixer(q, k, v, rel_bias):
    B, S = q.shape[:2]
    H, dh, P = DIL_HEADS, HEAD_DIM, DIL_BLOCK
    q = q.reshape(B, S, N_DIL, H, dh) * dh ** -0.5
    k = k.reshape(B, S, H, dh)
    v = v.reshape(B, S, H, dh)
    outs, lses = [], []
    for g, (window, dil) in enumerate(DIL_PATTERNS):
        span = dil * P
        s_pad = -(-S // span) * span
        nb = s_pad // span

        def to_phase(t):
            t = jnp.pad(t, ((0, 0), (0, s_pad - S), (0, 0), (0, 0)))
            return t.reshape(B, s_pad // dil, dil, H, dh).swapaxes(1, 2).reshape(B, dil, nb, P, H, dh)

        def band(t):
            prev = jnp.pad(t, ((0, 0), (0, 0), (1, 0), (0, 0), (0, 0), (0, 0)))[:, :, :-1]
            return jnp.concatenate([prev, t], axis=3)

        qg = to_phase(q[:, :, g])
        kb = band(to_phase(k))
        vb = band(to_phase(v))
        delta = jnp.arange(P)[:, None] + P - jnp.arange(2 * P)[None, :]
        bias = rel_bias[t5_bucket(delta * dil), g * H:(g + 1) * H].transpose(2, 0, 1)
        in_win = (delta >= 0) & (delta <= window // dil)
        blk_ok = (jnp.arange(nb)[:, None] > 0) | (jnp.arange(2 * P)[None, :] >= P)
        mask = in_win[None] & blk_ok[:, None, :]
        s = jnp.einsum('brnqhd,brnkhd->brnhqk', qg, kb) + bias
        s = jnp.where(mask[:, None], s, NEG)
        m = jnp.max(s, -1, keepdims=True)
        p = jnp.exp(s - m)
        l = jnp.sum(p, -1)
        o = jnp.einsum('brnhqk,brnkhd->brnqhd', p, vb) / jnp.swapaxes(l, 3, 4)[..., None]
        lse = jnp.swapaxes(m[..., 0] + jnp.log(l), 3, 4)
        outs.append(o.reshape(B, dil, nb * P, H, dh).swapaxes(1, 2).reshape(B, s_pad, H, dh)[:, :S])
        lses.append(lse.reshape(B, dil, nb * P, H).swapaxes(1, 2).reshape(B, s_pad, H)[:, :S])
    w = jax.nn.softmax(jnp.stack(lses), axis=0)
    o = jnp.sum(w[..., None] * jnp.stack(outs), axis=0)
    return o.reshape(B, S, H * dh)


def _sel_agg_matrix(n_cmp, n_sb):
    r = NSA_SEL_BLOCK // NSA_CMP_STRIDE
    c = NSA_CMP_BLOCK // NSA_CMP_STRIDE
    jj, aa, bb = np.meshgrid(np.arange(n_sb), np.arange(r), np.arange(c), indexing='ij')
    ii = r * jj + aa + bb - 1
    ok = (ii >= 0) & (ii < n_cmp)
    mat = np.zeros((n_cmp, n_sb), np.float32)
    np.add.at(mat, (ii[ok], jj[ok]), 1.0)
    return mat


def nsa_mixer(q, kc, vc, ks, vs, kw, vw, gate_pre, rel_bias, pe_k, pe_v, ck_w1, ck_w2, cv_w1, cv_w2):
    B, S = q.shape[:2]
    G, HG, dh = NSA_KV_GROUPS, NSA_HPG, HEAD_DIM
    q = q.reshape(B, S, G, HG, dh) * dh ** -0.5
    kc, vc, ks, vs, kw, vw = [t.reshape(B, S, G, dh) for t in (kc, vc, ks, vs, kw, vw)]
    tab = rel_bias[:, N_DIL * DIL_HEADS:].reshape(REL_BUCKETS, G, HG)
    pos = jnp.arange(S)

    n_cmp = (S - NSA_CMP_BLOCK) // NSA_CMP_STRIDE + 1
    starts = np.arange(n_cmp) * NSA_CMP_STRIDE
    ends = jnp.asarray(starts + NSA_CMP_BLOCK - 1)
    gidx = starts[:, None] + np.arange(NSA_CMP_BLOCK)[None, :]

    def compress(t, pe, w1, w2):
        blk = (t[:, gidx] + pe[:, None, :]).transpose(0, 1, 3, 2, 4)
        blk = blk.reshape(B, n_cmp, G, NSA_CMP_BLOCK * dh)
        return jax.nn.gelu(blk @ w1) @ w2

    k_cmp = compress(kc, pe_k, ck_w1, ck_w2)
    v_cmp = compress(vc, pe_v, cv_w1, cv_w2)
    mask_c = ends[None, :] <= pos[:, None]
    bias_c = tab[t5_bucket(pos[:, None] - ends[None, :])].transpose(2, 3, 0, 1)
    s_c = jnp.where(mask_c, jnp.einsum('bsghd,bcgd->bghsc', q, k_cmp) + bias_c, NEG)
    p_c = jnp.exp(s_c - jnp.max(s_c, -1, keepdims=True)) * mask_c
    l_c = jnp.sum(p_c, -1, keepdims=True)
    p_c = p_c / jnp.where(l_c > 0, l_c, 1.0)
    o_c = jnp.einsum('bghsc,bcgd->bsghd', p_c, v_cmp)

    n_sb = S // NSA_SEL_BLOCK
    imp = jnp.einsum('bghsc,cj->bgsj', p_c, jnp.asarray(_sel_agg_matrix(n_cmp, n_sb)))
    blk = jnp.arange(n_sb)[None, :]
    cur = (pos // NSA_SEL_BLOCK)[:, None]
    forced = (blk == 0) | (blk == cur) | (blk == cur - 1)
    score = jnp.where(forced, BIG, jnp.where(blk * NSA_SEL_BLOCK <= pos[:, None], imp, -BIG))
    n_sel = min(NSA_N_SEL, n_sb)
    sel = lax.top_k(score, n_sel)[1]
    kb = ks.reshape(B, n_sb, NSA_SEL_BLOCK, G, dh).transpose(0, 3, 1, 2, 4)
    vb = vs.reshape(B, n_sb, NSA_SEL_BLOCK, G, dh).transpose(0, 3, 1, 2, 4)
    QB = NSA_QBLOCK
    nqb = S // QB
    q_blocks = q.reshape(B, nqb, QB, G, HG, dh).transpose(1, 0, 3, 4, 2, 5)
    idx_blocks = sel.reshape(B, G, nqb, QB, n_sel).transpose(2, 0, 1, 3, 4)
    pos_blocks = pos.reshape(nqb, QB)
    b_idx = jnp.arange(B)[:, None, None]
    g_idx = jnp.arange(G)[None, :, None]
    g_idx5 = jnp.arange(G)[None, :, None, None, None]

    def sel_block(args):
        qb, ib, tb = args
        flat = ib.reshape(B, G, QB * n_sel)
        kg = kb[b_idx, g_idx, flat].reshape(B, G, QB, n_sel, NSA_SEL_BLOCK, dh)
        vg = vb[b_idx, g_idx, flat].reshape(B, G, QB, n_sel, NSA_SEL_BLOCK, dh)
        kpos = ib[..., None] * NSA_SEL_BLOCK + jnp.arange(NSA_SEL_BLOCK)
        dist = tb[None, None, :, None, None] - kpos
        bias = tab[t5_bucket(dist), g_idx5].transpose(0, 1, 5, 2, 3, 4)
        s = jnp.einsum('bghqd,bgqnld->bghqnl', qb, kg) + bias
        s = jnp.where((dist >= 0)[:, :, None], s, NEG)
        p = jax.nn.softmax(s.reshape(B, G, HG, QB, n_sel * NSA_SEL_BLOCK), axis=-1)
        p = p.reshape(B, G, HG, QB, n_sel, NSA_SEL_BLOCK)
        return jnp.einsum('bghqnl,bgqnld->bghqd', p, vg)

    o_s = lax.map(sel_block, (q_blocks, idx_blocks, pos_blocks))
    o_s = o_s.transpose(1, 0, 4, 2, 3, 5).reshape(B, S, G, HG, dh)

    WB = NSA_WIN_BLOCK
    nwb = S // WB
    nprev = NSA_WINDOW // WB

    def win_band(t):
        t = jnp.pad(t.reshape(B, nwb, WB, G, dh), ((0, 0), (nprev, 0), (0, 0), (0, 0), (0, 0)))
        return jnp.concatenate([t[:, i:i + nwb] for i in range(nprev + 1)], axis=2)

    kwb, vwb = win_band(kw), win_band(vw)
    qw = q.reshape(B, nwb, WB, G, HG, dh)
    delta = jnp.arange(WB)[:, None] + nprev * WB - jnp.arange((nprev + 1) * WB)[None, :]
    bias_w = tab[t5_bucket(delta)].transpose(2, 3, 0, 1)
    key_blk_ok = (jnp.arange(nwb)[:, None] - nprev + jnp.arange((nprev + 1) * WB)[None, :] // WB) >= 0
    mask_w = ((delta >= 0) & (delta < NSA_WINDOW))[None] & key_blk_ok[:, None, :]
    s_w = jnp.einsum('bnqghd,bnkgd->bnghqk', qw, kwb) + bias_w
    s_w = jnp.where(mask_w[:, None, None], s_w, NEG)
    p_w = jax.nn.softmax(s_w, axis=-1)
    o_w = jnp.einsum('bnghqk,bnkgd->bnqghd', p_w, vwb).reshape(B, S, G, HG, dh)

    gates = jax.nn.sigmoid(gate_pre.reshape(B, S, G, HG, 3))
    o = gates[..., 0:1] * o_c + gates[..., 1:2] * o_s + gates[..., 2:3] * o_w
    return o.reshape(B, S, NSA_HEADS * dh)


def moe_ffn(x, router_w, router_b, w1, b1, w2, b2):
    B, S, D = x.shape
    xt = x.reshape(B * S, D)
    logits = (xt @ router_w + router_b).astype(jnp.float32)
    top_v, top_i = lax.top_k(logits, TOP_K)
    comb = jnp.sum(jax.nn.one_hot(top_i, N_EXPERTS, dtype=jnp.float32) * jax.nn.softmax(top_v, -1)[..., None], axis=1)
    h = jnp.einsum('nd,edf->nef', xt, w1) + b1
    gate = jnp.minimum(h[..., :EXPERT_FF], SWIGLU_LIMIT)
    up = jnp.clip(h[..., EXPERT_FF:], -SWIGLU_LIMIT, SWIGLU_LIMIT)
    act = gate * jax.nn.sigmoid(SWIGLU_ALPHA * gate) * (up + 1.0)
    out = jnp.einsum('nef,efd->nd', act * comb[..., None], w2) + comb @ b2
    return out.reshape(B, S, D)


def setup_inputs(seed: int = 0) -> dict:
    key = jax.random.key(seed)
    ks = jax.random.split(key, 32)
    L, D = DEPTH, D_MODEL

    def nrm(k, shape, scale):
        return jax.random.normal(k, shape, jnp.float32) * scale

    return {
        'x': nrm(ks[0], (BATCH, SEQ, D), 1.0),
        'rel_bias': nrm(ks[1], (REL_BUCKETS, REL_HEADS), 0.2),
        'w_in': nrm(ks[2], (L, D, IN_WIDTH), D ** -0.5),
        'gla_gate_w': nrm(ks[3], (L, GLA_GATE_RANK, GLA_HEADS * GLA_DK), GLA_GATE_RANK ** -0.5),
        'gla_gate_b': nrm(ks[4], (L, GLA_HEADS * GLA_DK), 0.1),
        'gla_norm_g': 1.0 + nrm(ks[5], (L, GLA_HEADS * GLA_DV), 0.02),
        'mlstm_conv_w': nrm(ks[6], (L, MLSTM_CONV, 2 * MLSTM_HEADS * MLSTM_DQK), MLSTM_CONV ** -0.5),
        'mlstm_conv_b': nrm(ks[7], (L, 2 * MLSTM_HEADS * MLSTM_DQK), 0.02),
        'mlstm_igate_b': nrm(ks[8], (L, MLSTM_HEADS), 0.1),
        'mlstm_fgate_b': jnp.linspace(3.0, 6.0, MLSTM_HEADS, dtype=jnp.float32)[None, :] + nrm(ks[9], (L, MLSTM_HEADS), 0.1),
        'mlstm_norm_g': 1.0 + nrm(ks[10], (L, MLSTM_HEADS * MLSTM_DV), 0.02),
        'nsa_pe_k': nrm(ks[11], (L, NSA_CMP_BLOCK, HEAD_DIM), 0.1),
        'nsa_pe_v': nrm(ks[12], (L, NSA_CMP_BLOCK, HEAD_DIM), 0.1),
        'nsa_ck_w1': nrm(ks[13], (L, NSA_CMP_BLOCK * HEAD_DIM, NSA_CMP_HIDDEN), (NSA_CMP_BLOCK * HEAD_DIM) ** -0.5),
        'nsa_ck_w2': nrm(ks[14], (L, NSA_CMP_HIDDEN, HEAD_DIM), NSA_CMP_HIDDEN ** -0.5),
        'nsa_cv_w1': nrm(ks[15], (L, NSA_CMP_BLOCK * HEAD_DIM, NSA_CMP_HIDDEN), (NSA_CMP_BLOCK * HEAD_DIM) ** -0.5),
        'nsa_cv_w2': nrm(ks[16], (L, NSA_CMP_HIDDEN, HEAD_DIM), NSA_CMP_HIDDEN ** -0.5),
        'w_branch': nrm(ks[17], (L, N_BRANCH, BRANCH_WIDTH, D), BRANCH_WIDTH ** -0.5),
        'w_out': nrm(ks[18], (L, D, D), D ** -0.5 * DEEPNORM_BETA),
        'ln1_g': 1.0 + nrm(ks[19], (L, D), 0.02),
        'ln1_b': nrm(ks[20], (L, D), 0.02),
        'router_w': nrm(ks[21], (L, D, N_EXPERTS), D ** -0.5),
        'router_b': nrm(ks[22], (L, N_EXPERTS), 0.01),
        'exp_w1': nrm(ks[23], (L, N_EXPERTS, D, 2 * EXPERT_FF), D ** -0.5),
        'exp_b1': nrm(ks[24], (L, N_EXPERTS, 2 * EXPERT_FF), 0.01),
        'exp_w2': nrm(ks[25], (L, N_EXPERTS, EXPERT_FF, D), EXPERT_FF ** -0.5 * DEEPNORM_BETA),
        'exp_b2': nrm(ks[26], (L, N_EXPERTS, D), 0.01),
        'ln2_g': 1.0 + nrm(ks[27], (L, D), 0.02),
        'ln2_b': nrm(ks[28], (L, D), 0.02),
    }


def reference(x, rel_bias, w_in, gla_gate_w, gla_gate_b, gla_norm_g, mlstm_conv_w, mlstm_conv_b,
              mlstm_igate_b, mlstm_fgate_b, mlstm_norm_g, nsa_pe_k, nsa_pe_v, nsa_ck_w1, nsa_ck_w2,
              nsa_cv_w1, nsa_cv_w2, w_branch, w_out, ln1_g, ln1_b, router_w, router_b, exp_w1, exp_b1,
              exp_w2, exp_b2, ln2_g, ln2_b):
    out_dtype = x.dtype
    B, S, D = x.shape
    h = x.astype(jnp.float32)
    for l in range(DEPTH):
        u = (h @ w_in[l]).astype(jnp.float32)
        (gq, gk, gv, gr, ga, mqk, mv, mo, mi, mf, dq, dk, dv,
         nq, nkc, nvc, nks, nvs, nkw, nvw, ng, gates) = _split_in(u)
        y_a = gla_mixer(gq, gk, gv, gr, ga, gla_gate_w[l], gla_gate_b[l], gla_norm_g[l])
        y_b = mlstm_mixer(mqk, mv, mo, mi, mf, mlstm_conv_w[l], mlstm_conv_b[l], mlstm_igate_b[l],
                          mlstm_fgate_b[l], mlstm_norm_g[l])
        y_c = dilated_mixer(dq, dk, dv, rel_bias)
        y_d = nsa_mixer(nq, nkc, nvc, nks, nvs, nkw, nvw, ng, rel_bias, nsa_pe_k[l], nsa_pe_v[l],
                        nsa_ck_w1[l], nsa_ck_w2[l], nsa_cv_w1[l], nsa_cv_w2[l])
        ys = jnp.stack([y_a, y_b, y_c, y_d], axis=2)
        proj = jnp.einsum('bsnc,ncd->bsnd', ys, w_branch[l])
        merged = jnp.sum(jax.nn.sigmoid(gates.reshape(B, S, N_BRANCH, D)) * proj, axis=2)
        h = layer_norm(DEEPNORM_ALPHA * h + merged @ w_out[l], ln1_g[l], ln1_b[l])
        h = layer_norm(DEEPNORM_ALPHA * h + moe_ffn(h, router_w[l], router_b[l], exp_w1[l], exp_b1[l],
                                                     exp_w2[l], exp_b2[l]), ln2_g[l], ln2_b[l])
    return h.astype(out_dtype)
```

```python
import functools
import math

import numpy as np
import jax
import jax.numpy as jnp
from jax import lax
from jax.experimental import pallas as pl
from jax.experimental.pallas import tpu as pltpu

F32 = jnp.float32
BF16 = jnp.bfloat16
HIGHEST = lax.Precision.HIGHEST

N_LAYERS_FOR_DEEPNORM = 4
HEAD_DIM = 128
BRANCH_WIDTH = 1024
N_BRANCH = 4
N_HEADS = 8
DK = 64
CHUNK = 64
GLA_GATE_RANK = 16
GLA_TAU = 16.0
MLSTM_CONV = 4
MLSTM_IGATE_CAP = 15.0
DIL_PATTERNS = ((128, 1), (512, 4), (2048, 16))
DIL_BLOCK = 128
NSA_KV_GROUPS = 2
NSA_HPG = 4
NSA_CMP_BLOCK = 32
NSA_CMP_STRIDE = 16
NSA_CMP_HIDDEN = 256
NSA_SEL_BLOCK = 64
NSA_N_SEL = 16
NSA_WINDOW = 512
REL_BUCKETS = 32
REL_MAX_DIST = 2048
N_EXPERTS = 32
TOP_K = 4
EXPERT_FF = 128
SWIGLU_LIMIT = 7.0
SWIGLU_ALPHA = 1.702
DEEPNORM_ALPHA = (2 * N_LAYERS_FOR_DEEPNORM) ** 0.25
LN_EPS = 1e-5
NEG = -1e30
BIG = 1e9

LANES = 128
VMEM_LIMIT_BYTES = 56 * 1024 * 1024

U_GQ, U_GK, U_GV, U_GR = 0, 512, 1024, 2048
U_MQK, U_MV, U_MO = 3072, 4096, 5120
U_DQ, U_DK, U_DV = 6144, 9216, 10240
U_NQ = 11264
U_NKC, U_NVC, U_NKS, U_NVS, U_NKW, U_NVW = 12288, 12544, 12800, 13056, 13312, 13568
U_MISC = 13824
U_WIDTH = 14336
MISC_GA, MISC_MI, MISC_MF, MISC_NG = 0, 16, 24, 32

_SRC_SIZES = (512, 512, 1024, 1024, 16, 1024, 1024, 1024, 8, 8, 3072, 1024, 1024, 1024,
              256, 256, 256, 256, 256, 256, 24)
_SRC_OFF = np.concatenate([[0], np.cumsum(_SRC_SIZES)]).tolist()
SRC_GATES = _SRC_OFF[-1]


def _cparams(*sem):
    return pltpu.CompilerParams(dimension_semantics=sem, vmem_limit_bytes=VMEM_LIMIT_BYTES)


def _log_sigmoid(x):
    return jnp.minimum(x, 0.0) - jnp.log1p(jnp.exp(-jnp.abs(x)))


def _sigmoid(x):
    return 1.0 / (1.0 + jnp.exp(-x))


def _silu(x):
    return x * _sigmoid(x)


def _dot(a, b):
    return jnp.dot(a, b, preferred_element_type=F32)


def _dot_t(a, b):
    return lax.dot_general(a, b, (((1,), (1,)), ((), ())), preferred_element_type=F32)


def _tdot(a, b):
    return lax.dot_general(a, b, (((0,), (0,)), ((), ())), preferred_element_type=F32)


def _head_norm(o, g_row):
    mu = jnp.mean(o, axis=-1, keepdims=True)
    d = o - mu
    var = jnp.mean(d * d, axis=-1, keepdims=True)
    return d * lax.rsqrt(var + LN_EPS) * g_row


def t5_bucket(dist):
    d = jnp.maximum(dist, 0)
    exact = REL_BUCKETS // 2
    df = jnp.maximum(d, 1).astype(jnp.float32)
    large = exact + (jnp.log(df / exact) / math.log(REL_MAX_DIST / exact) * (REL_BUCKETS - exact)).astype(jnp.int32)
    return jnp.where(d < exact, d, jnp.minimum(large, REL_BUCKETS - 1))


def _mm_body(x_ref, w_ref, o_ref):
    o_ref[...] = _dot(x_ref[...], w_ref[...]).astype(o_ref.dtype)


def matmul(x, w, out_dtype, tm, tn, name):
    m, k = x.shape
    n = w.shape[1]
    tm, tn = min(tm, m), min(tn, n)
    return pl.pallas_call(
        _mm_body,
        grid=(m // tm, n // tn),
        in_specs=[pl.BlockSpec((tm, k), lambda i, j: (i, 0)), pl.BlockSpec((k, tn), lambda i, j: (0, j))],
        out_specs=pl.BlockSpec((tm, tn), lambda i, j: (i, j)),
        out_shape=jax.ShapeDtypeStruct((m, n), out_dtype),
        compiler_params=_cparams("parallel", "arbitrary"),
        name=name,
    )(x, w)


def _ln_body(h_ref, d_ref, g_ref, b_ref, o_ref, ob_ref):
    z = DEEPNORM_ALPHA * h_ref[...] + d_ref[...]
    mu = jnp.mean(z, axis=-1, keepdims=True)
    zc = z - mu
    var = jnp.mean(zc * zc, axis=-1, keepdims=True)
    y = zc * lax.rsqrt(var + LN_EPS) * g_ref[...] + b_ref[...]
    o_ref[...] = y
    ob_ref[...] = y.astype(BF16)


def ln_residual(h, delta, g, b, name):
    n, d = h.shape
    tm = min(256, n)
    row = pl.BlockSpec((tm, d), lambda i: (i, 0))
    vec = pl.BlockSpec((1, d), lambda i: (0, 0))
    return pl.pallas_call(
        _ln_body,
        grid=(n // tm,),
        in_specs=[row, row, vec, vec],
        out_specs=[row, row],
        out_shape=[jax.ShapeDtypeStruct((n, d), F32), jax.ShapeDtypeStruct((n, d), BF16)],
        compiler_params=_cparams("parallel"),
        name=name,
    )(h, delta, g.reshape(1, d), b.reshape(1, d))


def _gla_body(q_ref, k_ref, v_ref, r_ref, misc_ref, gw_ref, gb_ref, ng_ref, o_ref, st_ref):
    L = CHUNK

    @pl.when(pl.program_id(1) == 0)
    def _():
        st_ref[...] = jnp.zeros_like(st_ref)

    pre = _dot(misc_ref[...].astype(BF16), gw_ref[...]) + gb_ref[...]
    log_a = _log_sigmoid(pre) / GLA_TAU
    row = lax.broadcasted_iota(jnp.int32, (L, L), 0)
    col = lax.broadcasted_iota(jnp.int32, (L, L), 1)
    causal = col <= row
    tri = jnp.where(causal, 1.0, 0.0).astype(F32)
    b = jnp.dot(tri, log_a, precision=HIGHEST, preferred_element_type=F32)
    b_last = b[L - 1:L, :]
    q_dec = (q_ref[...] * (DK ** -0.5) * jnp.exp(b)).astype(BF16)
    k_dec = (k_ref[...] * jnp.exp(-b)).astype(BF16)
    k_end = (k_ref[...] * jnp.exp(b_last - b)).astype(BF16)
    decay = jnp.exp(b_last)
    eye = col == row
    for h in range(N_HEADS):
        ks = slice(h * DK, (h + 1) * DK)
        vs = slice(h * HEAD_DIM, (h + 1) * HEAD_DIM)
        vh = v_ref[:, vs].astype(BF16)
        state = st_ref[h]
        att = jnp.where(causal, _dot_t(q_dec[:, ks], k_dec[:, ks]), 0.0)
        o = _dot(att.astype(BF16), vh) + _dot(q_dec[:, ks], state.astype(BF16))
        dec_col = jnp.sum(jnp.where(eye, decay[:, ks], 0.0), axis=1, keepdims=True)
        st_ref[h] = dec_col * state + _tdot(k_end[:, ks], vh)
        on = _head_norm(o, ng_ref[:, vs])
        o_ref[:, vs] = (_silu(r_ref[:, vs]) * on).astype(BF16)


def gla_mixer(u3, gate_w, gate_b, norm_g):
    bsz, s, _ = u3.shape
    L = CHUNK
    gw = jnp.zeros((LANES, N_HEADS * DK), F32).at[MISC_GA:MISC_GA + GLA_GATE_RANK].set(gate_w).astype(BF16)

    def ublock(width, off):
        return pl.BlockSpec((None, L, width), lambda b, t: (b, t, off // width))

    def const(shape):
        return pl.BlockSpec(shape, lambda b, t: (0,) * len(shape))

    return pl.pallas_call(
        _gla_body,
        grid=(bsz, s // L),
        in_specs=[ublock(512, U_GQ), ublock(512, U_GK), ublock(1024, U_GV), ublock(1024, U_GR),
                  ublock(LANES, U_MISC), const((LANES, 512)), const((1, 512)), const((1, 1024))],
        out_specs=pl.BlockSpec((None, L, 1024), lambda b, t: (b, t, 0)),
        out_shape=jax.ShapeDtypeStruct((bsz, s, BRANCH_WIDTH), BF16),
        scratch_shapes=[pltpu.VMEM((N_HEADS, DK, HEAD_DIM), F32)],
        compiler_params=_cparams("parallel", "arbitrary"),
        name="gla_mixer",
    )(u3, u3, u3, u3, u3, gw, gate_b.reshape(1, -1), norm_g.reshape(1, -1))


def _mlstm_body(qk_ref, v_ref, op_ref, misc_ref, cw_ref, cb_ref, ib_ref, fb_ref, ng_ref, o_ref,
                ext_ref, c_ref, n_ref, m_ref):
    L = CHUNK
    C2 = 2 * N_HEADS * DK

    @pl.when(pl.program_id(1) == 0)
    def _():
        ext_ref[0:8, :] = jnp.zeros((8, C2), F32)
        c_ref[...] = jnp.zeros_like(c_ref)
        n_ref[...] = jnp.zeros_like(n_ref)
        m_ref[...] = jnp.zeros_like(m_ref)

    x = qk_ref[...]
    ext_ref[8:8 + L, :] = x
    y = (cb_ref[...] + cw_ref[3:4, :] * x + cw_ref[2:3, :] * ext_ref[pl.ds(7, L), :]
         + cw_ref[1:2, :] * ext_ref[pl.ds(6, L), :] + cw_ref[0:1, :] * ext_ref[pl.ds(5, L), :])
    ext_ref[0:8, :] = x[L - 8:L, :]
    qk = _silu(y)
    q_all = qk[:, :N_HEADS * DK].astype(BF16)
    k_all = qk[:, N_HEADS * DK:] * (DK ** -0.5)

    misc = misc_ref[...]
    i_g = MLSTM_IGATE_CAP * jnp.tanh((misc + ib_ref[...]) / MLSTM_IGATE_CAP)
    log_f = _log_sigmoid(misc + fb_ref[...])
    row = lax.broadcasted_iota(jnp.int32, (L, L), 0)
    col = lax.broadcasted_iota(jnp.int32, (L, L), 1)
    causal = col <= row
    tri = jnp.where(causal, 1.0, 0.0).astype(F32)
    b_c = jnp.dot(tri, log_f, precision=HIGHEST, preferred_element_type=F32)
    b_r = b_c.T
    i_r = i_g.T

    for h in range(N_HEADS):
        ks = slice(h * DK, (h + 1) * DK)
        vs = slice(h * HEAD_DIM, (h + 1) * HEAD_DIM)
        q = q_all[:, ks]
        k = k_all[:, ks]
        v = v_ref[:, vs].astype(BF16)
        bcol = b_c[:, MISC_MF + h:MISC_MF + h + 1]
        icol = i_g[:, MISC_MI + h:MISC_MI + h + 1]
        brow = b_r[MISC_MF + h:MISC_MF + h + 1, :]
        irow = i_r[MISC_MI + h:MISC_MI + h + 1, :]
        b_last = bcol[L - 1:L, :]
        dmat = jnp.where(causal, bcol - brow + irow, -jnp.inf)
        a_end = b_last - bcol + icol
        m_loc = jnp.max(a_end, axis=0, keepdims=True)
        c_s = c_ref[h]
        n_s = n_ref[h, 0:1, 0:DK]
        m_s = m_ref[h, 0:1, 0:1]
        inter_log = bcol + m_s
        m_i = jnp.maximum(inter_log, jnp.max(dmat, axis=-1, keepdims=True))
        w_inter = jnp.exp(inter_log - m_i)
        qk_s = _dot_t(q, k.astype(BF16)) * jnp.exp(dmat - m_i)
        num = w_inter * _dot(q, c_s.astype(BF16)) + _dot(qk_s.astype(BF16), v)
        qn = jnp.sum(q.astype(F32) * n_s, axis=-1, keepdims=True)
        den = w_inter * qn + jnp.sum(qk_s, axis=-1, keepdims=True)
        hout = num / jnp.maximum(jnp.abs(den), jnp.exp(-m_i))
        m_new = jnp.maximum(b_last + m_s, m_loc)
        w = jnp.exp(a_end - m_new)
        sc = jnp.exp(b_last + m_s - m_new)
        wk = w * k
        c_ref[h] = sc * c_s + _tdot(wk.astype(BF16), v)
        n_ref[h, 0:1, 0:DK] = sc * n_s + jnp.sum(wk, axis=0, keepdims=True)
        m_ref[h] = jnp.broadcast_to(m_new, (8, LANES))
        hn = _head_norm(hout, ng_ref[:, vs])
        o_ref[:, vs] = (_sigmoid(op_ref[:, vs]) * hn).astype(BF16)


def mlstm_mixer(u3, conv_w, conv_b, igate_b, fgate_b, norm_g):
    bsz, s, _ = u3.shape
    L = CHUNK
    ib = jnp.zeros((1, LANES), F32).at[0, MISC_MI:MISC_MI + N_HEADS].set(igate_b)
    fb = jnp.zeros((1, LANES), F32).at[0, MISC_MF:MISC_MF + N_HEADS].set(fgate_b)

    def ublock(width, off):
        return pl.BlockSpec((None, L, width), lambda b, t: (b, t, off // width))

    def const(shape):
        return pl.BlockSpec(shape, lambda b, t: (0,) * len(shape))

    return pl.pallas_call(
        _mlstm_body,
        grid=(bsz, s // L),
        in_specs=[ublock(1024, U_MQK), ublock(1024, U_MV), ublock(1024, U_MO), ublock(LANES, U_MISC),
                  const((MLSTM_CONV, 1024)), const((1, 1024)), const((1, LANES)), const((1, LANES)),
                  const((1, 1024))],
        out_specs=pl.BlockSpec((None, L, 1024), lambda b, t: (b, t, 0)),
        out_shape=jax.ShapeDtypeStruct((bsz, s, BRANCH_WIDTH), BF16),
        scratch_shapes=[pltpu.VMEM((L + 8, 1024), F32), pltpu.VMEM((N_HEADS, DK, HEAD_DIM), F32),
                        pltpu.VMEM((N_HEADS, 8, LANES), F32), pltpu.VMEM((N_HEADS, 8, LANES), F32)],
        compiler_params=_cparams("parallel", "arbitrary"),
        name="mlstm_mixer",
    )(u3, u3, u3, u3, conv_w, conv_b.reshape(1, -1), ib, fb, norm_g.reshape(1, -1))


def _dil_body(q_ref, kp_ref, kc_ref, vp_ref, vc_ref, bias_ref, o_ref, lse_ref):
    P = DIL_BLOCK
    n = pl.program_id(2)
    row = lax.broadcasted_iota(jnp.int32, (P, P), 0)
    col = lax.broadcasted_iota(jnp.int32, (P, P), 1)
    mask_prev = (col >= row) & (n > 0)
    mask_cur = col <= row
    lane = lax.broadcasted_iota(jnp.int32, (P, LANES), 1)
    lse_tile = jnp.zeros((P, LANES), F32)
    for h in range(N_HEADS):
        hs = slice(h * HEAD_DIM, (h + 1) * HEAD_DIM)
        q = (q_ref[:, hs] * (HEAD_DIM ** -0.5)).astype(BF16)
        s_p = _dot_t(q, kp_ref[:, hs].astype(BF16)) + bias_ref[h, :, 0:P]
        s_c = _dot_t(q, kc_ref[:, hs].astype(BF16)) + bias_ref[h, :, P:2 * P]
        s_p = jnp.where(mask_prev, s_p, NEG)
        s_c = jnp.where(mask_cur, s_c, NEG)
        m = jnp.maximum(jnp.max(s_p, axis=-1, keepdims=True), jnp.max(s_c, axis=-1, keepdims=True))
        p_p = jnp.exp(s_p - m)
        p_c = jnp.exp(s_c - m)
        l = jnp.sum(p_p, axis=-1, keepdims=True) + jnp.sum(p_c, axis=-1, keepdims=True)
        o = _dot(p_p.astype(BF16), vp_ref[:, hs].astype(BF16)) + _dot(p_c.astype(BF16), vc_ref[:, hs].astype(BF16))
        o_ref[:, hs] = o / l
        lse_tile = jnp.where(lane == h, m + jnp.log(l), lse_tile)
    lse_ref[...] = lse_tile


def _dil_group(u3, bias, g, dil):
    bsz, s, w = u3.shape
    P = DIL_BLOCK
    nb = s // (dil * P)
    ug = u3.reshape(bsz, s // dil, dil * w)
    wb = w // 1024

    def ub(off, prev):
        def idx(b, r, n):
            return (b, jnp.maximum(n - 1, 0) if prev else n, r * wb + off // 1024)
        return pl.BlockSpec((None, P, 1024), idx)

    o, lse = pl.pallas_call(
        _dil_body,
        grid=(bsz, dil, nb),
        in_specs=[ub(U_DQ + g * 1024, False), ub(U_DK, True), ub(U_DK, False), ub(U_DV, True), ub(U_DV, False),
                  pl.BlockSpec((N_HEADS, P, 2 * P), lambda b, r, n: (0, 0, 0))],
        out_specs=[pl.BlockSpec((None, P, 1024), lambda b, r, n: (b, n, r)),
                   pl.BlockSpec((None, P, LANES), lambda b, r, n: (b, n, r))],
        out_shape=[jax.ShapeDtypeStruct((bsz, s // dil, dil * 1024), F32),
                   jax.ShapeDtypeStruct((bsz, s // dil, dil * LANES), F32)],
        compiler_params=_cparams("parallel", "parallel", "arbitrary"),
        name=f"dilated_attn_g{g}",
    )(ug, ug, ug, ug, ug, bias)
    return o.reshape(bsz * s, 1024), lse.reshape(bsz * s, LANES)


def _dil_combine_body(o0, o1, o2, l0, l1, l2, y_ref):
    a0, a1, a2 = l0[...], l1[...], l2[...]
    m = jnp.maximum(jnp.maximum(a0, a1), a2)
    e0, e1, e2 = jnp.exp(a0 - m), jnp.exp(a1 - m), jnp.exp(a2 - m)
    den = e0 + e1 + e2
    w0, w1, w2 = e0 / den, e1 / den, e2 / den
    for h in range(N_HEADS):
        hs = slice(h * HEAD_DIM, (h + 1) * HEAD_DIM)
        y = w0[:, h:h + 1] * o0[:, hs] + w1[:, h:h + 1] * o1[:, hs] + w2[:, h:h + 1] * o2[:, hs]
        y_ref[:, hs] = y.astype(BF16)


def dilated_mixer(u3, rel_bias):
    bsz, s, _ = u3.shape
    P = DIL_BLOCK
    outs, lses = [], []
    delta = jnp.arange(P)[:, None] + P - jnp.arange(2 * P)[None, :]
    for g, (window, dil) in enumerate(DIL_PATTERNS):
        assert window // dil == P and s % (dil * P) == 0
        bias = rel_bias[t5_bucket(delta * dil), g * N_HEADS:(g + 1) * N_HEADS].transpose(2, 0, 1)
        o, lse = _dil_group(u3, bias, g, dil)
        outs.append(o)
        lses.append(lse)
    n = bsz * s
    tm = min(512, n)
    ob = pl.BlockSpec((tm, 1024), lambda i: (i, 0))
    lb = pl.BlockSpec((tm, LANES), lambda i: (i, 0))
    return pl.pallas_call(
        _dil_combine_body,
        grid=(n // tm,),
        in_specs=[ob, ob, ob, lb, lb, lb],
        out_specs=ob,
        out_shape=jax.ShapeDtypeStruct((n, BRANCH_WIDTH), BF16),
        compiler_params=_cparams("parallel"),
        name="dilated_combine",
    )(*outs, *lses)


def _gelu_tanh(x):
    return 0.5 * x * (1.0 + jnp.tanh(math.sqrt(2.0 / math.pi) * (x + 0.044715 * (x * x * x))))


def _nsa_compress_body(t_ref, pe_ref, w1_ref, w2_ref, o_ref):
    nblk = t_ref.shape[0] // NSA_CMP_STRIDE
    half = NSA_CMP_STRIDE
    t1 = jnp.zeros((nblk, NSA_CMP_HIDDEN), F32)
    t2 = jnp.zeros((nblk, NSA_CMP_HIDDEN), F32)
    for p in range(half):
        xp = t_ref[pl.ds(p, nblk, stride=NSA_CMP_STRIDE), :]
        t1 = t1 + _dot((xp + pe_ref[p:p + 1, :]).astype(BF16), w1_ref[p * HEAD_DIM:(p + 1) * HEAD_DIM, :])
        t2 = t2 + _dot((xp + pe_ref[half + p:half + p + 1, :]).astype(BF16),
                       w1_ref[(half + p) * HEAD_DIM:(half + p + 1) * HEAD_DIM, :])
    hidden = t1 + pltpu.roll(t2, nblk - 1, 0)
    o_ref[...] = _dot(_gelu_tanh(hidden).astype(BF16), w2_ref[...])


def _nsa_compress(u3, off, pe, w1, w2, name):
    bsz, s, _ = u3.shape
    nblk = s // NSA_CMP_STRIDE
    G = NSA_KV_GROUPS
    return pl.pallas_call(
        _nsa_compress_body,
        grid=(bsz, G),
        in_specs=[pl.BlockSpec((None, s, HEAD_DIM), lambda b, g: (b, 0, off // HEAD_DIM + g)),
                  pl.BlockSpec((NSA_CMP_BLOCK, HEAD_DIM), lambda b, g: (0, 0)),
                  pl.BlockSpec((NSA_CMP_BLOCK * HEAD_DIM, NSA_CMP_HIDDEN), lambda b, g: (0, 0)),
                  pl.BlockSpec((NSA_CMP_HIDDEN, HEAD_DIM), lambda b, g: (0, 0))],
        out_specs=pl.BlockSpec((None, None, nblk, HEAD_DIM), lambda b, g: (b, g, 0, 0)),
        out_shape=jax.ShapeDtypeStruct((bsz, G, nblk, HEAD_DIM), F32),
        compiler_params=_cparams("parallel", "parallel"),
        name=name,
    )(u3, pe, w1.astype(BF16), w2.astype(BF16))


def _nsa_cmp_body(q_ref, kc_ref, vc_ref, bc_ref, agg_ref, oc_ref, sel_ref):
    TQ = q_ref.shape[0]
    ncb = kc_ref.shape[0]
    nsb = agg_ref.shape[1]
    t0 = pl.program_id(2) * TQ
    tpos = t0 + lax.broadcasted_iota(jnp.int32, (TQ, ncb), 0)
    ends = lax.broadcasted_iota(jnp.int32, (TQ, ncb), 1) * NSA_CMP_STRIDE + (NSA_CMP_BLOCK - 1)
    mask = ends <= tpos
    maskf = jnp.where(mask, 1.0, 0.0).astype(F32)
    kcb = kc_ref[...].astype(BF16)
    vcb = vc_ref[...].astype(BF16)
    psum = jnp.zeros((TQ, ncb), F32)
    for hg in range(NSA_HPG):
        hs = slice(hg * HEAD_DIM, (hg + 1) * HEAD_DIM)
        q = (q_ref[:, hs] * (HEAD_DIM ** -0.5)).astype(BF16)
        s = jnp.where(mask, _dot_t(q, kcb) + bc_ref[hg], NEG)
        p = jnp.exp(s - jnp.max(s, axis=-1, keepdims=True)) * maskf
        l = jnp.sum(p, axis=-1, keepdims=True)
        p = p / jnp.where(l > 0, l, 1.0)
        oc_ref[:, hs] = _dot(p.astype(BF16), vcb)
        psum = psum + p
    imp = jnp.dot(psum, agg_ref[...], precision=HIGHEST, preferred_element_type=F32)
    jblk = lax.broadcasted_iota(jnp.int32, (TQ, nsb), 1)
    tq = t0 + lax.broadcasted_iota(jnp.int32, (TQ, nsb), 0)
    cur = tq // NSA_SEL_BLOCK
    forced = (jblk == 0) | (jblk == cur) | (jblk == cur - 1)
    score = jnp.where(forced, BIG, jnp.where(jblk * NSA_SEL_BLOCK <= tq, imp, -BIG))
    jf = jblk.astype(F32)
    sel = jnp.zeros((TQ, nsb), F32)
    for _ in range(min(NSA_N_SEL, nsb)):
        mx = jnp.max(score, axis=-1, keepdims=True)
        first = jnp.min(jnp.where(score == mx, jf, float(nsb)), axis=-1, keepdims=True)
        pick = jf == first
        sel = jnp.where(pick, 1.0, sel)
        score = jnp.where(pick, -jnp.inf, score)
    sel_ref[...] = sel


def _sel_agg_matrix(n_cmp, n_sb):
    r = NSA_SEL_BLOCK // NSA_CMP_STRIDE
    c = NSA_CMP_BLOCK // NSA_CMP_STRIDE
    jj, aa, bb = np.meshgrid(np.arange(n_sb), np.arange(r), np.arange(c), indexing='ij')
    ii = r * jj + aa + bb - 1
    ok = (ii >= 0) & (ii < n_cmp)
    mat = np.zeros((n_cmp, n_sb), np.float32)
    np.add.at(mat, (ii[ok], jj[ok]), 1.0)
    return mat


def _nsa_selwin_body(q_ref, oc_ref, sel_ref, misc_ref, ks_ref, vs_ref, kw_ref, vw_ref, tt_ref, o_ref,
                     ksb_ref, vsb_ref, m_ref, l_ref, acc_ref):
    TQ = q_ref.shape[0]
    HG = NSA_HPG
    nsb = sel_ref.shape[1]
    g = pl.program_id(1)
    qi = pl.program_id(2)

    @pl.when(qi == 0)
    def _():
        ksb_ref[...] = ks_ref[...].astype(BF16)
        vsb_ref[...] = vs_ref[...].astype(BF16)

    q4 = jnp.concatenate([(q_ref[:, hg * HEAD_DIM:(hg + 1) * HEAD_DIM] * (HEAD_DIM ** -0.5)).astype(BF16)
                          for hg in range(HG)], axis=0)
    sel = sel_ref[...].astype(BF16)
    row = lax.broadcasted_iota(jnp.int32, (TQ, TQ), 0)
    col = lax.broadcasted_iota(jnp.int32, (TQ, TQ), 1)
    e_row = lax.broadcasted_iota(jnp.int32, (nsb, TQ), 0)
    e_col = lax.broadcasted_iota(jnp.int32, (nsb, TQ), 1) // NSA_SEL_BLOCK
    per_tile = TQ // NSA_SEL_BLOCK

    def flash_step(s4, mask4, v_tile):
        s4 = jnp.where(mask4, s4, NEG)
        m_old = m_ref[...]
        m_new = jnp.maximum(m_old, jnp.max(s4, axis=-1, keepdims=True))
        alpha = jnp.exp(m_old - m_new)
        p = jnp.where(mask4, jnp.exp(s4 - m_new), 0.0)
        l_ref[...] = alpha * l_ref[...] + jnp.sum(p, axis=-1, keepdims=True)
        acc_ref[...] = alpha * acc_ref[...] + _dot(p.astype(BF16), v_tile)
        m_ref[...] = m_new

    def reset():
        m_ref[...] = jnp.full_like(m_ref, NEG)
        l_ref[...] = jnp.zeros_like(l_ref)
        acc_ref[...] = jnp.zeros_like(acc_ref)

    reset()

    def sel_step(kj, carry):
        k0 = pl.multiple_of(kj * TQ, TQ)
        k_tile = ksb_ref[pl.ds(k0, TQ), :]
        v_tile = vsb_ref[pl.ds(k0, TQ), :]
        expand = jnp.where(e_row == kj * per_tile + e_col, 1.0, 0.0).astype(BF16)
        chosen = _dot(sel, expand) > 0.5
        mask = chosen & ((kj < qi) | (col <= row))
        s4 = _dot_t(q4, k_tile)
        bias4 = jnp.concatenate([tt_ref[hg, qi - kj] for hg in range(HG)], axis=0)
        mask4 = jnp.concatenate([mask] * HG, axis=0)
        flash_step(s4 + bias4, mask4, v_tile)
        return carry

    lax.fori_loop(0, qi + 1, sel_step, 0)
    o_sel = acc_ref[...] / l_ref[...]

    reset()
    nprev = NSA_WINDOW // TQ
    for wdx in range(nprev + 1):
        back = nprev - wdx
        kj = qi - back
        kjc = jnp.maximum(kj, 0)
        k0 = pl.multiple_of(kjc * TQ, TQ)
        k_tile = kw_ref[pl.ds(k0, TQ), :].astype(BF16)
        v_tile = vw_ref[pl.ds(k0, TQ), :].astype(BF16)
        delta = back * TQ + row - col
        mask = (delta >= 0) & (delta < NSA_WINDOW) & (kj >= 0)
        s4 = _dot_t(q4, k_tile)
        bias4 = jnp.concatenate([tt_ref[hg, back] for hg in range(HG)], axis=0)
        mask4 = jnp.concatenate([mask] * HG, axis=0)
        flash_step(s4 + bias4, mask4, v_tile)
    o_win = acc_ref[...] / l_ref[...]

    misc = misc_ref[...]
    gpre = jnp.where(g == 0, misc[:, MISC_NG:MISC_NG + 3 * HG], misc[:, MISC_NG + 3 * HG:MISC_NG + 6 * HG])
    gates = _sigmoid(gpre)
    for hg in range(HG):
        hs = slice(hg * HEAD_DIM, (hg + 1) * HEAD_DIM)
        rs = slice(hg * TQ, (hg + 1) * TQ)
        y = (gates[:, 3 * hg:3 * hg + 1] * oc_ref[:, hs] + gates[:, 3 * hg + 1:3 * hg + 2] * o_sel[rs]
             + gates[:, 3 * hg + 2:3 * hg + 3] * o_win[rs])
        o_ref[:, hs] = y.astype(BF16)


def nsa_mixer(u3, rel_bias, pe_k, pe_v, ck_w1, ck_w2, cv_w1, cv_w2):
    bsz, s, _ = u3.shape
    G, HG, dh = NSA_KV_GROUPS, NSA_HPG, HEAD_DIM
    TQ = 128
    nqt = s // TQ
    assert NSA_KV_GROUPS == 2 and s % TQ == 0
    nblk = s // NSA_CMP_STRIDE
    n_cmp = (s - NSA_CMP_BLOCK) // NSA_CMP_STRIDE + 1
    n_sb = s // NSA_SEL_BLOCK

    tab = rel_bias[:, 3 * N_HEADS:]
    by_dist = tab[t5_bucket(jnp.arange(s))].T
    ii = np.arange(TQ)
    tt_idx = np.clip(np.arange(nqt)[:, None, None] * TQ + ii[None, :, None] - ii[None, None, :], 0, s - 1)
    tt = by_dist[:, tt_idx]
    bc_idx = np.clip(np.arange(s)[:, None] - (np.arange(nblk)[None, :] * NSA_CMP_STRIDE + NSA_CMP_BLOCK - 1), 0, s - 1)
    bc = by_dist[:, bc_idx]
    agg = np.zeros((nblk, n_sb), np.float32)
    agg[:n_cmp] = _sel_agg_matrix(n_cmp, n_sb)
    agg = jnp.asarray(agg)

    k_cmp = _nsa_compress(u3, U_NKC, pe_k, ck_w1, ck_w2, "nsa_compress_k")
    v_cmp = _nsa_compress(u3, U_NVC, pe_v, cv_w1, cv_w2, "nsa_compress_v")

    qspec = pl.BlockSpec((None, TQ, HG * dh), lambda b, g, i: (b, i, U_NQ // (HG * dh) + g))
    o_cmp, sel = pl.pallas_call(
        _nsa_cmp_body,
        grid=(bsz, G, nqt),
        in_specs=[qspec,
                  pl.BlockSpec((None, None, nblk, dh), lambda b, g, i: (b, g, 0, 0)),
                  pl.BlockSpec((None, None, nblk, dh), lambda b, g, i: (b, g, 0, 0)),
                  pl.BlockSpec((HG, TQ, nblk), lambda b, g, i: (g, i, 0)),
                  pl.BlockSpec((nblk, n_sb), lambda b, g, i: (0, 0))],
        out_specs=[pl.BlockSpec((None, TQ, HG * dh), lambda b, g, i: (b, i, g)),
                   pl.BlockSpec((None, None, TQ, n_sb), lambda b, g, i: (b, g, i, 0))],
        out_shape=[jax.ShapeDtypeStruct((bsz, s, G * HG * dh), F32),
                   jax.ShapeDtypeStruct((bsz, G, s, n_sb), F32)],
        compiler_params=_cparams("parallel", "parallel", "arbitrary"),
        name="nsa_compressed_attn",
    )(u3, k_cmp, v_cmp, bc, agg)

    def kv(off):
        return pl.BlockSpec((None, s, dh), lambda b, g, i: (b, 0, off // dh + g))

    y = pl.pallas_call(
        _nsa_selwin_body,
        grid=(bsz, G, nqt),
        in_specs=[qspec,
                  pl.BlockSpec((None, TQ, HG * dh), lambda b, g, i: (b, i, g)),
                  pl.BlockSpec((None, None, TQ, n_sb), lambda b, g, i: (b, g, i, 0)),
                  pl.BlockSpec((None, TQ, LANES), lambda b, g, i: (b, i, U_MISC // LANES)),
                  kv(U_NKS), kv(U_NVS), kv(U_NKW), kv(U_NVW),
                  pl.BlockSpec((HG, nqt, TQ, TQ), lambda b, g, i: (g, 0, 0, 0))],
        out_specs=pl.BlockSpec((None, TQ, HG * dh), lambda b, g, i: (b, i, g)),
        out_shape=jax.ShapeDtypeStruct((bsz, s, BRANCH_WIDTH), BF16),
        scratch_shapes=[pltpu.VMEM((s, dh), BF16), pltpu.VMEM((s, dh), BF16),
                        pltpu.VMEM((HG * TQ, 1), F32), pltpu.VMEM((HG * TQ, 1), F32),
                        pltpu.VMEM((HG * TQ, dh), F32)],
        compiler_params=_cparams("parallel", "parallel", "arbitrary"),
        name="nsa_selected_window_attn",
    )(u3, o_cmp, sel, u3, u3, u3, u3, u3, tt)
    return y


def _merge_body(x_ref, wg0, wg1, wg2, wg3, y0, y1, y2, y3, wb_ref, o_ref):
    x = x_ref[...]
    acc = None
    for b, (wg, y) in enumerate(((wg0, y0), (wg1, y1), (wg2, y2), (wg3, y3))):
        gate = _sigmoid(_dot(x, wg[...]))
        term = gate * _dot(y[...], wb_ref[b])
        acc = term if acc is None else acc + term
    o_ref[...] = acc.astype(o_ref.dtype)


def merge_branches(hb, w_gates, ys, w_branch):
    n, d = hb.shape
    tm, tn = min(512, n), min(256, d)
    nj = d // tn

    def wg(b):
        return pl.BlockSpec((d, tn), lambda i, j: (0, b * nj + j))

    yspec = pl.BlockSpec((tm, BRANCH_WIDTH), lambda i, j: (i, 0))
    return pl.pallas_call(
        _merge_body,
        grid=(n // tm, nj),
        in_specs=[pl.BlockSpec((tm, d), lambda i, j: (i, 0)), wg(0), wg(1), wg(2), wg(3),
                  yspec, yspec, yspec, yspec,
                  pl.BlockSpec((N_BRANCH, BRANCH_WIDTH, tn), lambda i, j: (0, 0, j))],
        out_specs=pl.BlockSpec((tm, tn), lambda i, j: (i, j)),
        out_shape=jax.ShapeDtypeStruct((n, d), BF16),
        compiler_params=_cparams("parallel", "arbitrary"),
        name="merge_branches",
    )(hb, w_gates, w_gates, w_gates, w_gates, *ys, w_branch)


MOE_TN = 512
EXPERTS_PER_TILE = MOE_TN // EXPERT_FF


def _router_body(x_ref, w_ref, b_ref, comb_ref, combt_ref):
    logits = _dot(x_ref[...], w_ref[...]) + b_ref[...]
    lane = lax.broadcasted_iota(jnp.int32, logits.shape, 1).astype(F32)
    work = logits
    picks, vals = [], []
    for _ in range(TOP_K):
        mx = jnp.max(work, axis=-1, keepdims=True)
        first = jnp.min(jnp.where(work == mx, lane, float(LANES)), axis=-1, keepdims=True)
        pick = lane == first
        picks.append(pick)
        vals.append(mx)
        work = jnp.where(pick, -jnp.inf, work)
    exps = [jnp.exp(v - vals[0]) for v in vals]
    den = exps[0]
    for e in exps[1:]:
        den = den + e
    comb = jnp.zeros_like(logits)
    for pick, e in zip(picks, exps):
        comb = comb + jnp.where(pick, e / den, 0.0)
    comb_ref[...] = comb
    for t in range(N_EXPERTS // EXPERTS_PER_TILE):
        combt_ref[t] = comb[:, t * EXPERTS_PER_TILE:(t + 1) * EXPERTS_PER_TILE]


def moe_router(hb, router_w, router_b):
    n, d = hb.shape
    tm = min(512, n)
    w = jnp.zeros((d, LANES), F32).at[:, :N_EXPERTS].set(router_w).astype(BF16)
    b = jnp.full((1, LANES), NEG, F32).at[0, :N_EXPERTS].set(router_b)
    nt = N_EXPERTS // EXPERTS_PER_TILE
    return pl.pallas_call(
        _router_body,
        grid=(n // tm,),
        in_specs=[pl.BlockSpec((tm, d), lambda i: (i, 0)), pl.BlockSpec((d, LANES), lambda i: (0, 0)),
                  pl.BlockSpec((1, LANES), lambda i: (0, 0))],
        out_specs=[pl.BlockSpec((tm, LANES), lambda i: (i, 0)),
                   pl.BlockSpec((nt, tm, EXPERTS_PER_TILE), lambda i: (0, i, 0))],
        out_shape=[jax.ShapeDtypeStruct((n, LANES), F32), jax.ShapeDtypeStruct((nt, n, EXPERTS_PER_TILE), F32)],
        compiler_params=_cparams("parallel"),
        name="moe_router",
    )(hb, w, b)


def _moe_up_body(x_ref, wg_ref, wu_ref, bg_ref, bu_ref, comb_ref, a_ref):
    x = x_ref[...]
    gate = jnp.minimum(_dot(x, wg_ref[...]) + bg_ref[...], SWIGLU_LIMIT)
    up = jnp.clip(_dot(x, wu_ref[...]) + bu_ref[...], -SWIGLU_LIMIT, SWIGLU_LIMIT)
    act = gate * _sigmoid(SWIGLU_ALPHA * gate) * (up + 1.0)
    for e in range(EXPERTS_PER_TILE):
        es = slice(e * EXPERT_FF, (e + 1) * EXPERT_FF)
        a_ref[:, es] = (act[:, es] * comb_ref[:, e:e + 1]).astype(BF16)


def moe_up(hb, w_gate, w_up, b_gate, b_up, comb_t):
    n, d = hb.shape
    tm = min(512, n)
    width = N_EXPERTS * EXPERT_FF
    wspec = pl.BlockSpec((d, MOE_TN), lambda i, j: (0, j))
    bspec = pl.BlockSpec((1, MOE_TN), lambda i, j: (0, j))
    return pl.pallas_call(
        _moe_up_body,
        grid=(n // tm, width // MOE_TN),
        in_specs=[pl.BlockSpec((tm, d), lambda i, j: (i, 0)), wspec, wspec, bspec, bspec,
                  pl.BlockSpec((None, tm, EXPERTS_PER_TILE), lambda i, j: (j, i, 0))],
        out_specs=pl.BlockSpec((tm, MOE_TN), lambda i, j: (i, j)),
        out_shape=jax.ShapeDtypeStruct((n, width), BF16),
        compiler_params=_cparams("parallel", "arbitrary"),
        name="moe_up",
    )(hb, w_gate, w_up, b_gate, b_up, comb_t)


def _moe_down_body(a_ref, w_ref, comb_ref, b2_ref, o_ref):
    o_ref[...] = _dot(a_ref[...], w_ref[...]) + _dot(comb_ref[...].astype(BF16), b2_ref[...])


def moe_down(a, w2, comb, b2):
    n, k = a.shape
    d = w2.shape[1]
    tm, tn = min(1024, n), min(512, d)
    return pl.pallas_call(
        _moe_down_body,
        grid=(n // tm, d // tn),
        in_specs=[pl.BlockSpec((tm, k), lambda i, j: (i, 0)), pl.BlockSpec((k, tn), lambda i, j: (0, j)),
                  pl.BlockSpec((tm, LANES), lambda i, j: (i, 0)), pl.BlockSpec((LANES, tn), lambda i, j: (0, j))],
        out_specs=pl.BlockSpec((tm, tn), lambda i, j: (i, j)),
        out_shape=jax.ShapeDtypeStruct((n, d), F32),
        compiler_params=_cparams("parallel", "arbitrary"),
        name="moe_down",
    )(a, w2, comb, b2)


def _regroup_w_in(w):
    d = w.shape[0]
    o = _SRC_OFF
    pad = U_WIDTH - U_MISC - (GLA_GATE_RANK + 2 * N_HEADS + 3 * N_HEADS)
    w_u = jnp.concatenate([
        w[:, o[0]:o[4]],
        w[:, o[5]:o[8]],
        w[:, o[10]:o[13]],
        w[:, o[13]:o[20]],
        w[:, o[4]:o[5]], w[:, o[8]:o[10]], w[:, o[20]:o[21]],
        jnp.zeros((d, pad), w.dtype),
    ], axis=1).astype(BF16)
    return w_u, w[:, SRC_GATES:].astype(BF16)


def kernel(x, rel_bias, w_in, gla_gate_w, gla_gate_b, gla_norm_g, mlstm_conv_w, mlstm_conv_b, mlstm_igate_b,
           mlstm_fgate_b, mlstm_norm_g, nsa_pe_k, nsa_pe_v, nsa_ck_w1, nsa_ck_w2, nsa_cv_w1, nsa_cv_w2, w_branch,
           w_out, ln1_g, ln1_b, router_w, router_b, exp_w1, exp_b1, exp_w2, exp_b2, ln2_g, ln2_b):
    out_dtype = x.dtype
    bsz, s, d = x.shape
    n = bsz * s
    h = x.astype(F32).reshape(n, d)
    hb = h.astype(BF16)
    for l in range(w_in.shape[0]):
        w_u, w_gates = _regroup_w_in(w_in[l])
        u = matmul(hb, w_u, F32, 1024, 512, "input_projection")
        u3 = u.reshape(bsz, s, U_WIDTH)
        y_a = gla_mixer(u3, gla_gate_w[l], gla_gate_b[l], gla_norm_g[l]).reshape(n, BRANCH_WIDTH)
        y_b = mlstm_mixer(u3, mlstm_conv_w[l], mlstm_conv_b[l], mlstm_igate_b[l], mlstm_fgate_b[l],
                          mlstm_norm_g[l]).reshape(n, BRANCH_WIDTH)
        y_c = dilated_mixer(u3, rel_bias)
        y_d = nsa_mixer(u3, rel_bias, nsa_pe_k[l], nsa_pe_v[l], nsa_ck_w1[l], nsa_ck_w2[l], nsa_cv_w1[l],
                        nsa_cv_w2[l]).reshape(n, BRANCH_WIDTH)
        merged = merge_branches(hb, w_gates, (y_a, y_b, y_c, y_d), w_branch[l].astype(BF16))
        attn = matmul(merged, w_out[l].astype(BF16), F32, 1024, 512, "output_projection")
        h, hb = ln_residual(h, attn, ln1_g[l], ln1_b[l], "layer_norm_1")

        comb, comb_t = moe_router(hb, router_w[l], router_b[l])
        w1 = exp_w1[l]
        w_gate = w1[:, :, :EXPERT_FF].transpose(1, 0, 2).reshape(d, N_EXPERTS * EXPERT_FF).astype(BF16)
        w_up = w1[:, :, EXPERT_FF:].transpose(1, 0, 2).reshape(d, N_EXPERTS * EXPERT_FF).astype(BF16)
        b_gate = exp_b1[l][:, :EXPERT_FF].reshape(1, -1)
        b_up = exp_b1[l][:, EXPERT_FF:].reshape(1, -1)
        act = moe_up(hb, w_gate, w_up, b_gate, b_up, comb_t)
        b2 = jnp.zeros((LANES, d), F32).at[:N_EXPERTS].set(exp_b2[l]).astype(BF16)
        ffn = moe_down(act, exp_w2[l].reshape(N_EXPERTS * EXPERT_FF, d).astype(BF16), comb, b2)
        h, hb = ln_residual(h, ffn, ln2_g[l], ln2_b[l], "layer_norm_2")
    return h.reshape(bsz, s, d).astype(out_dtype)
```

```python
import functools
import math

import numpy as np
import jax
import jax.numpy as jnp
from jax import lax
from jax.experimental import pallas as pl
from jax.experimental.pallas import tpu as pltpu

F32 = jnp.float32
BF16 = jnp.bfloat16
HIGHEST = lax.Precision.HIGHEST

N_LAYERS_FOR_DEEPNORM = 4
HEAD_DIM = 128
BRANCH_WIDTH = 1024
N_BRANCH = 4
N_HEADS = 8
DK = 64
CHUNK = 64
GLA_GATE_RANK = 16
GLA_TAU = 16.0
MLSTM_CONV = 4
MLSTM_IGATE_CAP = 15.0
DIL_PATTERNS = ((128, 1), (512, 4), (2048, 16))
DIL_BLOCK = 128
NSA_KV_GROUPS = 2
NSA_HPG = 4
NSA_CMP_BLOCK = 32
NSA_CMP_STRIDE = 16
NSA_CMP_HIDDEN = 256
NSA_SEL_BLOCK = 64
NSA_N_SEL = 16
NSA_WINDOW = 512
NSA_SEL_TK = 512
REL_BUCKETS = 32
REL_MAX_DIST = 2048
N_EXPERTS = 32
TOP_K = 4
EXPERT_FF = 128
SWIGLU_LIMIT = 7.0
SWIGLU_ALPHA = 1.702
DEEPNORM_ALPHA = (2 * N_LAYERS_FOR_DEEPNORM) ** 0.25
LN_EPS = 1e-5
NEG = -1e30
BIG = 1e9

LANES = 128
VMEM_LIMIT_BYTES = 56 * 1024 * 1024

U_GQ, U_GK, U_GV, U_GR = 0, 512, 1024, 2048
U_MQK, U_MV, U_MO = 3072, 4096, 5120
U_DQ, U_DK, U_DV = 6144, 9216, 10240
U_NQ = 11264
U_NKC, U_NVC, U_NKS, U_NVS, U_NKW, U_NVW = 12288, 12544, 12800, 13056, 13312, 13568
U_MISC = 13824
U_WIDTH = 14336
MISC_GA, MISC_MI, MISC_MF, MISC_NG = 0, 16, 24, 32

_SRC_SIZES = (512, 512, 1024, 1024, 16, 1024, 1024, 1024, 8, 8, 3072, 1024, 1024, 1024,
              256, 256, 256, 256, 256, 256, 24)
_SRC_OFF = np.concatenate([[0], np.cumsum(_SRC_SIZES)]).tolist()
SRC_GATES = _SRC_OFF[-1]


def _cparams(*sem):
    return pltpu.CompilerParams(dimension_semantics=sem, vmem_limit_bytes=VMEM_LIMIT_BYTES)


def _log_sigmoid(x):
    return jnp.minimum(x, 0.0) - jnp.log1p(jnp.exp(-jnp.abs(x)))


def _sigmoid(x):
    return 1.0 / (1.0 + jnp.exp(-x))


def _silu(x):
    return x * _sigmoid(x)


def _dot(a, b):
    return jnp.dot(a, b, preferred_element_type=F32)


def _dot_t(a, b):
    return lax.dot_general(a, b, (((1,), (1,)), ((), ())), preferred_element_type=F32)


def _tdot(a, b):
    return lax.dot_general(a, b, (((0,), (0,)), ((), ())), preferred_element_type=F32)


def _head_norm(o, g_row):
    mu = jnp.mean(o, axis=-1, keepdims=True)
    d = o - mu
    var = jnp.mean(d * d, axis=-1, keepdims=True)
    return d * lax.rsqrt(var + LN_EPS) * g_row


def t5_bucket(dist):
    d = jnp.maximum(dist, 0)
    exact = REL_BUCKETS // 2
    df = jnp.maximum(d, 1).astype(jnp.float32)
    large = exact + (jnp.log(df / exact) / math.log(REL_MAX_DIST / exact) * (REL_BUCKETS - exact)).astype(jnp.int32)
    return jnp.where(d < exact, d, jnp.minimum(large, REL_BUCKETS - 1))


def _mm_body(x_ref, w_ref, o_ref):
    o_ref[...] = _dot(x_ref[...], w_ref[...]).astype(o_ref.dtype)


def matmul(x, w, out_dtype, tm, tn, name):
    m, k = x.shape
    n = w.shape[1]
    tm, tn = min(tm, m), min(tn, n)
    return pl.pallas_call(
        _mm_body,
        grid=(m // tm, n // tn),
        in_specs=[pl.BlockSpec((tm, k), lambda i, j: (i, 0)), pl.BlockSpec((k, tn), lambda i, j: (0, j))],
        out_specs=pl.BlockSpec((tm, tn), lambda i, j: (i, j)),
        out_shape=jax.ShapeDtypeStruct((m, n), out_dtype),
        compiler_params=_cparams("parallel", "arbitrary"),
        name=name,
    )(x, w)


def _ln_body(h_ref, d_ref, g_ref, b_ref, o_ref, ob_ref):
    z = DEEPNORM_ALPHA * h_ref[...] + d_ref[...]
    mu = jnp.mean(z, axis=-1, keepdims=True)
    zc = z - mu
    var = jnp.mean(zc * zc, axis=-1, keepdims=True)
    y = zc * lax.rsqrt(var + LN_EPS) * g_ref[...] + b_ref[...]
    o_ref[...] = y
    ob_ref[...] = y.astype(BF16)


def ln_residual(h, delta, g, b, name):
    n, d = h.shape
    tm = min(256, n)
    row = pl.BlockSpec((tm, d), lambda i: (i, 0))
    vec = pl.BlockSpec((1, d), lambda i: (0, 0))
    return pl.pallas_call(
        _ln_body,
        grid=(n // tm,),
        in_specs=[row, row, vec, vec],
        out_specs=[row, row],
        out_shape=[jax.ShapeDtypeStruct((n, d), F32), jax.ShapeDtypeStruct((n, d), BF16)],
        compiler_params=_cparams("parallel"),
        name=name,
    )(h, delta, g.reshape(1, d), b.reshape(1, d))


def _gla_body(q_ref, k_ref, v_ref, r_ref, misc_ref, gw_ref, gb_ref, ng_ref, o_ref, st_ref):
    L = CHUNK

    @pl.when(pl.program_id(1) == 0)
    def _():
        st_ref[...] = jnp.zeros_like(st_ref)

    pre = _dot(misc_ref[...].astype(BF16), gw_ref[...]) + gb_ref[...]
    log_a = _log_sigmoid(pre) / GLA_TAU
    row = lax.broadcasted_iota(jnp.int32, (L, L), 0)
    col = lax.broadcasted_iota(jnp.int32, (L, L), 1)
    causal = col <= row
    tri = jnp.where(causal, 1.0, 0.0).astype(F32)
    b = jnp.dot(tri, log_a, precision=HIGHEST, preferred_element_type=F32)
    b_last = b[L - 1:L, :]
    q_dec = (q_ref[...] * (DK ** -0.5) * jnp.exp(b)).astype(BF16)
    k_dec = (k_ref[...] * jnp.exp(-b)).astype(BF16)
    k_end = (k_ref[...] * jnp.exp(b_last - b)).astype(BF16)
    decay = jnp.exp(b_last)
    eye = col == row
    for h in range(N_HEADS):
        ks = slice(h * DK, (h + 1) * DK)
        vs = slice(h * HEAD_DIM, (h + 1) * HEAD_DIM)
        vh = v_ref[:, vs].astype(BF16)
        state = st_ref[h]
        att = jnp.where(causal, _dot_t(q_dec[:, ks], k_dec[:, ks]), 0.0)
        o = _dot(att.astype(BF16), vh) + _dot(q_dec[:, ks], state.astype(BF16))
        dec_col = jnp.sum(jnp.where(eye, decay[:, ks], 0.0), axis=1, keepdims=True)
        st_ref[h] = dec_col * state + _tdot(k_end[:, ks], vh)
        on = _head_norm(o, ng_ref[:, vs])
        o_ref[:, vs] = (_silu(r_ref[:, vs]) * on).astype(BF16)


def gla_mixer(u3, gate_w, gate_b, norm_g):
    bsz, s, _ = u3.shape
    L = CHUNK
    gw = jnp.zeros((LANES, N_HEADS * DK), F32).at[MISC_GA:MISC_GA + GLA_GATE_RANK].set(gate_w).astype(BF16)

    def ublock(width, off):
        return pl.BlockSpec((None, L, width), lambda b, t: (b, t, off // width))

    def const(shape):
        return pl.BlockSpec(shape, lambda b, t: (0,) * len(shape))

    return pl.pallas_call(
        _gla_body,
        grid=(bsz, s // L),
        in_specs=[ublock(512, U_GQ), ublock(512, U_GK), ublock(1024, U_GV), ublock(1024, U_GR),
                  ublock(LANES, U_MISC), const((LANES, 512)), const((1, 512)), const((1, 1024))],
        out_specs=pl.BlockSpec((None, L, 1024), lambda b, t: (b, t, 0)),
        out_shape=jax.ShapeDtypeStruct((bsz, s, BRANCH_WIDTH), BF16),
        scratch_shapes=[pltpu.VMEM((N_HEADS, DK, HEAD_DIM), F32)],
        compiler_params=_cparams("parallel", "arbitrary"),
        name="gla_mixer",
    )(u3, u3, u3, u3, u3, gw, gate_b.reshape(1, -1), norm_g.reshape(1, -1))


def _mlstm_body(qk_ref, v_ref, op_ref, misc_ref, cw_ref, cb_ref, ib_ref, fb_ref, ng_ref, o_ref,
                ext_ref, c_ref, n_ref, m_ref):
    L = CHUNK
    C2 = 2 * N_HEADS * DK

    @pl.when(pl.program_id(1) == 0)
    def _():
        ext_ref[0:8, :] = jnp.zeros((8, C2), F32)
        c_ref[...] = jnp.zeros_like(c_ref)
        n_ref[...] = jnp.zeros_like(n_ref)
        m_ref[...] = jnp.zeros_like(m_ref)

    x = qk_ref[...]
    ext_ref[8:8 + L, :] = x
    y = (cb_ref[...] + cw_ref[3:4, :] * x + cw_ref[2:3, :] * ext_ref[pl.ds(7, L), :]
         + cw_ref[1:2, :] * ext_ref[pl.ds(6, L), :] + cw_ref[0:1, :] * ext_ref[pl.ds(5, L), :])
    ext_ref[0:8, :] = x[L - 8:L, :]
    qk = _silu(y)
    q_all = qk[:, :N_HEADS * DK].astype(BF16)
    k_all = qk[:, N_HEADS * DK:] * (DK ** -0.5)

    misc = misc_ref[...]
    i_g = MLSTM_IGATE_CAP * jnp.tanh((misc + ib_ref[...]) / MLSTM_IGATE_CAP)
    log_f = _log_sigmoid(misc + fb_ref[...])
    row = lax.broadcasted_iota(jnp.int32, (L, L), 0)
    col = lax.broadcasted_iota(jnp.int32, (L, L), 1)
    causal = col <= row
    tri = jnp.where(causal, 1.0, 0.0).astype(F32)
    b_c = jnp.dot(tri, log_f, precision=HIGHEST, preferred_element_type=F32)
    b_r = b_c.T
    i_r = i_g.T

    for h in range(N_HEADS):
        ks = slice(h * DK, (h + 1) * DK)
        vs = slice(h * HEAD_DIM, (h + 1) * HEAD_DIM)
        q = q_all[:, ks]
        k = k_all[:, ks]
        v = v_ref[:, vs].astype(BF16)
        bcol = b_c[:, MISC_MF + h:MISC_MF + h + 1]
        icol = i_g[:, MISC_MI + h:MISC_MI + h + 1]
        brow = b_r[MISC_MF + h:MISC_MF + h + 1, :]
        irow = i_r[MISC_MI + h:MISC_MI + h + 1, :]
        b_last = bcol[L - 1:L, :]
        dmat = jnp.where(causal, bcol - brow + irow, -jnp.inf)
        a_end = b_last - bcol + icol
        m_loc = jnp.max(a_end, axis=0, keepdims=True)
        c_s = c_ref[h]
        n_s = n_ref[h, 0:1, 0:DK]
        m_s = m_ref[h, 0:1, 0:1]
        inter_log = bcol + m_s
        m_i = jnp.maximum(inter_log, jnp.max(dmat, axis=-1, keepdims=True))
        w_inter = jnp.exp(inter_log - m_i)
        qk_s = _dot_t(q, k.astype(BF16)) * jnp.exp(dmat - m_i)
        num = w_inter * _dot(q, c_s.astype(BF16)) + _dot(qk_s.astype(BF16), v)
        qn = jnp.sum(q.astype(F32) * n_s, axis=-1, keepdims=True)
        den = w_inter * qn + jnp.sum(qk_s, axis=-1, keepdims=True)
        hout = num / jnp.maximum(jnp.abs(den), jnp.exp(-m_i))
        m_new = jnp.maximum(b_last + m_s, m_loc)
        w = jnp.exp(a_end - m_new)
        sc = jnp.exp(b_last + m_s - m_new)
        wk = w * k
        c_ref[h] = sc * c_s + _tdot(wk.astype(BF16), v)
        n_ref[h, 0:1, 0:DK] = sc * n_s + jnp.sum(wk, axis=0, keepdims=True)
        m_ref[h] = jnp.broadcast_to(m_new, (8, LANES))
        hn = _head_norm(hout, ng_ref[:, vs])
        o_ref[:, vs] = (_sigmoid(op_ref[:, vs]) * hn).astype(BF16)


def mlstm_mixer(u3, conv_w, conv_b, igate_b, fgate_b, norm_g):
    bsz, s, _ = u3.shape
    L = CHUNK
    ib = jnp.zeros((1, LANES), F32).at[0, MISC_MI:MISC_MI + N_HEADS].set(igate_b)
    fb = jnp.zeros((1, LANES), F32).at[0, MISC_MF:MISC_MF + N_HEADS].set(fgate_b)

    def ublock(width, off):
        return pl.BlockSpec((None, L, width), lambda b, t: (b, t, off // width))

    def const(shape):
        return pl.BlockSpec(shape, lambda b, t: (0,) * len(shape))

    return pl.pallas_call(
        _mlstm_body,
        grid=(bsz, s // L),
        in_specs=[ublock(1024, U_MQK), ublock(1024, U_MV), ublock(1024, U_MO), ublock(LANES, U_MISC),
                  const((MLSTM_CONV, 1024)), const((1, 1024)), const((1, LANES)), const((1, LANES)),
                  const((1, 1024))],
        out_specs=pl.BlockSpec((None, L, 1024), lambda b, t: (b, t, 0)),
        out_shape=jax.ShapeDtypeStruct((bsz, s, BRANCH_WIDTH), BF16),
        scratch_shapes=[pltpu.VMEM((L + 8, 1024), F32), pltpu.VMEM((N_HEADS, DK, HEAD_DIM), F32),
                        pltpu.VMEM((N_HEADS, 8, LANES), F32), pltpu.VMEM((N_HEADS, 8, LANES), F32)],
        compiler_params=_cparams("parallel", "arbitrary"),
        name="mlstm_mixer",
    )(u3, u3, u3, u3, conv_w, conv_b.reshape(1, -1), ib, fb, norm_g.reshape(1, -1))


BIAS_DIST_RANGE = 4096


def bucket_thresholds(max_dist):
    lut = t5_bucket(jnp.arange(max_dist))
    return jnp.sum(lut[None, :] < jnp.arange(REL_BUCKETS)[:, None], axis=1).astype(jnp.int32)


def _bias_table_body(thr_ref, tab_ref, o_ref, *, kind):
    a = pl.program_id(0)
    c = pl.program_id(1)
    shape = o_ref.shape
    i = lax.broadcasted_iota(jnp.int32, shape, 0)
    j = lax.broadcasted_iota(jnp.int32, shape, 1)
    if kind == "dilated":
        dil = jnp.where(a == 0, DIL_PATTERNS[0][1], jnp.where(a == 1, DIL_PATTERNS[1][1], DIL_PATTERNS[2][1]))
        dist = (i + DIL_BLOCK - j) * dil
        head = a * N_HEADS + c
    elif kind == "toeplitz":
        dist = c * shape[0] + i - j
        head = 3 * N_HEADS + a
    else:
        dist = c * shape[0] + i - (j * NSA_CMP_STRIDE + NSA_CMP_BLOCK - 1)
        head = 3 * N_HEADS + a
    dist = jnp.maximum(dist, 0)
    acc = jnp.full(shape, tab_ref[head, 0], F32)
    for k in range(1, REL_BUCKETS):
        acc = jnp.where(dist >= thr_ref[k], tab_ref[head, k], acc)
    o_ref[...] = acc


def bias_table(thr, tab_t, kind, out_dims, block):
    smem = pl.BlockSpec(memory_space=pltpu.SMEM)
    return pl.pallas_call(
        functools.partial(_bias_table_body, kind=kind),
        grid=out_dims[:2] if kind != "compressed" else (out_dims[0], out_dims[1] // block[0]),
        in_specs=[smem, smem],
        out_specs=(pl.BlockSpec((None, None) + block, lambda a, c: (a, c, 0, 0)) if kind != "compressed"
                   else pl.BlockSpec((None,) + block, lambda a, c: (a, c, 0))),
        out_shape=jax.ShapeDtypeStruct(out_dims, F32),
        compiler_params=_cparams("parallel", "parallel"),
        name=f"bias_table_{kind}",
    )(thr, tab_t)


DIL_SPAN = 2048


def _dil_body(q0_ref, q1_ref, q2_ref, kp_ref, kc_ref, vp_ref, vc_ref, bias_ref, o_ref, m_scr, l_scr, acc_scr):
    P = DIL_BLOCK
    q_refs = (q0_ref, q1_ref, q2_ref)
    row = lax.broadcasted_iota(jnp.int32, (P, P), 0)
    col = lax.broadcasted_iota(jnp.int32, (P, P), 1)
    mask_cur = col <= row
    mask_prev = col >= row
    mask_prev_first = (col - row) >= jnp.where(pl.program_id(1) > 0, 0, P)

    def rows(start, dil):
        return pl.ds(start, P, stride=dil) if dil > 1 else pl.ds(start, P)

    def block(g, dil, start, first):
        sl = rows(start, dil)
        q = (q_refs[g][sl, :] * (HEAD_DIM ** -0.5)).astype(BF16)
        k_c = kc_ref[sl, :].astype(BF16)
        v_c = vc_ref[sl, :].astype(BF16)
        if first:
            psl = rows(start + DIL_SPAN - dil * P, dil)
            k_p, v_p, mp = kp_ref[psl, :].astype(BF16), vp_ref[psl, :].astype(BF16), mask_prev_first
        else:
            psl = rows(start - dil * P, dil)
            k_p, v_p, mp = kc_ref[psl, :].astype(BF16), vc_ref[psl, :].astype(BF16), mask_prev
        s_p = jnp.where(mp, _dot_t(q, k_p) + bias_ref[g, :, 0:P], NEG)
        s_c = jnp.where(mask_cur, _dot_t(q, k_c) + bias_ref[g, :, P:2 * P], NEG)
        m = jnp.maximum(jnp.max(s_p, axis=-1, keepdims=True), jnp.max(s_c, axis=-1, keepdims=True))
        p_p = jnp.exp(s_p - m)
        p_c = jnp.exp(s_c - m)
        l = jnp.sum(p_p, axis=-1, keepdims=True) + jnp.sum(p_c, axis=-1, keepdims=True)
        acc = _dot(p_p.astype(BF16), v_p) + _dot(p_c.astype(BF16), v_c)
        m = jnp.broadcast_to(m, (P, LANES))
        l = jnp.broadcast_to(l, (P, LANES))
        if g == 0:
            m_scr[sl, :], l_scr[sl, :], acc_scr[sl, :] = m, l, acc
        else:
            m_old = m_scr[sl, :]
            m_new = jnp.maximum(m_old, m)
            a_old, a_new = jnp.exp(m_old - m_new), jnp.exp(m - m_new)
            m_scr[sl, :] = m_new
            l_scr[sl, :] = a_old * l_scr[sl, :] + a_new * l
            acc_scr[sl, :] = a_old * acc_scr[sl, :] + a_new * acc

    for g, (_, dil) in enumerate(DIL_PATTERNS):
        nblk = DIL_SPAN // (dil * P)
        if dil == 1:
            for n in range(nblk):
                block(g, dil, n * P, n == 0)
        else:
            def per_residue(r, carry, g=g, dil=dil, nblk=nblk):
                for n in range(nblk):
                    block(g, dil, r + n * dil * P, n == 0)
                return carry
            lax.fori_loop(0, dil, per_residue, 0)
    o_ref[...] = (acc_scr[...] / l_scr[...]).astype(BF16)


def dilated_bias(thr, tab_t):
    P = DIL_BLOCK
    return bias_table(thr, tab_t, "dilated", (len(DIL_PATTERNS), N_HEADS, P, 2 * P), (P, 2 * P))


def dilated_mixer(u3, bias):
    bsz, s, _ = u3.shape
    P = DIL_BLOCK
    assert s % DIL_SPAN == 0 and all(w // d == P and DIL_SPAN % (d * P) == 0 for w, d in DIL_PATTERNS)

    def ub(off, prev):
        def idx(b, t, h):
            return (b, jnp.maximum(t - 1, 0) if prev else t, off // HEAD_DIM + h)
        return pl.BlockSpec((None, DIL_SPAN, HEAD_DIM), idx)

    return pl.pallas_call(
        _dil_body,
        grid=(bsz, s // DIL_SPAN, N_HEADS),
        in_specs=[ub(U_DQ, False), ub(U_DQ + 1024, False), ub(U_DQ + 2048, False),
                  ub(U_DK, True), ub(U_DK, False), ub(U_DV, True), ub(U_DV, False),
                  pl.BlockSpec((len(DIL_PATTERNS), None, P, 2 * P), lambda b, t, h: (0, h, 0, 0))],
        out_specs=pl.BlockSpec((None, DIL_SPAN, HEAD_DIM), lambda b, t, h: (b, t, h)),
        out_shape=jax.ShapeDtypeStruct((bsz, s, BRANCH_WIDTH), BF16),
        scratch_shapes=[pltpu.VMEM((DIL_SPAN, LANES), F32), pltpu.VMEM((DIL_SPAN, LANES), F32),
                        pltpu.VMEM((DIL_SPAN, HEAD_DIM), F32)],
        compiler_params=_cparams("parallel", "parallel", "arbitrary"),
        name="dilated_attn",
    )(u3, u3, u3, u3, u3, u3, u3, bias)


def _gelu_tanh(x):
    return 0.5 * x * (1.0 + jnp.tanh(math.sqrt(2.0 / math.pi) * (x + 0.044715 * (x * x * x))))


def _nsa_compress_body(t_ref, pe_ref, w1_ref, w2_ref, o_ref):
    nblk = t_ref.shape[0] // NSA_CMP_STRIDE
    half = NSA_CMP_STRIDE
    t1 = jnp.zeros((nblk, NSA_CMP_HIDDEN), F32)
    t2 = jnp.zeros((nblk, NSA_CMP_HIDDEN), F32)
    for p in range(half):
        xp = t_ref[pl.ds(p, nblk, stride=NSA_CMP_STRIDE), :]
        t1 = t1 + _dot((xp + pe_ref[p:p + 1, :]).astype(BF16), w1_ref[p * HEAD_DIM:(p + 1) * HEAD_DIM, :])
        t2 = t2 + _dot((xp + pe_ref[half + p:half + p + 1, :]).astype(BF16),
                       w1_ref[(half + p) * HEAD_DIM:(half + p + 1) * HEAD_DIM, :])
    hidden = t1 + pltpu.roll(t2, nblk - 1, 0)
    o_ref[...] = _dot(_gelu_tanh(hidden).astype(BF16), w2_ref[...])


def _nsa_compress(u3, off, pe, w1, w2, name):
    bsz, s, _ = u3.shape
    nblk = s // NSA_CMP_STRIDE
    G = NSA_KV_GROUPS
    return pl.pallas_call(
        _nsa_compress_body,
        grid=(bsz, G),
        in_specs=[pl.BlockSpec((None, s, HEAD_DIM), lambda b, g: (b, 0, off // HEAD_DIM + g)),
                  pl.BlockSpec((NSA_CMP_BLOCK, HEAD_DIM), lambda b, g: (0, 0)),
                  pl.BlockSpec((NSA_CMP_BLOCK * HEAD_DIM, NSA_CMP_HIDDEN), lambda b, g: (0, 0)),
                  pl.BlockSpec((NSA_CMP_HIDDEN, HEAD_DIM), lambda b, g: (0, 0))],
        out_specs=pl.BlockSpec((None, None, nblk, HEAD_DIM), lambda b, g: (b, g, 0, 0)),
        out_shape=jax.ShapeDtypeStruct((bsz, G, nblk, HEAD_DIM), F32),
        compiler_params=_cparams("parallel", "parallel"),
        name=name,
    )(u3, pe, w1.astype(BF16), w2.astype(BF16))


def _nsa_cmp_body(q_ref, kc_ref, vc_ref, bc_ref, agg_ref, oc_ref, sel_ref):
    TQ = q_ref.shape[0]
    ncb = kc_ref.shape[0]
    nsb = agg_ref.shape[1]
    t0 = pl.program_id(2) * TQ
    tpos = t0 + lax.broadcasted_iota(jnp.int32, (TQ, ncb), 0)
    ends = lax.broadcasted_iota(jnp.int32, (TQ, ncb), 1) * NSA_CMP_STRIDE + (NSA_CMP_BLOCK - 1)
    mask = ends <= tpos
    maskf = jnp.where(mask, 1.0, 0.0).astype(F32)
    kcb = kc_ref[...].astype(BF16)
    vcb = vc_ref[...].astype(BF16)
    psum = jnp.zeros((TQ, ncb), F32)
    for hg in range(NSA_HPG):
        hs = slice(hg * HEAD_DIM, (hg + 1) * HEAD_DIM)
        q = (q_ref[:, hs] * (HEAD_DIM ** -0.5)).astype(BF16)
        s = jnp.where(mask, _dot_t(q, kcb) + bc_ref[hg], NEG)
        p = jnp.exp(s - jnp.max(s, axis=-1, keepdims=True)) * maskf
        l = jnp.sum(p, axis=-1, keepdims=True)
        p = p / jnp.where(l > 0, l, 1.0)
        oc_ref[:, hs] = _dot(p.astype(BF16), vcb)
        psum = psum + p
    imp = jnp.dot(psum, agg_ref[...], precision=HIGHEST, preferred_element_type=F32)
    jblk = lax.broadcasted_iota(jnp.int32, (TQ, nsb), 1)
    tq = t0 + lax.broadcasted_iota(jnp.int32, (TQ, nsb), 0)
    cur = tq // NSA_SEL_BLOCK
    forced = (jblk == 0) | (jblk == cur) | (jblk == cur - 1)
    score = jnp.where(forced, BIG, jnp.where(jblk * NSA_SEL_BLOCK <= tq, imp, -BIG))
    jf = jblk.astype(F32)
    sel = jnp.zeros((TQ, nsb), F32)
    for _ in range(min(NSA_N_SEL, nsb)):
        mx = jnp.max(score, axis=-1, keepdims=True)
        first = jnp.min(jnp.where(score == mx, jf, float(nsb)), axis=-1, keepdims=True)
        pick = jf == first
        sel = jnp.where(pick, 1.0, sel)
        score = jnp.where(pick, -jnp.inf, score)
    sel_ref[...] = sel


def _sel_agg_matrix(n_cmp, n_sb):
    r = NSA_SEL_BLOCK // NSA_CMP_STRIDE
    c = NSA_CMP_BLOCK // NSA_CMP_STRIDE
    jj, aa, bb = np.meshgrid(np.arange(n_sb), np.arange(r), np.arange(c), indexing='ij')
    ii = r * jj + aa + bb - 1
    ok = (ii >= 0) & (ii < n_cmp)
    mat = np.zeros((n_cmp, n_sb), np.float32)
    np.add.at(mat, (ii[ok], jj[ok]), 1.0)
    return mat


def _nsa_selwin_body(q_ref, oc_ref, sel_ref, misc_ref, ks_ref, vs_ref, kw_ref, vw_ref, tt_ref, o_ref,
                     ksb_ref, vsb_ref, m_ref, l_ref, acc_ref):
    TQ = q_ref.shape[0]
    TK = NSA_SEL_TK
    HG = NSA_HPG
    nsb = sel_ref.shape[1]
    sub = TK // TQ
    g = pl.program_id(1)
    qi = pl.program_id(2)

    @pl.when(qi == 0)
    def _():
        ksb_ref[...] = ks_ref[...].astype(BF16)
        vsb_ref[...] = vs_ref[...].astype(BF16)

    qs = [(q_ref[:, hg * HEAD_DIM:(hg + 1) * HEAD_DIM] * (HEAD_DIM ** -0.5)).astype(BF16) for hg in range(HG)]
    sel = sel_ref[...].astype(BF16)
    tpos = qi * TQ + lax.broadcasted_iota(jnp.int32, (TQ, TK), 0)
    kcol = lax.broadcasted_iota(jnp.int32, (TQ, TK), 1)
    e_row = lax.broadcasted_iota(jnp.int32, (nsb, TK), 0)
    e_col = lax.broadcasted_iota(jnp.int32, (nsb, TK), 1) // NSA_SEL_BLOCK

    m_ref[...] = jnp.full_like(m_ref, 0.1 * NEG)
    l_ref[...] = jnp.zeros_like(l_ref)
    acc_ref[...] = jnp.zeros_like(acc_ref)

    def sel_step(kj, carry):
        k0 = pl.multiple_of(kj * TK, TK)
        k_tile = ksb_ref[pl.ds(k0, TK), :]
        v_tile = vsb_ref[pl.ds(k0, TK), :]
        expand = jnp.where(e_row == kj * (TK // NSA_SEL_BLOCK) + e_col, 1.0, 0.0).astype(BF16)
        chosen = _dot(sel, expand) > 0.5
        mask_add = jnp.where(chosen & (k0 + kcol <= tpos), 0.0, NEG)
        for hg in range(HG):
            rs = slice(hg * TQ, (hg + 1) * TQ)
            bias = jnp.concatenate([tt_ref[hg, jnp.maximum(qi - (kj * sub + w), 0)] for w in range(sub)], axis=1)
            s = _dot_t(qs[hg], k_tile) + bias + mask_add
            m_old = m_ref[rs, :]
            m_new = jnp.maximum(m_old, jnp.max(s, axis=-1, keepdims=True))
            alpha = jnp.exp(m_old - m_new)
            p = jnp.exp(s - m_new)
            l_ref[rs, :] = alpha * l_ref[rs, :] + jnp.sum(p, axis=-1, keepdims=True)
            acc_ref[rs, :] = alpha * acc_ref[rs, :] + _dot(p.astype(BF16), v_tile)
            m_ref[rs, :] = m_new
        return carry

    lax.fori_loop(0, qi // sub + 1, sel_step, 0)

    nwin = NSA_WINDOW // TQ + 1
    first = jnp.maximum(qi - (nwin - 1), 0)
    w0 = pl.multiple_of(first * TQ, TQ)
    kw_tile = kw_ref[pl.ds(w0, nwin * TQ), :].astype(BF16)
    vw_tile = vw_ref[pl.ds(w0, nwin * TQ), :].astype(BF16)
    delta = (qi * TQ + lax.broadcasted_iota(jnp.int32, (TQ, nwin * TQ), 0)
             - (w0 + lax.broadcasted_iota(jnp.int32, (TQ, nwin * TQ), 1)))
    wmask_add = jnp.where((delta >= 0) & (delta < NSA_WINDOW), 0.0, NEG)

    misc = misc_ref[...]
    gpre = jnp.where(g == 0, misc[:, MISC_NG:MISC_NG + 3 * HG], misc[:, MISC_NG + 3 * HG:MISC_NG + 6 * HG])
    gates = _sigmoid(gpre)
    for hg in range(HG):
        hs = slice(hg * HEAD_DIM, (hg + 1) * HEAD_DIM)
        rs = slice(hg * TQ, (hg + 1) * TQ)
        bias = jnp.concatenate([tt_ref[hg, jnp.maximum(qi - (first + w), 0)] for w in range(nwin)], axis=1)
        s = _dot_t(qs[hg], kw_tile) + bias + wmask_add
        p = jnp.exp(s - jnp.max(s, axis=-1, keepdims=True))
        o_win = _dot(p.astype(BF16), vw_tile) / jnp.sum(p, axis=-1, keepdims=True)
        o_sel = acc_ref[rs, :] / l_ref[rs, :]
        y = (gates[:, 3 * hg:3 * hg + 1] * oc_ref[:, hs] + gates[:, 3 * hg + 1:3 * hg + 2] * o_sel
             + gates[:, 3 * hg + 2:3 * hg + 3] * o_win)
        o_ref[:, hs] = y.astype(BF16)


NSA_TQ = 128


def nsa_bias(thr, tab_t, s):
    nh = NSA_KV_GROUPS * NSA_HPG
    tt = bias_table(thr, tab_t, "toeplitz", (nh, s // NSA_TQ, NSA_TQ, NSA_TQ), (NSA_TQ, NSA_TQ))
    bc = bias_table(thr, tab_t, "compressed", (nh, s, s // NSA_CMP_STRIDE), (NSA_TQ, s // NSA_CMP_STRIDE))
    return tt, bc


def nsa_mixer(u3, tt, bc, pe_k, pe_v, ck_w1, ck_w2, cv_w1, cv_w2):
    bsz, s, _ = u3.shape
    G, HG, dh = NSA_KV_GROUPS, NSA_HPG, HEAD_DIM
    TQ = NSA_TQ
    nqt = s // TQ
    assert NSA_KV_GROUPS == 2 and s % NSA_SEL_TK == 0 and s >= NSA_WINDOW + TQ
    nblk = s // NSA_CMP_STRIDE
    n_cmp = (s - NSA_CMP_BLOCK) // NSA_CMP_STRIDE + 1
    n_sb = s // NSA_SEL_BLOCK
    agg = np.zeros((nblk, n_sb), np.float32)
    agg[:n_cmp] = _sel_agg_matrix(n_cmp, n_sb)
    agg = jnp.asarray(agg)

    k_cmp = _nsa_compress(u3, U_NKC, pe_k, ck_w1, ck_w2, "nsa_compress_k")
    v_cmp = _nsa_compress(u3, U_NVC, pe_v, cv_w1, cv_w2, "nsa_compress_v")

    qspec = pl.BlockSpec((None, TQ, HG * dh), lambda b, g, i: (b, i, U_NQ // (HG * dh) + g))
    o_cmp, sel = pl.pallas_call(
        _nsa_cmp_body,
        grid=(bsz, G, nqt),
        in_specs=[qspec,
                  pl.BlockSpec((None, None, nblk, dh), lambda b, g, i: (b, g, 0, 0)),
                  pl.BlockSpec((None, None, nblk, dh), lambda b, g, i: (b, g, 0, 0)),
                  pl.BlockSpec((HG, TQ, nblk), lambda b, g, i: (g, i, 0)),
                  pl.BlockSpec((nblk, n_sb), lambda b, g, i: (0, 0))],
        out_specs=[pl.BlockSpec((None, TQ, HG * dh), lambda b, g, i: (b, i, g)),
                   pl.BlockSpec((None, None, TQ, n_sb), lambda b, g, i: (b, g, i, 0))],
        out_shape=[jax.ShapeDtypeStruct((bsz, s, G * HG * dh), F32),
                   jax.ShapeDtypeStruct((bsz, G, s, n_sb), F32)],
        compiler_params=_cparams("parallel", "parallel", "arbitrary"),
        name="nsa_compressed_attn",
    )(u3, k_cmp, v_cmp, bc, agg)

    def kv(off):
        return pl.BlockSpec((None, s, dh), lambda b, g, i: (b, 0, off // dh + g))

    y = pl.pallas_call(
        _nsa_selwin_body,
        grid=(bsz, G, nqt),
        in_specs=[qspec,
                  pl.BlockSpec((None, TQ, HG * dh), lambda b, g, i: (b, i, g)),
                  pl.BlockSpec((None, None, TQ, n_sb), lambda b, g, i: (b, g, i, 0)),
                  pl.BlockSpec((None, TQ, LANES), lambda b, g, i: (b, i, U_MISC // LANES)),
                  kv(U_NKS), kv(U_NVS), kv(U_NKW), kv(U_NVW),
                  pl.BlockSpec((HG, nqt, TQ, TQ), lambda b, g, i: (g, 0, 0, 0))],
        out_specs=pl.BlockSpec((None, TQ, HG * dh), lambda b, g, i: (b, i, g)),
        out_shape=jax.ShapeDtypeStruct((bsz, s, BRANCH_WIDTH), BF16),
        scratch_shapes=[pltpu.VMEM((s, dh), BF16), pltpu.VMEM((s, dh), BF16),
                        pltpu.VMEM((HG * TQ, 1), F32), pltpu.VMEM((HG * TQ, 1), F32),
                        pltpu.VMEM((HG * TQ, dh), F32)],
        compiler_params=_cparams("parallel", "parallel", "arbitrary"),
        name="nsa_selected_window_attn",
    )(u3, o_cmp, sel, u3, u3, u3, u3, u3, tt)
    return y


def _merge_body(x_ref, wg0, wg1, wg2, wg3, y0, y1, y2, y3, wb_ref, o_ref):
    x = x_ref[...]
    acc = None
    for b, (wg, y) in enumerate(((wg0, y0), (wg1, y1), (wg2, y2), (wg3, y3))):
        gate = _sigmoid(_dot(x, wg[...]))
        term = gate * _dot(y[...], wb_ref[b])
        acc = term if acc is None else acc + term
    o_ref[...] = acc.astype(o_ref.dtype)


def merge_branches(hb, w_gates, ys, w_branch):
    n, d = hb.shape
    tm, tn = min(512, n), min(256, d)
    nj = d // tn

    def wg(b):
        return pl.BlockSpec((d, tn), lambda i, j: (0, b * nj + j))

    yspec = pl.BlockSpec((tm, BRANCH_WIDTH), lambda i, j: (i, 0))
    return pl.pallas_call(
        _merge_body,
        grid=(n // tm, nj),
        in_specs=[pl.BlockSpec((tm, d), lambda i, j: (i, 0)), wg(0), wg(1), wg(2), wg(3),
                  yspec, yspec, yspec, yspec,
                  pl.BlockSpec((N_BRANCH, BRANCH_WIDTH, tn), lambda i, j: (0, 0, j))],
        out_specs=pl.BlockSpec((tm, tn), lambda i, j: (i, j)),
        out_shape=jax.ShapeDtypeStruct((n, d), BF16),
        compiler_params=_cparams("parallel", "arbitrary"),
        name="merge_branches",
    )(hb, w_gates, w_gates, w_gates, w_gates, *ys, w_branch)


MOE_TN = 512
EXPERTS_PER_TILE = MOE_TN // EXPERT_FF


def _router_body(x_ref, w_ref, b_ref, comb_ref, combt_ref):
    logits = _dot(x_ref[...], w_ref[...]) + b_ref[...]
    lane = lax.broadcasted_iota(jnp.int32, logits.shape, 1).astype(F32)
    work = logits
    picks, vals = [], []
    for _ in range(TOP_K):
        mx = jnp.max(work, axis=-1, keepdims=True)
        first = jnp.min(jnp.where(work == mx, lane, float(LANES)), axis=-1, keepdims=True)
        pick = lane == first
        picks.append(pick)
        vals.append(mx)
        work = jnp.where(pick, -jnp.inf, work)
    exps = [jnp.exp(v - vals[0]) for v in vals]
    den = exps[0]
    for e in exps[1:]:
        den = den + e
    comb = jnp.zeros_like(logits)
    for pick, e in zip(picks, exps):
        comb = comb + jnp.where(pick, e / den, 0.0)
    comb_ref[...] = comb
    for t in range(N_EXPERTS // EXPERTS_PER_TILE):
        combt_ref[t] = comb[:, t * EXPERTS_PER_TILE:(t + 1) * EXPERTS_PER_TILE]


def moe_router(hb, router_w, router_b):
    n, d = hb.shape
    tm = min(512, n)
    w = jnp.zeros((d, LANES), F32).at[:, :N_EXPERTS].set(router_w).astype(BF16)
    b = jnp.full((1, LANES), NEG, F32).at[0, :N_EXPERTS].set(router_b)
    nt = N_EXPERTS // EXPERTS_PER_TILE
    return pl.pallas_call(
        _router_body,
        grid=(n // tm,),
        in_specs=[pl.BlockSpec((tm, d), lambda i: (i, 0)), pl.BlockSpec((d, LANES), lambda i: (0, 0)),
                  pl.BlockSpec((1, LANES), lambda i: (0, 0))],
        out_specs=[pl.BlockSpec((tm, LANES), lambda i: (i, 0)),
                   pl.BlockSpec((nt, tm, EXPERTS_PER_TILE), lambda i: (0, i, 0))],
        out_shape=[jax.ShapeDtypeStruct((n, LANES), F32), jax.ShapeDtypeStruct((nt, n, EXPERTS_PER_TILE), F32)],
        compiler_params=_cparams("parallel"),
        name="moe_router",
    )(hb, w, b)


def _moe_up_body(x_ref, wg_ref, wu_ref, bg_ref, bu_ref, comb_ref, a_ref):
    x = x_ref[...]
    gate = jnp.minimum(_dot(x, wg_ref[...]) + bg_ref[...], SWIGLU_LIMIT)
    up = jnp.clip(_dot(x, wu_ref[...]) + bu_ref[...], -SWIGLU_LIMIT, SWIGLU_LIMIT)
    act = gate * _sigmoid(SWIGLU_ALPHA * gate) * (up + 1.0)
    for e in range(EXPERTS_PER_TILE):
        es = slice(e * EXPERT_FF, (e + 1) * EXPERT_FF)
        a_ref[:, es] = (act[:, es] * comb_ref[:, e:e + 1]).astype(BF16)


def moe_up(hb, w_gate, w_up, b_gate, b_up, comb_t):
    n, d = hb.shape
    tm = min(512, n)
    width = N_EXPERTS * EXPERT_FF
    wspec = pl.BlockSpec((d, MOE_TN), lambda i, j: (0, j))
    bspec = pl.BlockSpec((1, MOE_TN), lambda i, j: (0, j))
    return pl.pallas_call(
        _moe_up_body,
        grid=(n // tm, width // MOE_TN),
        in_specs=[pl.BlockSpec((tm, d), lambda i, j: (i, 0)), wspec, wspec, bspec, bspec,
                  pl.BlockSpec((None, tm, EXPERTS_PER_TILE), lambda i, j: (j, i, 0))],
        out_specs=pl.BlockSpec((tm, MOE_TN), lambda i, j: (i, j)),
        out_shape=jax.ShapeDtypeStruct((n, width), BF16),
        compiler_params=_cparams("parallel", "arbitrary"),
        name="moe_up",
    )(hb, w_gate, w_up, b_gate, b_up, comb_t)


def _moe_down_body(a_ref, w_ref, comb_ref, b2_ref, o_ref):
    o_ref[...] = _dot(a_ref[...], w_ref[...]) + _dot(comb_ref[...].astype(BF16), b2_ref[...])


def moe_down(a, w2, comb, b2):
    n, k = a.shape
    d = w2.shape[1]
    tm, tn = min(1024, n), min(512, d)
    return pl.pallas_call(
        _moe_down_body,
        grid=(n // tm, d // tn),
        in_specs=[pl.BlockSpec((tm, k), lambda i, j: (i, 0)), pl.BlockSpec((k, tn), lambda i, j: (0, j)),
                  pl.BlockSpec((tm, LANES), lambda i, j: (i, 0)), pl.BlockSpec((LANES, tn), lambda i, j: (0, j))],
        out_specs=pl.BlockSpec((tm, tn), lambda i, j: (i, j)),
        out_shape=jax.ShapeDtypeStruct((n, d), F32),
        compiler_params=_cparams("parallel", "arbitrary"),
        name="moe_down",
    )(a, w2, comb, b2)


def _regroup_w_in(w):
    d = w.shape[0]
    o = _SRC_OFF
    pad = U_WIDTH - U_MISC - (GLA_GATE_RANK + 2 * N_HEADS + 3 * N_HEADS)
    w_u = jnp.concatenate([
        w[:, o[0]:o[4]],
        w[:, o[5]:o[8]],
        w[:, o[10]:o[13]],
        w[:, o[13]:o[20]],
        w[:, o[4]:o[5]], w[:, o[8]:o[10]], w[:, o[20]:o[21]],
        jnp.zeros((d, pad), w.dtype),
    ], axis=1).astype(BF16)
    return w_u, w[:, SRC_GATES:].astype(BF16)


def kernel(x, rel_bias, w_in, gla_gate_w, gla_gate_b, gla_norm_g, mlstm_conv_w, mlstm_conv_b, mlstm_igate_b,
           mlstm_fgate_b, mlstm_norm_g, nsa_pe_k, nsa_pe_v, nsa_ck_w1, nsa_ck_w2, nsa_cv_w1, nsa_cv_w2, w_branch,
           w_out, ln1_g, ln1_b, router_w, router_b, exp_w1, exp_b1, exp_w2, exp_b2, ln2_g, ln2_b):
    out_dtype = x.dtype
    bsz, s, d = x.shape
    n = bsz * s
    h = x.astype(F32).reshape(n, d)
    hb = h.astype(BF16)
    thr = bucket_thresholds(max(s, BIAS_DIST_RANGE))
    tab_t = rel_bias.T
    dil_bias = dilated_bias(thr, tab_t)
    nsa_tt, nsa_bc = nsa_bias(thr, tab_t, s)
    for l in range(w_in.shape[0]):
        w_u, w_gates = _regroup_w_in(w_in[l])
        u = matmul(hb, w_u, F32, 1024, 512, "input_projection")
        u3 = u.reshape(bsz, s, U_WIDTH)
        y_a = gla_mixer(u3, gla_gate_w[l], gla_gate_b[l], gla_norm_g[l]).reshape(n, BRANCH_WIDTH)
        y_b = mlstm_mixer(u3, mlstm_conv_w[l], mlstm_conv_b[l], mlstm_igate_b[l], mlstm_fgate_b[l],
                          mlstm_norm_g[l]).reshape(n, BRANCH_WIDTH)
        y_c = dilated_mixer(u3, dil_bias).reshape(n, BRANCH_WIDTH)
        y_d = nsa_mixer(u3, nsa_tt, nsa_bc, nsa_pe_k[l], nsa_pe_v[l], nsa_ck_w1[l], nsa_ck_w2[l], nsa_cv_w1[l],
                        nsa_cv_w2[l]).reshape(n, BRANCH_WIDTH)
        merged = merge_branches(hb, w_gates, (y_a, y_b, y_c, y_d), w_branch[l].astype(BF16))
        attn = matmul(merged, w_out[l].astype(BF16), F32, 1024, 512, "output_projection")
        h, hb = ln_residual(h, attn, ln1_g[l], ln1_b[l], "layer_norm_1")

        comb, comb_t = moe_router(hb, router_w[l], router_b[l])
        w1 = exp_w1[l]
        w_gate = w1[:, :, :EXPERT_FF].transpose(1, 0, 2).reshape(d, N_EXPERTS * EXPERT_FF).astype(BF16)
        w_up = w1[:, :, EXPERT_FF:].transpose(1, 0, 2).reshape(d, N_EXPERTS * EXPERT_FF).astype(BF16)
        b_gate = exp_b1[l][:, :EXPERT_FF].reshape(1, -1)
        b_up = exp_b1[l][:, EXPERT_FF:].reshape(1, -1)
        act = moe_up(hb, w_gate, w_up, b_gate, b_up, comb_t)
        b2 = jnp.zeros((LANES, d), F32).at[:N_EXPERTS].set(exp_b2[l]).astype(BF16)
        ffn = moe_down(act, exp_w2[l].reshape(N_EXPERTS * EXPERT_FF, d).astype(BF16), comb, b2)
        h, hb = ln_residual(h, ffn, ln2_g[l], ln2_b[l], "layer_norm_2")
    return h.reshape(bsz, s, d).astype(out_dtype)
```

```python
import functools
import math

import numpy as np
import jax
import jax.numpy as jnp
from jax import lax
from jax.experimental import pallas as pl
from jax.experimental.pallas import tpu as pltpu

F32 = jnp.float32
BF16 = jnp.bfloat16
HIGHEST = lax.Precision.HIGHEST

N_LAYERS_FOR_DEEPNORM = 4
HEAD_DIM = 128
BRANCH_WIDTH = 1024
N_BRANCH = 4
N_HEADS = 8
DK = 64
CHUNK = 64
GLA_GATE_RANK = 16
GLA_TAU = 16.0
MLSTM_CONV = 4
MLSTM_IGATE_CAP = 15.0
DIL_PATTERNS = ((128, 1), (512, 4), (2048, 16))
DIL_BLOCK = 128
NSA_KV_GROUPS = 2
NSA_HPG = 4
NSA_CMP_BLOCK = 32
NSA_CMP_STRIDE = 16
NSA_CMP_HIDDEN = 256
NSA_SEL_BLOCK = 64
NSA_N_SEL = 16
NSA_WINDOW = 512
NSA_SEL_TK = 512
REL_BUCKETS = 32
REL_MAX_DIST = 2048
N_EXPERTS = 32
TOP_K = 4
EXPERT_FF = 128
SWIGLU_LIMIT = 7.0
SWIGLU_ALPHA = 1.702
DEEPNORM_ALPHA = (2 * N_LAYERS_FOR_DEEPNORM) ** 0.25
LN_EPS = 1e-5
NEG = -1e30
BIG = 1e9

LANES = 128
VMEM_LIMIT_BYTES = 56 * 1024 * 1024

U_GQ, U_GK, U_GV, U_GR = 0, 512, 1024, 2048
U_MQK, U_MV, U_MO = 3072, 4096, 5120
U_DQ, U_DK, U_DV = 6144, 9216, 10240
U_NQ = 11264
U_NKC, U_NVC, U_NKS, U_NVS, U_NKW, U_NVW = 12288, 12544, 12800, 13056, 13312, 13568
U_MISC = 13824
U_WIDTH = 14336
MISC_GA, MISC_MI, MISC_MF, MISC_NG = 0, 16, 24, 32

_SRC_SIZES = (512, 512, 1024, 1024, 16, 1024, 1024, 1024, 8, 8, 3072, 1024, 1024, 1024,
              256, 256, 256, 256, 256, 256, 24)
_SRC_OFF = np.concatenate([[0], np.cumsum(_SRC_SIZES)]).tolist()
SRC_GATES = _SRC_OFF[-1]


def _cparams(*sem):
    return pltpu.CompilerParams(dimension_semantics=sem, vmem_limit_bytes=VMEM_LIMIT_BYTES)


def _log_sigmoid(x):
    return jnp.minimum(x, 0.0) - jnp.log1p(jnp.exp(-jnp.abs(x)))


def _sigmoid(x):
    return 1.0 / (1.0 + jnp.exp(-x))


def _silu(x):
    return x * _sigmoid(x)


def _dot(a, b):
    return jnp.dot(a, b, preferred_element_type=F32)


def _dot_t(a, b):
    return lax.dot_general(a, b, (((1,), (1,)), ((), ())), preferred_element_type=F32)


def _tdot(a, b):
    return lax.dot_general(a, b, (((0,), (0,)), ((), ())), preferred_element_type=F32)


def _head_norm(o, g_row):
    mu = jnp.mean(o, axis=-1, keepdims=True)
    d = o - mu
    var = jnp.mean(d * d, axis=-1, keepdims=True)
    return d * lax.rsqrt(var + LN_EPS) * g_row


def t5_bucket(dist):
    d = jnp.maximum(dist, 0)
    exact = REL_BUCKETS // 2
    df = jnp.maximum(d, 1).astype(jnp.float32)
    large = exact + (jnp.log(df / exact) / math.log(REL_MAX_DIST / exact) * (REL_BUCKETS - exact)).astype(jnp.int32)
    return jnp.where(d < exact, d, jnp.minimum(large, REL_BUCKETS - 1))


def _mm_body(x_ref, w_ref, o_ref):
    o_ref[...] = _dot(x_ref[...], w_ref[...]).astype(o_ref.dtype)


def matmul(x, w, out_dtype, tm, tn, name):
    m, k = x.shape
    n = w.shape[1]
    tm, tn = min(tm, m), min(tn, n)
    return pl.pallas_call(
        _mm_body,
        grid=(m // tm, n // tn),
        in_specs=[pl.BlockSpec((tm, k), lambda i, j: (i, 0)), pl.BlockSpec((k, tn), lambda i, j: (0, j))],
        out_specs=pl.BlockSpec((tm, tn), lambda i, j: (i, j)),
        out_shape=jax.ShapeDtypeStruct((m, n), out_dtype),
        compiler_params=_cparams("parallel", "arbitrary"),
        name=name,
    )(x, w)


def _ln_body(h_ref, d_ref, g_ref, b_ref, o_ref, ob_ref):
    z = DEEPNORM_ALPHA * h_ref[...] + d_ref[...]
    mu = jnp.mean(z, axis=-1, keepdims=True)
    zc = z - mu
    var = jnp.mean(zc * zc, axis=-1, keepdims=True)
    y = zc * lax.rsqrt(var + LN_EPS) * g_ref[...] + b_ref[...]
    o_ref[...] = y
    ob_ref[...] = y.astype(BF16)


def ln_residual(h, delta, g, b, name):
    n, d = h.shape
    tm = min(256, n)
    row = pl.BlockSpec((tm, d), lambda i: (i, 0))
    vec = pl.BlockSpec((1, d), lambda i: (0, 0))
    return pl.pallas_call(
        _ln_body,
        grid=(n // tm,),
        in_specs=[row, row, vec, vec],
        out_specs=[row, row],
        out_shape=[jax.ShapeDtypeStruct((n, d), F32), jax.ShapeDtypeStruct((n, d), BF16)],
        compiler_params=_cparams("parallel"),
        name=name,
    )(h, delta, g.reshape(1, d), b.reshape(1, d))


def _gla_body(q_ref, k_ref, v_ref, r_ref, misc_ref, gw_ref, gb_ref, ng_ref, o_ref, st_ref):
    L = CHUNK

    @pl.when(pl.program_id(1) == 0)
    def _():
        st_ref[...] = jnp.zeros_like(st_ref)

    pre = _dot(misc_ref[...].astype(BF16), gw_ref[...]) + gb_ref[...]
    log_a = _log_sigmoid(pre) / GLA_TAU
    row = lax.broadcasted_iota(jnp.int32, (L, L), 0)
    col = lax.broadcasted_iota(jnp.int32, (L, L), 1)
    causal = col <= row
    tri = jnp.where(causal, 1.0, 0.0).astype(F32)
    b = jnp.dot(tri, log_a, precision=HIGHEST, preferred_element_type=F32)
    b_last = b[L - 1:L, :]
    q_dec = (q_ref[...] * (DK ** -0.5) * jnp.exp(b)).astype(BF16)
    k_dec = (k_ref[...] * jnp.exp(-b)).astype(BF16)
    k_end = (k_ref[...] * jnp.exp(b_last - b)).astype(BF16)
    decay = jnp.exp(b_last)
    eye = col == row
    for h in range(N_HEADS):
        ks = slice(h * DK, (h + 1) * DK)
        vs = slice(h * HEAD_DIM, (h + 1) * HEAD_DIM)
        vh = v_ref[:, vs].astype(BF16)
        state = st_ref[h]
        att = jnp.where(causal, _dot_t(q_dec[:, ks], k_dec[:, ks]), 0.0)
        o = _dot(att.astype(BF16), vh) + _dot(q_dec[:, ks], state.astype(BF16))
        dec_col = jnp.sum(jnp.where(eye, decay[:, ks], 0.0), axis=1, keepdims=True)
        st_ref[h] = dec_col * state + _tdot(k_end[:, ks], vh)
        on = _head_norm(o, ng_ref[:, vs])
        o_ref[:, vs] = (_silu(r_ref[:, vs]) * on).astype(BF16)


def gla_mixer(u3, gate_w, gate_b, norm_g):
    bsz, s, _ = u3.shape
    L = CHUNK
    gw = jnp.zeros((LANES, N_HEADS * DK), F32).at[MISC_GA:MISC_GA + GLA_GATE_RANK].set(gate_w).astype(BF16)

    def ublock(width, off):
        return pl.BlockSpec((None, L, width), lambda b, t: (b, t, off // width))

    def const(shape):
        return pl.BlockSpec(shape, lambda b, t: (0,) * len(shape))

    return pl.pallas_call(
        _gla_body,
        grid=(bsz, s // L),
        in_specs=[ublock(512, U_GQ), ublock(512, U_GK), ublock(1024, U_GV), ublock(1024, U_GR),
                  ublock(LANES, U_MISC), const((LANES, 512)), const((1, 512)), const((1, 1024))],
        out_specs=pl.BlockSpec((None, L, 1024), lambda b, t: (b, t, 0)),
        out_shape=jax.ShapeDtypeStruct((bsz, s, BRANCH_WIDTH), BF16),
        scratch_shapes=[pltpu.VMEM((N_HEADS, DK, HEAD_DIM), F32)],
        compiler_params=_cparams("parallel", "arbitrary"),
        name="gla_mixer",
    )(u3, u3, u3, u3, u3, gw, gate_b.reshape(1, -1), norm_g.reshape(1, -1))


def _mlstm_body(qk_ref, v_ref, op_ref, misc_ref, cw_ref, cb_ref, ib_ref, fb_ref, ng_ref, o_ref,
                ext_ref, c_ref, n_ref, m_ref):
    L = CHUNK
    C2 = 2 * N_HEADS * DK

    @pl.when(pl.program_id(1) == 0)
    def _():
        ext_ref[0:8, :] = jnp.zeros((8, C2), F32)
        c_ref[...] = jnp.zeros_like(c_ref)
        n_ref[...] = jnp.zeros_like(n_ref)
        m_ref[...] = jnp.zeros_like(m_ref)

    x = qk_ref[...]
    ext_ref[8:8 + L, :] = x
    y = (cb_ref[...] + cw_ref[3:4, :] * x + cw_ref[2:3, :] * ext_ref[pl.ds(7, L), :]
         + cw_ref[1:2, :] * ext_ref[pl.ds(6, L), :] + cw_ref[0:1, :] * ext_ref[pl.ds(5, L), :])
    ext_ref[0:8, :] = x[L - 8:L, :]
    qk = _silu(y)
    q_all = qk[:, :N_HEADS * DK].astype(BF16)
    k_all = qk[:, N_HEADS * DK:] * (DK ** -0.5)

    misc = misc_ref[...]
    i_g = MLSTM_IGATE_CAP * jnp.tanh((misc + ib_ref[...]) / MLSTM_IGATE_CAP)
    log_f = _log_sigmoid(misc + fb_ref[...])
    row = lax.broadcasted_iota(jnp.int32, (L, L), 0)
    col = lax.broadcasted_iota(jnp.int32, (L, L), 1)
    causal = col <= row
    tri = jnp.where(causal, 1.0, 0.0).astype(F32)
    b_c = jnp.dot(tri, log_f, precision=HIGHEST, preferred_element_type=F32)
    b_r = b_c.T
    i_r = i_g.T

    for h in range(N_HEADS):
        ks = slice(h * DK, (h + 1) * DK)
        vs = slice(h * HEAD_DIM, (h + 1) * HEAD_DIM)
        q = q_all[:, ks]
        k = k_all[:, ks]
        v = v_ref[:, vs].astype(BF16)
        bcol = b_c[:, MISC_MF + h:MISC_MF + h + 1]
        icol = i_g[:, MISC_MI + h:MISC_MI + h + 1]
        brow = b_r[MISC_MF + h:MISC_MF + h + 1, :]
        irow = i_r[MISC_MI + h:MISC_MI + h + 1, :]
        b_last = bcol[L - 1:L, :]
        dmat = jnp.where(causal, bcol - brow + irow, -jnp.inf)
        a_end = b_last - bcol + icol
        m_loc = jnp.max(a_end, axis=0, keepdims=True)
        c_s = c_ref[h]
        n_s = n_ref[h, 0:1, 0:DK]
        m_s = m_ref[h, 0:1, 0:1]
        inter_log = bcol + m_s
        m_i = jnp.maximum(inter_log, jnp.max(dmat, axis=-1, keepdims=True))
        w_inter = jnp.exp(inter_log - m_i)
        qk_s = _dot_t(q, k.astype(BF16)) * jnp.exp(dmat - m_i)
        num = w_inter * _dot(q, c_s.astype(BF16)) + _dot(qk_s.astype(BF16), v)
        qn = jnp.sum(q.astype(F32) * n_s, axis=-1, keepdims=True)
        den = w_inter * qn + jnp.sum(qk_s, axis=-1, keepdims=True)
        hout = num / jnp.maximum(jnp.abs(den), jnp.exp(-m_i))
        m_new = jnp.maximum(b_last + m_s, m_loc)
        w = jnp.exp(a_end - m_new)
        sc = jnp.exp(b_last + m_s - m_new)
        wk = w * k
        c_ref[h] = sc * c_s + _tdot(wk.astype(BF16), v)
        n_ref[h, 0:1, 0:DK] = sc * n_s + jnp.sum(wk, axis=0, keepdims=True)
        m_ref[h] = jnp.broadcast_to(m_new, (8, LANES))
        hn = _head_norm(hout, ng_ref[:, vs])
        o_ref[:, vs] = (_sigmoid(op_ref[:, vs]) * hn).astype(BF16)


def mlstm_mixer(u3, conv_w, conv_b, igate_b, fgate_b, norm_g):
    bsz, s, _ = u3.shape
    L = CHUNK
    ib = jnp.zeros((1, LANES), F32).at[0, MISC_MI:MISC_MI + N_HEADS].set(igate_b)
    fb = jnp.zeros((1, LANES), F32).at[0, MISC_MF:MISC_MF + N_HEADS].set(fgate_b)

    def ublock(width, off):
        return pl.BlockSpec((None, L, width), lambda b, t: (b, t, off // width))

    def const(shape):
        return pl.BlockSpec(shape, lambda b, t: (0,) * len(shape))

    return pl.pallas_call(
        _mlstm_body,
        grid=(bsz, s // L),
        in_specs=[ublock(1024, U_MQK), ublock(1024, U_MV), ublock(1024, U_MO), ublock(LANES, U_MISC),
                  const((MLSTM_CONV, 1024)), const((1, 1024)), const((1, LANES)), const((1, LANES)),
                  const((1, 1024))],
        out_specs=pl.BlockSpec((None, L, 1024), lambda b, t: (b, t, 0)),
        out_shape=jax.ShapeDtypeStruct((bsz, s, BRANCH_WIDTH), BF16),
        scratch_shapes=[pltpu.VMEM((L + 8, 1024), F32), pltpu.VMEM((N_HEADS, DK, HEAD_DIM), F32),
                        pltpu.VMEM((N_HEADS, 8, LANES), F32), pltpu.VMEM((N_HEADS, 8, LANES), F32)],
        compiler_params=_cparams("parallel", "arbitrary"),
        name="mlstm_mixer",
    )(u3, u3, u3, u3, conv_w, conv_b.reshape(1, -1), ib, fb, norm_g.reshape(1, -1))


BIAS_DIST_RANGE = 4096


def bucket_thresholds(max_dist):
    lut = t5_bucket(jnp.arange(max_dist))
    return jnp.sum(lut[None, :] < jnp.arange(REL_BUCKETS)[:, None], axis=1).astype(jnp.int32)


def _bias_table_body(thr_ref, tab_ref, o_ref, *, kind):
    a = pl.program_id(0)
    c = pl.program_id(1)
    shape = o_ref.shape
    i = lax.broadcasted_iota(jnp.int32, shape, 0)
    j = lax.broadcasted_iota(jnp.int32, shape, 1)
    if kind == "dilated":
        dil = jnp.where(a == 0, DIL_PATTERNS[0][1], jnp.where(a == 1, DIL_PATTERNS[1][1], DIL_PATTERNS[2][1]))
        dist = (i + DIL_BLOCK - j) * dil
        head = a * N_HEADS + c
    elif kind == "toeplitz":
        dist = c * shape[0] + i - j
        head = 3 * N_HEADS + a
    else:
        dist = c * shape[0] + i - (j * NSA_CMP_STRIDE + NSA_CMP_BLOCK - 1)
        head = 3 * N_HEADS + a
    dist = jnp.maximum(dist, 0)
    acc = jnp.full(shape, tab_ref[head, 0], F32)
    for k in range(1, REL_BUCKETS):
        acc = jnp.where(dist >= thr_ref[k], tab_ref[head, k], acc)
    o_ref[...] = acc


def bias_table(thr, tab_t, kind, out_dims, block):
    smem = pl.BlockSpec(memory_space=pltpu.SMEM)
    return pl.pallas_call(
        functools.partial(_bias_table_body, kind=kind),
        grid=out_dims[:2] if kind != "compressed" else (out_dims[0], out_dims[1] // block[0]),
        in_specs=[smem, smem],
        out_specs=(pl.BlockSpec((None, None) + block, lambda a, c: (a, c, 0, 0)) if kind != "compressed"
                   else pl.BlockSpec((None,) + block, lambda a, c: (a, c, 0))),
        out_shape=jax.ShapeDtypeStruct(out_dims, F32),
        compiler_params=_cparams("parallel", "parallel"),
        name=f"bias_table_{kind}",
    )(thr, tab_t)


DIL_SPAN = 2048


def _dil_body(q0_ref, q1_ref, q2_ref, kp_ref, kc_ref, vp_ref, vc_ref, bias_ref, o_ref, m_scr, l_scr, acc_scr):
    P = DIL_BLOCK
    NB = DIL_SPAN // P
    q_refs = (q0_ref, q1_ref, q2_ref)
    row = lax.broadcasted_iota(jnp.int32, (P, P), 0)
    col = lax.broadcasted_iota(jnp.int32, (P, P), 1)
    mask_cur = jnp.where(col <= row, 0.0, NEG)
    mask_prev = jnp.where(col >= row, 0.0, NEG)
    no_prev = jnp.where(pl.program_id(1) > 0, 0, P)
    mask_prev_first = jnp.where((col - row) >= no_prev, 0.0, NEG)

    def rows(r, n, dil):
        return pl.ds(r, n, stride=dil) if dil > 1 else pl.ds(r, n)

    def stacked(ref, dil):
        per = DIL_SPAN // dil
        return jnp.concatenate([ref[rows(r, per, dil), :].reshape(per // P, P, HEAD_DIM) for r in range(dil)], axis=0)

    def stacked_prev(cur3, prev_ref, dil):
        nblk = NB // dil
        parts = []
        for r in range(dil):
            parts.append(prev_ref[rows(r + DIL_SPAN - dil * P, P, dil), :].reshape(1, P, HEAD_DIM))
            if nblk > 1:
                parts.append(cur3[r * nblk:(r + 1) * nblk - 1])
        return jnp.concatenate(parts, axis=0)

    for g, (_, dil) in enumerate(DIL_PATTERNS):
        nblk = NB // dil
        per = DIL_SPAN // dil
        q3 = (stacked(q_refs[g], dil) * (HEAD_DIM ** -0.5)).astype(BF16)
        kc3 = stacked(kc_ref, dil)
        vc3 = stacked(vc_ref, dil)
        kp3 = stacked_prev(kc3, kp_ref, dil).astype(BF16)
        vp3 = stacked_prev(vc3, vp_ref, dil).astype(BF16)
        kc3 = kc3.astype(BF16)
        vc3 = vc3.astype(BF16)
        bias_p = bias_ref[g, :, 0:P]
        bias_prev3 = jnp.concatenate(
            [(bias_p + (mask_prev_first if b % nblk == 0 else mask_prev)).reshape(1, P, P) for b in range(NB)], axis=0)
        s_p = jnp.einsum('nqd,nkd->nqk', q3, kp3, preferred_element_type=F32) + bias_prev3
        s_c = (jnp.einsum('nqd,nkd->nqk', q3, kc3, preferred_element_type=F32)
               + (bias_ref[g, :, P:2 * P] + mask_cur)[None])
        m = jnp.maximum(jnp.max(s_p, axis=-1, keepdims=True), jnp.max(s_c, axis=-1, keepdims=True))
        p_p = jnp.exp(s_p - m)
        p_c = jnp.exp(s_c - m)
        l = jnp.sum(p_p, axis=-1, keepdims=True) + jnp.sum(p_c, axis=-1, keepdims=True)
        acc = (jnp.einsum('nqk,nkd->nqd', p_p.astype(BF16), vp3, preferred_element_type=F32)
               + jnp.einsum('nqk,nkd->nqd', p_c.astype(BF16), vc3, preferred_element_type=F32))
        for r in range(dil):
            sl = rows(g * DIL_SPAN + r, per, dil)
            bs = slice(r * nblk, (r + 1) * nblk)
            m_scr[sl, :] = jnp.broadcast_to(m[bs].reshape(per, 1), (per, LANES))
            l_scr[sl, :] = jnp.broadcast_to(l[bs].reshape(per, 1), (per, LANES))
            acc_scr[sl, :] = acc[bs].reshape(per, HEAD_DIM)

    ms = [m_scr[g * DIL_SPAN:(g + 1) * DIL_SPAN, :] for g in range(len(DIL_PATTERNS))]
    m_all = jnp.maximum(jnp.maximum(ms[0], ms[1]), ms[2])
    num = jnp.zeros((DIL_SPAN, HEAD_DIM), F32)
    den = jnp.zeros((DIL_SPAN, LANES), F32)
    for g in range(len(DIL_PATTERNS)):
        w = jnp.exp(ms[g] - m_all)
        num = num + w * acc_scr[g * DIL_SPAN:(g + 1) * DIL_SPAN, :]
        den = den + w * l_scr[g * DIL_SPAN:(g + 1) * DIL_SPAN, :]
    o_ref[...] = (num / den).astype(BF16)


def dilated_bias(thr, tab_t):
    P = DIL_BLOCK
    return bias_table(thr, tab_t, "dilated", (len(DIL_PATTERNS), N_HEADS, P, 2 * P), (P, 2 * P))


def dilated_mixer(u3, bias):
    bsz, s, _ = u3.shape
    P = DIL_BLOCK
    assert s % DIL_SPAN == 0 and all(w // d == P and DIL_SPAN % (d * P) == 0 for w, d in DIL_PATTERNS)

    def ub(off, prev):
        def idx(b, t, h):
            return (b, jnp.maximum(t - 1, 0) if prev else t, off // HEAD_DIM + h)
        return pl.BlockSpec((None, DIL_SPAN, HEAD_DIM), idx)

    return pl.pallas_call(
        _dil_body,
        grid=(bsz, s // DIL_SPAN, N_HEADS),
        in_specs=[ub(U_DQ, False), ub(U_DQ + 1024, False), ub(U_DQ + 2048, False),
                  ub(U_DK, True), ub(U_DK, False), ub(U_DV, True), ub(U_DV, False),
                  pl.BlockSpec((len(DIL_PATTERNS), None, P, 2 * P), lambda b, t, h: (0, h, 0, 0))],
        out_specs=pl.BlockSpec((None, DIL_SPAN, HEAD_DIM), lambda b, t, h: (b, t, h)),
        out_shape=jax.ShapeDtypeStruct((bsz, s, BRANCH_WIDTH), BF16),
        scratch_shapes=[pltpu.VMEM((len(DIL_PATTERNS) * DIL_SPAN, LANES), F32),
                        pltpu.VMEM((len(DIL_PATTERNS) * DIL_SPAN, LANES), F32),
                        pltpu.VMEM((len(DIL_PATTERNS) * DIL_SPAN, HEAD_DIM), F32)],
        compiler_params=_cparams("parallel", "parallel", "arbitrary"),
        name="dilated_attn",
    )(u3, u3, u3, u3, u3, u3, u3, bias)


def _gelu_tanh(x):
    return 0.5 * x * (1.0 + jnp.tanh(math.sqrt(2.0 / math.pi) * (x + 0.044715 * (x * x * x))))


def _nsa_compress_body(t_ref, pe_ref, w1_ref, w2_ref, o_ref):
    nblk = t_ref.shape[0] // NSA_CMP_STRIDE
    half = NSA_CMP_STRIDE
    t1 = jnp.zeros((nblk, NSA_CMP_HIDDEN), F32)
    t2 = jnp.zeros((nblk, NSA_CMP_HIDDEN), F32)
    for p in range(half):
        xp = t_ref[pl.ds(p, nblk, stride=NSA_CMP_STRIDE), :]
        t1 = t1 + _dot((xp + pe_ref[p:p + 1, :]).astype(BF16), w1_ref[p * HEAD_DIM:(p + 1) * HEAD_DIM, :])
        t2 = t2 + _dot((xp + pe_ref[half + p:half + p + 1, :]).astype(BF16),
                       w1_ref[(half + p) * HEAD_DIM:(half + p + 1) * HEAD_DIM, :])
    hidden = t1 + pltpu.roll(t2, nblk - 1, 0)
    o_ref[...] = _dot(_gelu_tanh(hidden).astype(BF16), w2_ref[...])


def _nsa_compress(u3, off, pe, w1, w2, name):
    bsz, s, _ = u3.shape
    nblk = s // NSA_CMP_STRIDE
    G = NSA_KV_GROUPS
    return pl.pallas_call(
        _nsa_compress_body,
        grid=(bsz, G),
        in_specs=[pl.BlockSpec((None, s, HEAD_DIM), lambda b, g: (b, 0, off // HEAD_DIM + g)),
                  pl.BlockSpec((NSA_CMP_BLOCK, HEAD_DIM), lambda b, g: (0, 0)),
                  pl.BlockSpec((NSA_CMP_BLOCK * HEAD_DIM, NSA_CMP_HIDDEN), lambda b, g: (0, 0)),
                  pl.BlockSpec((NSA_CMP_HIDDEN, HEAD_DIM), lambda b, g: (0, 0))],
        out_specs=pl.BlockSpec((None, None, nblk, HEAD_DIM), lambda b, g: (b, g, 0, 0)),
        out_shape=jax.ShapeDtypeStruct((bsz, G, nblk, HEAD_DIM), F32),
        compiler_params=_cparams("parallel", "parallel"),
        name=name,
    )(u3, pe, w1.astype(BF16), w2.astype(BF16))


def _nsa_cmp_body(q_ref, kc_ref, vc_ref, bc_ref, agg_ref, oc_ref, sel_ref):
    TQ = q_ref.shape[0]
    ncb = kc_ref.shape[0]
    nsb = agg_ref.shape[1]
    t0 = pl.program_id(2) * TQ
    tpos = t0 + lax.broadcasted_iota(jnp.int32, (TQ, ncb), 0)
    ends = lax.broadcasted_iota(jnp.int32, (TQ, ncb), 1) * NSA_CMP_STRIDE + (NSA_CMP_BLOCK - 1)
    mask = ends <= tpos
    maskf = jnp.where(mask, 1.0, 0.0).astype(F32)
    kcb = kc_ref[...].astype(BF16)
    vcb = vc_ref[...].astype(BF16)
    psum = jnp.zeros((TQ, ncb), F32)
    for hg in range(NSA_HPG):
        hs = slice(hg * HEAD_DIM, (hg + 1) * HEAD_DIM)
        q = (q_ref[:, hs] * (HEAD_DIM ** -0.5)).astype(BF16)
        s = jnp.where(mask, _dot_t(q, kcb) + bc_ref[hg], NEG)
        p = jnp.exp(s - jnp.max(s, axis=-1, keepdims=True)) * maskf
        l = jnp.sum(p, axis=-1, keepdims=True)
        p = p / jnp.where(l > 0, l, 1.0)
        oc_ref[:, hs] = _dot(p.astype(BF16), vcb)
        psum = psum + p
    imp = jnp.dot(psum, agg_ref[...], precision=HIGHEST, preferred_element_type=F32)
    jblk = lax.broadcasted_iota(jnp.int32, (TQ, nsb), 1)
    tq = t0 + lax.broadcasted_iota(jnp.int32, (TQ, nsb), 0)
    cur = tq // NSA_SEL_BLOCK
    forced = (jblk == 0) | (jblk == cur) | (jblk == cur - 1)
    score = jnp.where(forced, BIG, jnp.where(jblk * NSA_SEL_BLOCK <= tq, imp, -BIG))
    jf = jblk.astype(F32)
    sel = jnp.zeros((TQ, nsb), F32)
    for _ in range(min(NSA_N_SEL, nsb)):
        mx = jnp.max(score, axis=-1, keepdims=True)
        first = jnp.min(jnp.where(score == mx, jf, float(nsb)), axis=-1, keepdims=True)
        pick = jf == first
        sel = jnp.where(pick, 1.0, sel)
        score = jnp.where(pick, -jnp.inf, score)
    sel_ref[...] = sel


def _sel_agg_matrix(n_cmp, n_sb):
    r = NSA_SEL_BLOCK // NSA_CMP_STRIDE
    c = NSA_CMP_BLOCK // NSA_CMP_STRIDE
    jj, aa, bb = np.meshgrid(np.arange(n_sb), np.arange(r), np.arange(c), indexing='ij')
    ii = r * jj + aa + bb - 1
    ok = (ii >= 0) & (ii < n_cmp)
    mat = np.zeros((n_cmp, n_sb), np.float32)
    np.add.at(mat, (ii[ok], jj[ok]), 1.0)
    return mat


def _nsa_selwin_body(q_ref, oc_ref, sel_ref, misc_ref, ks_ref, vs_ref, kw_ref, vw_ref, tt_ref, o_ref,
                     ksb_ref, vsb_ref, m_ref, l_ref, acc_ref):
    TQ = q_ref.shape[0]
    TK = NSA_SEL_TK
    HG = NSA_HPG
    nsb = sel_ref.shape[1]
    sub = TK // TQ
    g = pl.program_id(1)
    qi = pl.program_id(2)

    @pl.when(qi == 0)
    def _():
        ksb_ref[...] = ks_ref[...].astype(BF16)
        vsb_ref[...] = vs_ref[...].astype(BF16)

    q4 = jnp.concatenate([(q_ref[:, hg * HEAD_DIM:(hg + 1) * HEAD_DIM] * (HEAD_DIM ** -0.5)).astype(BF16)
                          for hg in range(HG)], axis=0)
    sel = sel_ref[...].astype(BF16)
    tpos = qi * TQ + lax.broadcasted_iota(jnp.int32, (TQ, TK), 0)
    kcol = lax.broadcasted_iota(jnp.int32, (TQ, TK), 1)
    e_row = lax.broadcasted_iota(jnp.int32, (nsb, TK), 0)
    e_col = lax.broadcasted_iota(jnp.int32, (nsb, TK), 1) // NSA_SEL_BLOCK

    m_ref[...] = jnp.full_like(m_ref, 0.1 * NEG)
    l_ref[...] = jnp.zeros_like(l_ref)
    acc_ref[...] = jnp.zeros_like(acc_ref)

    def sel_step(kj, carry):
        k0 = pl.multiple_of(kj * TK, TK)
        k_tile = ksb_ref[pl.ds(k0, TK), :]
        v_tile = vsb_ref[pl.ds(k0, TK), :]
        expand = jnp.where(e_row == kj * (TK // NSA_SEL_BLOCK) + e_col, 1.0, 0.0).astype(BF16)
        chosen = _dot(sel, expand) > 0.5
        mask_add = jnp.where(chosen & (k0 + kcol <= tpos), 0.0, NEG)
        bias4 = jnp.concatenate(
            [jnp.concatenate([tt_ref[hg, jnp.maximum(qi - (kj * sub + w), 0)] for w in range(sub)], axis=1) + mask_add
             for hg in range(HG)], axis=0)
        s = _dot_t(q4, k_tile) + bias4
        m_old = m_ref[...]
        m_new = jnp.maximum(m_old, jnp.max(s, axis=-1, keepdims=True))
        alpha = jnp.exp(m_old - m_new)
        p = jnp.exp(s - m_new)
        l_ref[...] = alpha * l_ref[...] + jnp.sum(p, axis=-1, keepdims=True)
        acc_ref[...] = alpha * acc_ref[...] + _dot(p.astype(BF16), v_tile)
        m_ref[...] = m_new
        return carry

    lax.fori_loop(0, qi // sub + 1, sel_step, 0)

    nwin = NSA_WINDOW // TQ + 1
    first = jnp.maximum(qi - (nwin - 1), 0)
    w0 = pl.multiple_of(first * TQ, TQ)
    kw_tile = kw_ref[pl.ds(w0, nwin * TQ), :].astype(BF16)
    vw_tile = vw_ref[pl.ds(w0, nwin * TQ), :].astype(BF16)
    delta = (qi * TQ + lax.broadcasted_iota(jnp.int32, (TQ, nwin * TQ), 0)
             - (w0 + lax.broadcasted_iota(jnp.int32, (TQ, nwin * TQ), 1)))
    wmask_add = jnp.where((delta >= 0) & (delta < NSA_WINDOW), 0.0, NEG)
    wbias4 = jnp.concatenate(
        [jnp.concatenate([tt_ref[hg, jnp.maximum(qi - (first + w), 0)] for w in range(nwin)], axis=1) + wmask_add
         for hg in range(HG)], axis=0)
    s = _dot_t(q4, kw_tile) + wbias4
    p = jnp.exp(s - jnp.max(s, axis=-1, keepdims=True))
    o_win = _dot(p.astype(BF16), vw_tile) / jnp.sum(p, axis=-1, keepdims=True)
    o_sel = acc_ref[...] / l_ref[...]

    misc = misc_ref[...]
    gpre = jnp.where(g == 0, misc[:, MISC_NG:MISC_NG + 3 * HG], misc[:, MISC_NG + 3 * HG:MISC_NG + 6 * HG])
    gates = _sigmoid(gpre)
    for hg in range(HG):
        hs = slice(hg * HEAD_DIM, (hg + 1) * HEAD_DIM)
        rs = slice(hg * TQ, (hg + 1) * TQ)
        y = (gates[:, 3 * hg:3 * hg + 1] * oc_ref[:, hs] + gates[:, 3 * hg + 1:3 * hg + 2] * o_sel[rs]
             + gates[:, 3 * hg + 2:3 * hg + 3] * o_win[rs])
        o_ref[:, hs] = y.astype(BF16)


NSA_TQ = 128
NSA_CMP_TQ = 512


def nsa_bias(thr, tab_t, s):
    nh = NSA_KV_GROUPS * NSA_HPG
    tt = bias_table(thr, tab_t, "toeplitz", (nh, s // NSA_TQ, NSA_TQ, NSA_TQ), (NSA_TQ, NSA_TQ))
    bc = bias_table(thr, tab_t, "compressed", (nh, s, s // NSA_CMP_STRIDE), (NSA_TQ, s // NSA_CMP_STRIDE))
    return tt, bc


def nsa_mixer(u3, tt, bc, pe_k, pe_v, ck_w1, ck_w2, cv_w1, cv_w2):
    bsz, s, _ = u3.shape
    G, HG, dh = NSA_KV_GROUPS, NSA_HPG, HEAD_DIM
    TQ = NSA_TQ
    nqt = s // TQ
    assert NSA_KV_GROUPS == 2 and s % NSA_SEL_TK == 0 and s >= NSA_WINDOW + TQ
    nblk = s // NSA_CMP_STRIDE
    n_cmp = (s - NSA_CMP_BLOCK) // NSA_CMP_STRIDE + 1
    n_sb = s // NSA_SEL_BLOCK
    agg = np.zeros((nblk, n_sb), np.float32)
    agg[:n_cmp] = _sel_agg_matrix(n_cmp, n_sb)
    agg = jnp.asarray(agg)

    k_cmp = _nsa_compress(u3, U_NKC, pe_k, ck_w1, ck_w2, "nsa_compress_k")
    v_cmp = _nsa_compress(u3, U_NVC, pe_v, cv_w1, cv_w2, "nsa_compress_v")

    qspec = pl.BlockSpec((None, TQ, HG * dh), lambda b, g, i: (b, i, U_NQ // (HG * dh) + g))
    TC = NSA_CMP_TQ
    o_cmp, sel = pl.pallas_call(
        _nsa_cmp_body,
        grid=(bsz, G, s // TC),
        in_specs=[pl.BlockSpec((None, TC, HG * dh), lambda b, g, i: (b, i, U_NQ // (HG * dh) + g)),
                  pl.BlockSpec((None, None, nblk, dh), lambda b, g, i: (b, g, 0, 0)),
                  pl.BlockSpec((None, None, nblk, dh), lambda b, g, i: (b, g, 0, 0)),
                  pl.BlockSpec((HG, TC, nblk), lambda b, g, i: (g, i, 0)),
                  pl.BlockSpec((nblk, n_sb), lambda b, g, i: (0, 0))],
        out_specs=[pl.BlockSpec((None, TC, HG * dh), lambda b, g, i: (b, i, g)),
                   pl.BlockSpec((None, None, TC, n_sb), lambda b, g, i: (b, g, i, 0))],
        out_shape=[jax.ShapeDtypeStruct((bsz, s, G * HG * dh), F32),
                   jax.ShapeDtypeStruct((bsz, G, s, n_sb), F32)],
        compiler_params=_cparams("parallel", "parallel", "arbitrary"),
        name="nsa_compressed_attn",
    )(u3, k_cmp, v_cmp, bc, agg)

    def kv(off):
        return pl.BlockSpec((None, s, dh), lambda b, g, i: (b, 0, off // dh + g))

    y = pl.pallas_call(
        _nsa_selwin_body,
        grid=(bsz, G, nqt),
        in_specs=[qspec,
                  pl.BlockSpec((None, TQ, HG * dh), lambda b, g, i: (b, i, g)),
                  pl.BlockSpec((None, None, TQ, n_sb), lambda b, g, i: (b, g, i, 0)),
                  pl.BlockSpec((None, TQ, LANES), lambda b, g, i: (b, i, U_MISC // LANES)),
                  kv(U_NKS), kv(U_NVS), kv(U_NKW), kv(U_NVW),
                  pl.BlockSpec((HG, nqt, TQ, TQ), lambda b, g, i: (g, 0, 0, 0))],
        out_specs=pl.BlockSpec((None, TQ, HG * dh), lambda b, g, i: (b, i, g)),
        out_shape=jax.ShapeDtypeStruct((bsz, s, BRANCH_WIDTH), BF16),
        scratch_shapes=[pltpu.VMEM((s, dh), BF16), pltpu.VMEM((s, dh), BF16),
                        pltpu.VMEM((HG * TQ, 1), F32), pltpu.VMEM((HG * TQ, 1), F32),
                        pltpu.VMEM((HG * TQ, dh), F32)],
        compiler_params=_cparams("parallel", "parallel", "arbitrary"),
        name="nsa_selected_window_attn",
    )(u3, o_cmp, sel, u3, u3, u3, u3, u3, tt)
    return y


def _merge_body(x_ref, wg0, wg1, wg2, wg3, y0, y1, y2, y3, wb_ref, o_ref):
    x = x_ref[...]
    acc = None
    for b, (wg, y) in enumerate(((wg0, y0), (wg1, y1), (wg2, y2), (wg3, y3))):
        gate = _sigmoid(_dot(x, wg[...]))
        term = gate * _dot(y[...], wb_ref[b])
        acc = term if acc is None else acc + term
    o_ref[...] = acc.astype(o_ref.dtype)


def merge_branches(hb, w_gates, ys, w_branch):
    n, d = hb.shape
    tm, tn = min(512, n), min(256, d)
    nj = d // tn

    def wg(b):
        return pl.BlockSpec((d, tn), lambda i, j: (0, b * nj + j))

    yspec = pl.BlockSpec((tm, BRANCH_WIDTH), lambda i, j: (i, 0))
    return pl.pallas_call(
        _merge_body,
        grid=(n // tm, nj),
        in_specs=[pl.BlockSpec((tm, d), lambda i, j: (i, 0)), wg(0), wg(1), wg(2), wg(3),
                  yspec, yspec, yspec, yspec,
                  pl.BlockSpec((N_BRANCH, BRANCH_WIDTH, tn), lambda i, j: (0, 0, j))],
        out_specs=pl.BlockSpec((tm, tn), lambda i, j: (i, j)),
        out_shape=jax.ShapeDtypeStruct((n, d), BF16),
        compiler_params=_cparams("parallel", "arbitrary"),
        name="merge_branches",
    )(hb, w_gates, w_gates, w_gates, w_gates, *ys, w_branch)


MOE_TN = 512
EXPERTS_PER_TILE = MOE_TN // EXPERT_FF


def _router_body(x_ref, w_ref, b_ref, comb_ref, combt_ref):
    logits = _dot(x_ref[...], w_ref[...]) + b_ref[...]
    lane = lax.broadcasted_iota(jnp.int32, logits.shape, 1).astype(F32)
    work = logits
    picks, vals = [], []
    for _ in range(TOP_K):
        mx = jnp.max(work, axis=-1, keepdims=True)
        first = jnp.min(jnp.where(work == mx, lane, float(LANES)), axis=-1, keepdims=True)
        pick = lane == first
        picks.append(pick)
        vals.append(mx)
        work = jnp.where(pick, -jnp.inf, work)
    exps = [jnp.exp(v - vals[0]) for v in vals]
    den = exps[0]
    for e in exps[1:]:
        den = den + e
    comb = jnp.zeros_like(logits)
    for pick, e in zip(picks, exps):
        comb = comb + jnp.where(pick, e / den, 0.0)
    comb_ref[...] = comb
    for t in range(N_EXPERTS // EXPERTS_PER_TILE):
        combt_ref[t] = comb[:, t * EXPERTS_PER_TILE:(t + 1) * EXPERTS_PER_TILE]


def moe_router(hb, router_w, router_b):
    n, d = hb.shape
    tm = min(512, n)
    w = jnp.zeros((d, LANES), F32).at[:, :N_EXPERTS].set(router_w).astype(BF16)
    b = jnp.full((1, LANES), NEG, F32).at[0, :N_EXPERTS].set(router_b)
    nt = N_EXPERTS // EXPERTS_PER_TILE
    return pl.pallas_call(
        _router_body,
        grid=(n // tm,),
        in_specs=[pl.BlockSpec((tm, d), lambda i: (i, 0)), pl.BlockSpec((d, LANES), lambda i: (0, 0)),
                  pl.BlockSpec((1, LANES), lambda i: (0, 0))],
        out_specs=[pl.BlockSpec((tm, LANES), lambda i: (i, 0)),
                   pl.BlockSpec((nt, tm, EXPERTS_PER_TILE), lambda i: (0, i, 0))],
        out_shape=[jax.ShapeDtypeStruct((n, LANES), F32), jax.ShapeDtypeStruct((nt, n, EXPERTS_PER_TILE), F32)],
        compiler_params=_cparams("parallel"),
        name="moe_router",
    )(hb, w, b)


def _moe_up_body(x_ref, wg_ref, wu_ref, bg_ref, bu_ref, comb_ref, a_ref):
    x = x_ref[...]
    gate = jnp.minimum(_dot(x, wg_ref[...]) + bg_ref[...], SWIGLU_LIMIT)
    up = jnp.clip(_dot(x, wu_ref[...]) + bu_ref[...], -SWIGLU_LIMIT, SWIGLU_LIMIT)
    act = gate * _sigmoid(SWIGLU_ALPHA * gate) * (up + 1.0)
    for e in range(EXPERTS_PER_TILE):
        es = slice(e * EXPERT_FF, (e + 1) * EXPERT_FF)
        a_ref[:, es] = (act[:, es] * comb_ref[:, e:e + 1]).astype(BF16)


def moe_up(hb, w_gate, w_up, b_gate, b_up, comb_t):
    n, d = hb.shape
    tm = min(512, n)
    width = N_EXPERTS * EXPERT_FF
    wspec = pl.BlockSpec((d, MOE_TN), lambda i, j: (0, j))
    bspec = pl.BlockSpec((1, MOE_TN), lambda i, j: (0, j))
    return pl.pallas_call(
        _moe_up_body,
        grid=(n // tm, width // MOE_TN),
        in_specs=[pl.BlockSpec((tm, d), lambda i, j: (i, 0)), wspec, wspec, bspec, bspec,
                  pl.BlockSpec((None, tm, EXPERTS_PER_TILE), lambda i, j: (j, i, 0))],
        out_specs=pl.BlockSpec((tm, MOE_TN), lambda i, j: (i, j)),
        out_shape=jax.ShapeDtypeStruct((n, width), BF16),
        compiler_params=_cparams("parallel", "arbitrary"),
        name="moe_up",
    )(hb, w_gate, w_up, b_gate, b_up, comb_t)


def _moe_down_body(a_ref, w_ref, comb_ref, b2_ref, o_ref):
    o_ref[...] = _dot(a_ref[...], w_ref[...]) + _dot(comb_ref[...].astype(BF16), b2_ref[...])


def moe_down(a, w2, comb, b2):
    n, k = a.shape
    d = w2.shape[1]
    tm, tn = min(1024, n), min(512, d)
    return pl.pallas_call(
        _moe_down_body,
        grid=(n // tm, d // tn),
        in_specs=[pl.BlockSpec((tm, k), lambda i, j: (i, 0)), pl.BlockSpec((k, tn), lambda i, j: (0, j)),
                  pl.BlockSpec((tm, LANES), lambda i, j: (i, 0)), pl.BlockSpec((LANES, tn), lambda i, j: (0, j))],
        out_specs=pl.BlockSpec((tm, tn), lambda i, j: (i, j)),
        out_shape=jax.ShapeDtypeStruct((n, d), F32),
        compiler_params=_cparams("parallel", "arbitrary"),
        name="moe_down",
    )(a, w2, comb, b2)


def _regroup_w_in(w):
    d = w.shape[0]
    o = _SRC_OFF
    pad = U_WIDTH - U_MISC - (GLA_GATE_RANK + 2 * N_HEADS + 3 * N_HEADS)
    w_u = jnp.concatenate([
        w[:, o[0]:o[4]],
        w[:, o[5]:o[8]],
        w[:, o[10]:o[13]],
        w[:, o[13]:o[20]],
        w[:, o[4]:o[5]], w[:, o[8]:o[10]], w[:, o[20]:o[21]],
        jnp.zeros((d, pad), w.dtype),
    ], axis=1).astype(BF16)
    return w_u, w[:, SRC_GATES:].astype(BF16)


def kernel(x, rel_bias, w_in, gla_gate_w, gla_gate_b, gla_norm_g, mlstm_conv_w, mlstm_conv_b, mlstm_igate_b,
           mlstm_fgate_b, mlstm_norm_g, nsa_pe_k, nsa_pe_v, nsa_ck_w1, nsa_ck_w2, nsa_cv_w1, nsa_cv_w2, w_branch,
           w_out, ln1_g, ln1_b, router_w, router_b, exp_w1, exp_b1, exp_w2, exp_b2, ln2_g, ln2_b):
    out_dtype = x.dtype
    bsz, s, d = x.shape
    n = bsz * s
    h = x.astype(F32).reshape(n, d)
    hb = h.astype(BF16)
    thr = bucket_thresholds(max(s, BIAS_DIST_RANGE))
    tab_t = rel_bias.T
    dil_bias = dilated_bias(thr, tab_t)
    nsa_tt, nsa_bc = nsa_bias(thr, tab_t, s)
    for l in range(w_in.shape[0]):
        w_u, w_gates = _regroup_w_in(w_in[l])
        u = matmul(hb, w_u, F32, 1024, 512, "input_projection")
        u3 = u.reshape(bsz, s, U_WIDTH)
        y_a = gla_mixer(u3, gla_gate_w[l], gla_gate_b[l], gla_norm_g[l]).reshape(n, BRANCH_WIDTH)
        y_b = mlstm_mixer(u3, mlstm_conv_w[l], mlstm_conv_b[l], mlstm_igate_b[l], mlstm_fgate_b[l],
                          mlstm_norm_g[l]).reshape(n, BRANCH_WIDTH)
        y_c = dilated_mixer(u3, dil_bias).reshape(n, BRANCH_WIDTH)
        y_d = nsa_mixer(u3, nsa_tt, nsa_bc, nsa_pe_k[l], nsa_pe_v[l], nsa_ck_w1[l], nsa_ck_w2[l], nsa_cv_w1[l],
                        nsa_cv_w2[l]).reshape(n, BRANCH_WIDTH)
        merged = merge_branches(hb, w_gates, (y_a, y_b, y_c, y_d), w_branch[l].astype(BF16))
        attn = matmul(merged, w_out[l].astype(BF16), F32, 1024, 512, "output_projection")
        h, hb = ln_residual(h, attn, ln1_g[l], ln1_b[l], "layer_norm_1")

        comb, comb_t = moe_router(hb, router_w[l], router_b[l])
        w1 = exp_w1[l]
        w_gate = w1[:, :, :EXPERT_FF].transpose(1, 0, 2).reshape(d, N_EXPERTS * EXPERT_FF).astype(BF16)
        w_up = w1[:, :, EXPERT_FF:].transpose(1, 0, 2).reshape(d, N_EXPERTS * EXPERT_FF).astype(BF16)
        b_gate = exp_b1[l][:, :EXPERT_FF].reshape(1, -1)
        b_up = exp_b1[l][:, EXPERT_FF:].reshape(1, -1)
        act = moe_up(hb, w_gate, w_up, b_gate, b_up, comb_t)
        b2 = jnp.zeros((LANES, d), F32).at[:N_EXPERTS].set(exp_b2[l]).astype(BF16)
        ffn = moe_down(act, exp_w2[l].reshape(N_EXPERTS * EXPERT_FF, d).astype(BF16), comb, b2)
        h, hb = ln_residual(h, ffn, ln2_g[l], ln2_b[l], "layer_norm_2")
    return h.reshape(bsz, s, d).astype(out_dtype)
```

```python
import functools
import math

import numpy as np
import jax
import jax.numpy as jnp
from jax import lax
from jax.experimental import pallas as pl
from jax.experimental.pallas import tpu as pltpu

F32 = jnp.float32
BF16 = jnp.bfloat16
HIGHEST = lax.Precision.HIGHEST

N_LAYERS_FOR_DEEPNORM = 4
HEAD_DIM = 128
BRANCH_WIDTH = 1024
N_BRANCH = 4
N_HEADS = 8
DK = 64
CHUNK = 64
GLA_GATE_RANK = 16
GLA_TAU = 16.0
MLSTM_CONV = 4
MLSTM_IGATE_CAP = 15.0
DIL_PATTERNS = ((128, 1), (512, 4), (2048, 16))
DIL_BLOCK = 128
NSA_KV_GROUPS = 2
NSA_HPG = 4
NSA_CMP_BLOCK = 32
NSA_CMP_STRIDE = 16
NSA_CMP_HIDDEN = 256
NSA_SEL_BLOCK = 64
NSA_N_SEL = 16
NSA_WINDOW = 512
NSA_SEL_TK = 512
REL_BUCKETS = 32
REL_MAX_DIST = 2048
N_EXPERTS = 32
TOP_K = 4
EXPERT_FF = 128
SWIGLU_LIMIT = 7.0
SWIGLU_ALPHA = 1.702
DEEPNORM_ALPHA = (2 * N_LAYERS_FOR_DEEPNORM) ** 0.25
LN_EPS = 1e-5
NEG = -1e30
BIG = 1e9

LANES = 128
VMEM_LIMIT_BYTES = 56 * 1024 * 1024

U_GQ, U_GK, U_GV, U_GR = 0, 512, 1024, 2048
U_MQK, U_MV, U_MO = 3072, 4096, 5120
U_DQ, U_DK, U_DV = 6144, 9216, 10240
U_NQ = 11264
U_NKC, U_NVC, U_NKS, U_NVS, U_NKW, U_NVW = 12288, 12544, 12800, 13056, 13312, 13568
U_MISC = 13824
U_WIDTH = 14336
MISC_GA, MISC_MI, MISC_MF, MISC_NG = 0, 16, 24, 32

_SRC_SIZES = (512, 512, 1024, 1024, 16, 1024, 1024, 1024, 8, 8, 3072, 1024, 1024, 1024,
              256, 256, 256, 256, 256, 256, 24)
_SRC_OFF = np.concatenate([[0], np.cumsum(_SRC_SIZES)]).tolist()
SRC_GATES = _SRC_OFF[-1]


def _cparams(*sem):
    return pltpu.CompilerParams(dimension_semantics=sem, vmem_limit_bytes=VMEM_LIMIT_BYTES)


def _log_sigmoid(x):
    return jnp.minimum(x, 0.0) - jnp.log1p(jnp.exp(-jnp.abs(x)))


def _sigmoid(x):
    return 1.0 / (1.0 + jnp.exp(-x))


def _silu(x):
    return x * _sigmoid(x)


def _dot(a, b):
    return jnp.dot(a, b, preferred_element_type=F32)


def _dot_t(a, b):
    return lax.dot_general(a, b, (((1,), (1,)), ((), ())), preferred_element_type=F32)


def _tdot(a, b):
    return lax.dot_general(a, b, (((0,), (0,)), ((), ())), preferred_element_type=F32)


def _head_norm(o, g_row):
    mu = jnp.mean(o, axis=-1, keepdims=True)
    d = o - mu
    var = jnp.mean(d * d, axis=-1, keepdims=True)
    return d * lax.rsqrt(var + LN_EPS) * g_row


def t5_bucket(dist):
    d = jnp.maximum(dist, 0)
    exact = REL_BUCKETS // 2
    df = jnp.maximum(d, 1).astype(jnp.float32)
    large = exact + (jnp.log(df / exact) / math.log(REL_MAX_DIST / exact) * (REL_BUCKETS - exact)).astype(jnp.int32)
    return jnp.where(d < exact, d, jnp.minimum(large, REL_BUCKETS - 1))


def _mm_body(x_ref, w_ref, o_ref):
    o_ref[...] = _dot(x_ref[...], w_ref[...]).astype(o_ref.dtype)


def matmul(x, w, out_dtype, tm, tn, name):
    m, k = x.shape
    n = w.shape[1]
    tm, tn = min(tm, m), min(tn, n)
    return pl.pallas_call(
        _mm_body,
        grid=(m // tm, n // tn),
        in_specs=[pl.BlockSpec((tm, k), lambda i, j: (i, 0)), pl.BlockSpec((k, tn), lambda i, j: (0, j))],
        out_specs=pl.BlockSpec((tm, tn), lambda i, j: (i, j)),
        out_shape=jax.ShapeDtypeStruct((m, n), out_dtype),
        compiler_params=_cparams("parallel", "arbitrary"),
        name=name,
    )(x, w)


def _ln_body(h_ref, d_ref, g_ref, b_ref, o_ref, ob_ref):
    z = DEEPNORM_ALPHA * h_ref[...] + d_ref[...]
    mu = jnp.mean(z, axis=-1, keepdims=True)
    zc = z - mu
    var = jnp.mean(zc * zc, axis=-1, keepdims=True)
    y = zc * lax.rsqrt(var + LN_EPS) * g_ref[...] + b_ref[...]
    o_ref[...] = y
    ob_ref[...] = y.astype(BF16)


def ln_residual(h, delta, g, b, name):
    n, d = h.shape
    tm = min(256, n)
    row = pl.BlockSpec((tm, d), lambda i: (i, 0))
    vec = pl.BlockSpec((1, d), lambda i: (0, 0))
    return pl.pallas_call(
        _ln_body,
        grid=(n // tm,),
        in_specs=[row, row, vec, vec],
        out_specs=[row, row],
        out_shape=[jax.ShapeDtypeStruct((n, d), F32), jax.ShapeDtypeStruct((n, d), BF16)],
        compiler_params=_cparams("parallel"),
        name=name,
    )(h, delta, g.reshape(1, d), b.reshape(1, d))


def _gla_body(q_ref, k_ref, v_ref, r_ref, misc_ref, gw_ref, gb_ref, ng_ref, o_ref, st_ref):
    L = CHUNK

    @pl.when(pl.program_id(1) == 0)
    def _():
        st_ref[...] = jnp.zeros_like(st_ref)

    pre = _dot(misc_ref[...].astype(BF16), gw_ref[...]) + gb_ref[...]
    log_a = _log_sigmoid(pre) / GLA_TAU
    row = lax.broadcasted_iota(jnp.int32, (L, L), 0)
    col = lax.broadcasted_iota(jnp.int32, (L, L), 1)
    causal = col <= row
    tri = jnp.where(causal, 1.0, 0.0).astype(F32)
    b = jnp.dot(tri, log_a, precision=HIGHEST, preferred_element_type=F32)
    b_last = b[L - 1:L, :]
    q_dec = (q_ref[...] * (DK ** -0.5) * jnp.exp(b)).astype(BF16)
    k_dec = (k_ref[...] * jnp.exp(-b)).astype(BF16)
    k_end = (k_ref[...] * jnp.exp(b_last - b)).astype(BF16)
    decay = jnp.exp(b_last)

    def heads(x, width):
        return jnp.stack([x[:, h * width:(h + 1) * width] for h in range(N_HEADS)], axis=0)

    q3, k3, ke3 = heads(q_dec, DK), heads(k_dec, DK), heads(k_end, DK)
    v3 = heads(v_ref[...].astype(BF16), HEAD_DIM)
    dec3 = heads(decay, DK)
    state_t = st_ref[...]
    att = jnp.where(causal[None], jnp.einsum('hqd,hkd->hqk', q3, k3, preferred_element_type=F32), 0.0)
    o3 = (jnp.einsum('hqk,hkv->hqv', att.astype(BF16), v3, preferred_element_type=F32)
          + jnp.einsum('hqd,hvd->hqv', q3, state_t.astype(BF16), preferred_element_type=F32))
    st_ref[...] = dec3 * state_t + jnp.einsum('hkv,hkd->hvd', v3, ke3, preferred_element_type=F32)
    on3 = _head_norm(o3, heads(ng_ref[...], HEAD_DIM))
    out3 = (_silu(heads(r_ref[...], HEAD_DIM)) * on3).astype(BF16)
    for h in range(N_HEADS):
        o_ref[:, h * HEAD_DIM:(h + 1) * HEAD_DIM] = out3[h]


def gla_mixer(u3, gate_w, gate_b, norm_g):
    bsz, s, _ = u3.shape
    L = CHUNK
    gw = jnp.zeros((LANES, N_HEADS * DK), F32).at[MISC_GA:MISC_GA + GLA_GATE_RANK].set(gate_w).astype(BF16)

    def ublock(width, off):
        return pl.BlockSpec((None, L, width), lambda b, t: (b, t, off // width))

    def const(shape):
        return pl.BlockSpec(shape, lambda b, t: (0,) * len(shape))

    return pl.pallas_call(
        _gla_body,
        grid=(bsz, s // L),
        in_specs=[ublock(512, U_GQ), ublock(512, U_GK), ublock(1024, U_GV), ublock(1024, U_GR),
                  ublock(LANES, U_MISC), const((LANES, 512)), const((1, 512)), const((1, 1024))],
        out_specs=pl.BlockSpec((None, L, 1024), lambda b, t: (b, t, 0)),
        out_shape=jax.ShapeDtypeStruct((bsz, s, BRANCH_WIDTH), BF16),
        scratch_shapes=[pltpu.VMEM((N_HEADS, HEAD_DIM, DK), F32)],
        compiler_params=_cparams("parallel", "arbitrary"),
        name="gla_mixer",
    )(u3, u3, u3, u3, u3, gw, gate_b.reshape(1, -1), norm_g.reshape(1, -1))


def _mlstm_body(qk_ref, v_ref, op_ref, misc_ref, cw_ref, cb_ref, ib_ref, fb_ref, ng_ref, o_ref,
                ext_ref, c_ref, n_ref, m_ref):
    L = CHUNK
    C2 = 2 * N_HEADS * DK

    @pl.when(pl.program_id(1) == 0)
    def _():
        ext_ref[0:8, :] = jnp.zeros((8, C2), F32)
        c_ref[...] = jnp.zeros_like(c_ref)
        n_ref[...] = jnp.zeros_like(n_ref)
        m_ref[...] = jnp.zeros_like(m_ref)

    x = qk_ref[...]
    ext_ref[8:8 + L, :] = x
    y = (cb_ref[...] + cw_ref[3:4, :] * x + cw_ref[2:3, :] * ext_ref[pl.ds(7, L), :]
         + cw_ref[1:2, :] * ext_ref[pl.ds(6, L), :] + cw_ref[0:1, :] * ext_ref[pl.ds(5, L), :])
    ext_ref[0:8, :] = x[L - 8:L, :]
    qk = _silu(y)
    q_all = qk[:, :N_HEADS * DK].astype(BF16)
    k_all = qk[:, N_HEADS * DK:] * (DK ** -0.5)

    misc = misc_ref[...]
    i_g = MLSTM_IGATE_CAP * jnp.tanh((misc + ib_ref[...]) / MLSTM_IGATE_CAP)
    log_f = _log_sigmoid(misc + fb_ref[...])
    row = lax.broadcasted_iota(jnp.int32, (L, L), 0)
    col = lax.broadcasted_iota(jnp.int32, (L, L), 1)
    causal = col <= row
    tri = jnp.where(causal, 1.0, 0.0).astype(F32)
    b = jnp.dot(tri, log_f, precision=HIGHEST, preferred_element_type=F32)
    it = pltpu.roll(i_g, MISC_MF - MISC_MI, 1)
    b_last = b[L - 1:L, :]
    m_s = m_ref[0:1, :]
    a_end = b_last - b + it
    m_new = jnp.maximum(b_last + m_s, jnp.max(a_end, axis=0, keepdims=True))
    w = jnp.exp(a_end - m_new)
    sc = jnp.exp(b_last + m_s - m_new)
    inter_log = b + m_s
    c = it - b
    rows_i = lax.broadcasted_iota(jnp.int32, c.shape, 0)
    cmax = c
    for sh in (1, 2, 4, 8, 16, 32):
        cmax = jnp.maximum(cmax, jnp.where(rows_i >= sh, pltpu.roll(cmax, sh, 0), -jnp.inf))
    m_i = jnp.maximum(inter_log, b + cmax)
    w_inter = jnp.exp(inter_log - m_i)
    e_neg = jnp.exp(-m_i)
    m_ref[0:1, :] = m_new

    stats = jnp.concatenate([b - m_i, w, w_inter, e_neg, jnp.broadcast_to(sc, (8, LANES))], axis=0)
    lane = lax.broadcasted_iota(jnp.int32, stats.shape, 1)
    stats = jnp.where((lane >= MISC_MF) & (lane < MISC_MF + N_HEADS), stats, 0.0)
    sel_r = lax.broadcasted_iota(jnp.int32, (LANES, N_HEADS * LANES), 0)
    sel_c = lax.broadcasted_iota(jnp.int32, (LANES, N_HEADS * LANES), 1) // LANES
    onehot = jnp.where(sel_r == sel_c + MISC_MF, 1.0, 0.0).astype(F32)
    spread = jnp.dot(stats, onehot, precision=HIGHEST, preferred_element_type=F32)

    def heads(x, width):
        return jnp.stack([x[:, h * width:(h + 1) * width] for h in range(N_HEADS)], axis=0)

    dcol3 = heads(spread[0:L], LANES)[:, :, 0:L]
    w3 = heads(spread[L:2 * L], LANES)[:, :, 0:DK]
    wi3 = heads(spread[2 * L:3 * L], LANES)
    en3 = heads(spread[3 * L:4 * L], LANES)
    sc3 = heads(spread[4 * L:4 * L + 1], LANES)
    c_t = c.T
    crow3 = jnp.stack([c_t[MISC_MF + h:MISC_MF + h + 1, :] for h in range(N_HEADS)], axis=0)

    q3 = heads(q_all, DK)
    k3 = heads(k_all, DK)
    v3 = heads(v_ref[...].astype(BF16), HEAD_DIM)
    c_s = c_ref[...]
    n_s = n_ref[...]
    decay3 = jnp.exp(jnp.where(causal[None], dcol3 + crow3, -jnp.inf))
    qk_s = jnp.einsum('hqd,hkd->hqk', q3, k3.astype(BF16), preferred_element_type=F32) * decay3
    num = (wi3 * jnp.einsum('hqd,hdv->hqv', q3, c_s.astype(BF16), preferred_element_type=F32)
           + jnp.einsum('hqk,hkv->hqv', qk_s.astype(BF16), v3, preferred_element_type=F32))
    qn = jnp.sum(q3.astype(F32) * n_s, axis=-1, keepdims=True)
    den = wi3[:, :, 0:1] * qn + jnp.sum(qk_s, axis=-1, keepdims=True)
    hout = num / jnp.maximum(jnp.abs(den), en3[:, :, 0:1])
    wk3 = w3 * k3
    c_ref[...] = sc3 * c_s + jnp.einsum('hkd,hkv->hdv', wk3.astype(BF16), v3, preferred_element_type=F32)
    n_ref[...] = sc3[:, :, 0:DK] * n_s + jnp.sum(wk3, axis=1, keepdims=True)
    hn = _head_norm(hout, heads(ng_ref[...], HEAD_DIM))
    out3 = (_sigmoid(heads(op_ref[...], HEAD_DIM)) * hn).astype(BF16)
    for h in range(N_HEADS):
        o_ref[:, h * HEAD_DIM:(h + 1) * HEAD_DIM] = out3[h]


def mlstm_mixer(u3, conv_w, conv_b, igate_b, fgate_b, norm_g):
    bsz, s, _ = u3.shape
    L = CHUNK
    ib = jnp.zeros((1, LANES), F32).at[0, MISC_MI:MISC_MI + N_HEADS].set(igate_b)
    fb = jnp.zeros((1, LANES), F32).at[0, MISC_MF:MISC_MF + N_HEADS].set(fgate_b)

    def ublock(width, off):
        return pl.BlockSpec((None, L, width), lambda b, t: (b, t, off // width))

    def const(shape):
        return pl.BlockSpec(shape, lambda b, t: (0,) * len(shape))

    return pl.pallas_call(
        _mlstm_body,
        grid=(bsz, s // L),
        in_specs=[ublock(1024, U_MQK), ublock(1024, U_MV), ublock(1024, U_MO), ublock(LANES, U_MISC),
                  const((MLSTM_CONV, 1024)), const((1, 1024)), const((1, LANES)), const((1, LANES)),
                  const((1, 1024))],
        out_specs=pl.BlockSpec((None, L, 1024), lambda b, t: (b, t, 0)),
        out_shape=jax.ShapeDtypeStruct((bsz, s, BRANCH_WIDTH), BF16),
        scratch_shapes=[pltpu.VMEM((L + 8, 1024), F32), pltpu.VMEM((N_HEADS, DK, HEAD_DIM), F32),
                        pltpu.VMEM((N_HEADS, 1, DK), F32), pltpu.VMEM((8, LANES), F32)],
        compiler_params=_cparams("parallel", "arbitrary"),
        name="mlstm_mixer",
    )(u3, u3, u3, u3, conv_w, conv_b.reshape(1, -1), ib, fb, norm_g.reshape(1, -1))


BIAS_DIST_RANGE = 4096


def bucket_thresholds(max_dist):
    lut = t5_bucket(jnp.arange(max_dist))
    return jnp.sum(lut[None, :] < jnp.arange(REL_BUCKETS)[:, None], axis=1).astype(jnp.int32)


def _bias_table_body(thr_ref, tab_ref, o_ref, *, kind):
    a = pl.program_id(0)
    c = pl.program_id(1)
    shape = o_ref.shape
    i = lax.broadcasted_iota(jnp.int32, shape, 0)
    j = lax.broadcasted_iota(jnp.int32, shape, 1)
    if kind == "dilated":
        dil = jnp.where(a == 0, DIL_PATTERNS[0][1], jnp.where(a == 1, DIL_PATTERNS[1][1], DIL_PATTERNS[2][1]))
        dist = (i + DIL_BLOCK - j) * dil
        head = a * N_HEADS + c
    elif kind == "toeplitz":
        dist = c * shape[0] + i - j
        head = 3 * N_HEADS + a
    else:
        dist = c * shape[0] + i - (j * NSA_CMP_STRIDE + NSA_CMP_BLOCK - 1)
        head = 3 * N_HEADS + a
    dist = jnp.maximum(dist, 0)
    acc = jnp.full(shape, tab_ref[head, 0], F32)
    for k in range(1, REL_BUCKETS):
        acc = jnp.where(dist >= thr_ref[k], tab_ref[head, k], acc)
    o_ref[...] = acc


def bias_table(thr, tab_t, kind, out_dims, block):
    smem = pl.BlockSpec(memory_space=pltpu.SMEM)
    return pl.pallas_call(
        functools.partial(_bias_table_body, kind=kind),
        grid=out_dims[:2] if kind != "compressed" else (out_dims[0], out_dims[1] // block[0]),
        in_specs=[smem, smem],
        out_specs=(pl.BlockSpec((None, None) + block, lambda a, c: (a, c, 0, 0)) if kind != "compressed"
                   else pl.BlockSpec((None,) + block, lambda a, c: (a, c, 0))),
        out_shape=jax.ShapeDtypeStruct(out_dims, F32),
        compiler_params=_cparams("parallel", "parallel"),
        name=f"bias_table_{kind}",
    )(thr, tab_t)


DIL_SPAN = 2048


def _dil_body(q0_ref, q1_ref, q2_ref, kp_ref, kc_ref, vp_ref, vc_ref, bias_ref, o_ref, m_scr, l_scr, acc_scr):
    P = DIL_BLOCK
    NB = DIL_SPAN // P
    q_refs = (q0_ref, q1_ref, q2_ref)
    row = lax.broadcasted_iota(jnp.int32, (P, P), 0)
    col = lax.broadcasted_iota(jnp.int32, (P, P), 1)
    mask_cur = jnp.where(col <= row, 0.0, NEG)
    mask_prev = jnp.where(col >= row, 0.0, NEG)
    no_prev = jnp.where(pl.program_id(1) > 0, 0, P)
    mask_prev_first = jnp.where((col - row) >= no_prev, 0.0, NEG)

    def rows(r, n, dil):
        return pl.ds(r, n, stride=dil) if dil > 1 else pl.ds(r, n)

    def stacked(ref, dil):
        per = DIL_SPAN // dil
        return jnp.concatenate([ref[rows(r, per, dil), :].reshape(per // P, P, HEAD_DIM) for r in range(dil)], axis=0)

    def stacked_prev(cur3, prev_ref, dil):
        nblk = NB // dil
        parts = []
        for r in range(dil):
            parts.append(prev_ref[rows(r + DIL_SPAN - dil * P, P, dil), :].reshape(1, P, HEAD_DIM))
            if nblk > 1:
                parts.append(cur3[r * nblk:(r + 1) * nblk - 1])
        return jnp.concatenate(parts, axis=0)

    for g, (_, dil) in enumerate(DIL_PATTERNS):
        nblk = NB // dil
        per = DIL_SPAN // dil
        q3 = (stacked(q_refs[g], dil) * (HEAD_DIM ** -0.5)).astype(BF16)
        kc3 = stacked(kc_ref, dil)
        vc3 = stacked(vc_ref, dil)
        kp3 = stacked_prev(kc3, kp_ref, dil).astype(BF16)
        vp3 = stacked_prev(vc3, vp_ref, dil).astype(BF16)
        kc3 = kc3.astype(BF16)
        vc3 = vc3.astype(BF16)
        bias_p = bias_ref[g, :, 0:P]
        bias_prev3 = jnp.concatenate(
            [(bias_p + (mask_prev_first if b % nblk == 0 else mask_prev)).reshape(1, P, P) for b in range(NB)], axis=0)
        s_p = jnp.einsum('nqd,nkd->nqk', q3, kp3, preferred_element_type=F32) + bias_prev3
        s_c = (jnp.einsum('nqd,nkd->nqk', q3, kc3, preferred_element_type=F32)
               + (bias_ref[g, :, P:2 * P] + mask_cur)[None])
        m = jnp.maximum(jnp.max(s_p, axis=-1, keepdims=True), jnp.max(s_c, axis=-1, keepdims=True))
        p_p = jnp.exp(s_p - m)
        p_c = jnp.exp(s_c - m)
        l = jnp.sum(p_p, axis=-1, keepdims=True) + jnp.sum(p_c, axis=-1, keepdims=True)
        acc = (jnp.einsum('nqk,nkd->nqd', p_p.astype(BF16), vp3, preferred_element_type=F32)
               + jnp.einsum('nqk,nkd->nqd', p_c.astype(BF16), vc3, preferred_element_type=F32))
        for r in range(dil):
            sl = rows(g * DIL_SPAN + r, per, dil)
            bs = slice(r * nblk, (r + 1) * nblk)
            m_scr[sl, :] = jnp.broadcast_to(m[bs].reshape(per, 1), (per, LANES))
            l_scr[sl, :] = jnp.broadcast_to(l[bs].reshape(per, 1), (per, LANES))
            acc_scr[sl, :] = acc[bs].reshape(per, HEAD_DIM)

    ms = [m_scr[g * DIL_SPAN:(g + 1) * DIL_SPAN, :] for g in range(len(DIL_PATTERNS))]
    m_all = jnp.maximum(jnp.maximum(ms[0], ms[1]), ms[2])
    num = jnp.zeros((DIL_SPAN, HEAD_DIM), F32)
    den = jnp.zeros((DIL_SPAN, LANES), F32)
    for g in range(len(DIL_PATTERNS)):
        w = jnp.exp(ms[g] - m_all)
        num = num + w * acc_scr[g * DIL_SPAN:(g + 1) * DIL_SPAN, :]
        den = den + w * l_scr[g * DIL_SPAN:(g + 1) * DIL_SPAN, :]
    o_ref[...] = (num / den).astype(BF16)


def dilated_bias(thr, tab_t):
    P = DIL_BLOCK
    return bias_table(thr, tab_t, "dilated", (len(DIL_PATTERNS), N_HEADS, P, 2 * P), (P, 2 * P))


def dilated_mixer(u3, bias):
    bsz, s, _ = u3.shape
    P = DIL_BLOCK
    assert s % DIL_SPAN == 0 and all(w // d == P and DIL_SPAN % (d * P) == 0 for w, d in DIL_PATTERNS)

    def ub(off, prev):
        def idx(b, t, h):
            return (b, jnp.maximum(t - 1, 0) if prev else t, off // HEAD_DIM + h)
        return pl.BlockSpec((None, DIL_SPAN, HEAD_DIM), idx)

    return pl.pallas_call(
        _dil_body,
        grid=(bsz, s // DIL_SPAN, N_HEADS),
        in_specs=[ub(U_DQ, False), ub(U_DQ + 1024, False), ub(U_DQ + 2048, False),
                  ub(U_DK, True), ub(U_DK, False), ub(U_DV, True), ub(U_DV, False),
                  pl.BlockSpec((len(DIL_PATTERNS), None, P, 2 * P), lambda b, t, h: (0, h, 0, 0))],
        out_specs=pl.BlockSpec((None, DIL_SPAN, HEAD_DIM), lambda b, t, h: (b, t, h)),
        out_shape=jax.ShapeDtypeStruct((bsz, s, BRANCH_WIDTH), BF16),
        scratch_shapes=[pltpu.VMEM((len(DIL_PATTERNS) * DIL_SPAN, LANES), F32),
                        pltpu.VMEM((len(DIL_PATTERNS) * DIL_SPAN, LANES), F32),
                        pltpu.VMEM((len(DIL_PATTERNS) * DIL_SPAN, HEAD_DIM), F32)],
        compiler_params=_cparams("parallel", "parallel", "arbitrary"),
        name="dilated_attn",
    )(u3, u3, u3, u3, u3, u3, u3, bias)


def _gelu_tanh(x):
    return 0.5 * x * (1.0 + jnp.tanh(math.sqrt(2.0 / math.pi) * (x + 0.044715 * (x * x * x))))


def _nsa_compress_body(t_ref, pe_ref, w1_ref, w2_ref, o_ref):
    nblk = t_ref.shape[0] // NSA_CMP_STRIDE
    half = NSA_CMP_STRIDE
    t1 = jnp.zeros((nblk, NSA_CMP_HIDDEN), F32)
    t2 = jnp.zeros((nblk, NSA_CMP_HIDDEN), F32)
    for p in range(half):
        xp = t_ref[pl.ds(p, nblk, stride=NSA_CMP_STRIDE), :]
        t1 = t1 + _dot((xp + pe_ref[p:p + 1, :]).astype(BF16), w1_ref[p * HEAD_DIM:(p + 1) * HEAD_DIM, :])
        t2 = t2 + _dot((xp + pe_ref[half + p:half + p + 1, :]).astype(BF16),
                       w1_ref[(half + p) * HEAD_DIM:(half + p + 1) * HEAD_DIM, :])
    hidden = t1 + pltpu.roll(t2, nblk - 1, 0)
    o_ref[...] = _dot(_gelu_tanh(hidden).astype(BF16), w2_ref[...])


def _nsa_compress(u3, off, pe, w1, w2, name):
    bsz, s, _ = u3.shape
    nblk = s // NSA_CMP_STRIDE
    G = NSA_KV_GROUPS
    return pl.pallas_call(
        _nsa_compress_body,
        grid=(bsz, G),
        in_specs=[pl.BlockSpec((None, s, HEAD_DIM), lambda b, g: (b, 0, off // HEAD_DIM + g)),
                  pl.BlockSpec((NSA_CMP_BLOCK, HEAD_DIM), lambda b, g: (0, 0)),
                  pl.BlockSpec((NSA_CMP_BLOCK * HEAD_DIM, NSA_CMP_HIDDEN), lambda b, g: (0, 0)),
                  pl.BlockSpec((NSA_CMP_HIDDEN, HEAD_DIM), lambda b, g: (0, 0))],
        out_specs=pl.BlockSpec((None, None, nblk, HEAD_DIM), lambda b, g: (b, g, 0, 0)),
        out_shape=jax.ShapeDtypeStruct((bsz, G, nblk, HEAD_DIM), F32),
        compiler_params=_cparams("parallel", "parallel"),
        name=name,
    )(u3, pe, w1.astype(BF16), w2.astype(BF16))


def _nsa_cmp_body(q_ref, kc_ref, vc_ref, bc_ref, agg_ref, oc_ref, sel_ref):
    TQ = q_ref.shape[0]
    ncb = kc_ref.shape[0]
    nsb = agg_ref.shape[1]
    t0 = pl.program_id(2) * TQ
    tpos = t0 + lax.broadcasted_iota(jnp.int32, (TQ, ncb), 0)
    ends = lax.broadcasted_iota(jnp.int32, (TQ, ncb), 1) * NSA_CMP_STRIDE + (NSA_CMP_BLOCK - 1)
    mask = ends <= tpos
    maskf = jnp.where(mask, 1.0, 0.0).astype(F32)
    kcb = kc_ref[...].astype(BF16)
    vcb = vc_ref[...].astype(BF16)
    psum = jnp.zeros((TQ, ncb), F32)
    for hg in range(NSA_HPG):
        hs = slice(hg * HEAD_DIM, (hg + 1) * HEAD_DIM)
        q = (q_ref[:, hs] * (HEAD_DIM ** -0.5)).astype(BF16)
        s = jnp.where(mask, _dot_t(q, kcb) + bc_ref[hg], NEG)
        p = jnp.exp(s - jnp.max(s, axis=-1, keepdims=True)) * maskf
        l = jnp.sum(p, axis=-1, keepdims=True)
        p = p / jnp.where(l > 0, l, 1.0)
        oc_ref[:, hs] = _dot(p.astype(BF16), vcb)
        psum = psum + p
    imp = jnp.dot(psum, agg_ref[...], precision=HIGHEST, preferred_element_type=F32)
    jblk = lax.broadcasted_iota(jnp.int32, (TQ, nsb), 1)
    tq = t0 + lax.broadcasted_iota(jnp.int32, (TQ, nsb), 0)
    cur = tq // NSA_SEL_BLOCK
    forced = (jblk == 0) | (jblk == cur) | (jblk == cur - 1)
    score = jnp.where(forced, BIG, jnp.where(jblk * NSA_SEL_BLOCK <= tq, imp, -BIG))
    jf = jblk.astype(F32)
    sel = jnp.zeros((TQ, nsb), F32)
    for _ in range(min(NSA_N_SEL, nsb)):
        mx = jnp.max(score, axis=-1, keepdims=True)
        first = jnp.min(jnp.where(score == mx, jf, float(nsb)), axis=-1, keepdims=True)
        pick = jf == first
        sel = jnp.where(pick, 1.0, sel)
        score = jnp.where(pick, -jnp.inf, score)
    sel_ref[...] = sel


def _sel_agg_matrix(n_cmp, n_sb):
    r = NSA_SEL_BLOCK // NSA_CMP_STRIDE
    c = NSA_CMP_BLOCK // NSA_CMP_STRIDE
    jj, aa, bb = np.meshgrid(np.arange(n_sb), np.arange(r), np.arange(c), indexing='ij')
    ii = r * jj + aa + bb - 1
    ok = (ii >= 0) & (ii < n_cmp)
    mat = np.zeros((n_cmp, n_sb), np.float32)
    np.add.at(mat, (ii[ok], jj[ok]), 1.0)
    return mat


def _nsa_selwin_body(q_ref, oc_ref, sel_ref, misc_ref, ks_ref, vs_ref, kw_ref, vw_ref, tt_ref, o_ref,
                     ksb_ref, vsb_ref, m_ref, l_ref, acc_ref):
    TQ = q_ref.shape[0]
    TK = NSA_SEL_TK
    HG = NSA_HPG
    nsb = sel_ref.shape[1]
    sub = TK // TQ
    g = pl.program_id(1)
    qi = pl.program_id(2)

    @pl.when(qi == 0)
    def _():
        ksb_ref[...] = ks_ref[...].astype(BF16)
        vsb_ref[...] = vs_ref[...].astype(BF16)

    q4 = jnp.concatenate([(q_ref[:, hg * HEAD_DIM:(hg + 1) * HEAD_DIM] * (HEAD_DIM ** -0.5)).astype(BF16)
                          for hg in range(HG)], axis=0)
    sel = sel_ref[...].astype(BF16)
    tpos = qi * TQ + lax.broadcasted_iota(jnp.int32, (TQ, TK), 0)
    kcol = lax.broadcasted_iota(jnp.int32, (TQ, TK), 1)
    e_row = lax.broadcasted_iota(jnp.int32, (nsb, TK), 0)
    e_col = lax.broadcasted_iota(jnp.int32, (nsb, TK), 1) // NSA_SEL_BLOCK

    m_ref[...] = jnp.full_like(m_ref, 0.1 * NEG)
    l_ref[...] = jnp.zeros_like(l_ref)
    acc_ref[...] = jnp.zeros_like(acc_ref)

    def sel_step(kj, carry):
        k0 = pl.multiple_of(kj * TK, TK)
        k_tile = ksb_ref[pl.ds(k0, TK), :]
        v_tile = vsb_ref[pl.ds(k0, TK), :]
        expand = jnp.where(e_row == kj * (TK // NSA_SEL_BLOCK) + e_col, 1.0, 0.0).astype(BF16)
        chosen = _dot(sel, expand) > 0.5
        mask_add = jnp.where(chosen & (k0 + kcol <= tpos), 0.0, NEG)
        bias4 = jnp.concatenate(
            [jnp.concatenate([tt_ref[hg, jnp.maximum(qi - (kj * sub + w), 0)] for w in range(sub)], axis=1) + mask_add
             for hg in range(HG)], axis=0)
        s = _dot_t(q4, k_tile) + bias4
        m_old = m_ref[...]
        m_new = jnp.maximum(m_old, jnp.max(s, axis=-1, keepdims=True))
        alpha = jnp.exp(m_old - m_new)
        p = jnp.exp(s - m_new)
        l_ref[...] = alpha * l_ref[...] + jnp.sum(p, axis=-1, keepdims=True)
        acc_ref[...] = alpha * acc_ref[...] + _dot(p.astype(BF16), v_tile)
        m_ref[...] = m_new
        return carry

    lax.fori_loop(0, qi // sub + 1, sel_step, 0)

    nwin = NSA_WINDOW // TQ + 1
    first = jnp.maximum(qi - (nwin - 1), 0)
    w0 = pl.multiple_of(first * TQ, TQ)
    kw_tile = kw_ref[pl.ds(w0, nwin * TQ), :].astype(BF16)
    vw_tile = vw_ref[pl.ds(w0, nwin * TQ), :].astype(BF16)
    delta = (qi * TQ + lax.broadcasted_iota(jnp.int32, (TQ, nwin * TQ), 0)
             - (w0 + lax.broadcasted_iota(jnp.int32, (TQ, nwin * TQ), 1)))
    wmask_add = jnp.where((delta >= 0) & (delta < NSA_WINDOW), 0.0, NEG)
    wbias4 = jnp.concatenate(
        [jnp.concatenate([tt_ref[hg, jnp.maximum(qi - (first + w), 0)] for w in range(nwin)], axis=1) + wmask_add
         for hg in range(HG)], axis=0)
    s = _dot_t(q4, kw_tile) + wbias4
    p = jnp.exp(s - jnp.max(s, axis=-1, keepdims=True))
    o_win = _dot(p.astype(BF16), vw_tile) / jnp.sum(p, axis=-1, keepdims=True)
    o_sel = acc_ref[...] / l_ref[...]

    misc = misc_ref[...]
    gpre = jnp.where(g == 0, misc[:, MISC_NG:MISC_NG + 3 * HG], misc[:, MISC_NG + 3 * HG:MISC_NG + 6 * HG])
    gates = _sigmoid(gpre)
    for hg in range(HG):
        hs = slice(hg * HEAD_DIM, (hg + 1) * HEAD_DIM)
        rs = slice(hg * TQ, (hg + 1) * TQ)
        y = (gates[:, 3 * hg:3 * hg + 1] * oc_ref[:, hs] + gates[:, 3 * hg + 1:3 * hg + 2] * o_sel[rs]
             + gates[:, 3 * hg + 2:3 * hg + 3] * o_win[rs])
        o_ref[:, hs] = y.astype(BF16)


NSA_TQ = 128
NSA_CMP_TQ = 512


def nsa_bias(thr, tab_t, s):
    nh = NSA_KV_GROUPS * NSA_HPG
    tt = bias_table(thr, tab_t, "toeplitz", (nh, s // NSA_TQ, NSA_TQ, NSA_TQ), (NSA_TQ, NSA_TQ))
    bc = bias_table(thr, tab_t, "compressed", (nh, s, s // NSA_CMP_STRIDE), (NSA_TQ, s // NSA_CMP_STRIDE))
    return tt, bc


def nsa_mixer(u3, tt, bc, pe_k, pe_v, ck_w1, ck_w2, cv_w1, cv_w2):
    bsz, s, _ = u3.shape
    G, HG, dh = NSA_KV_GROUPS, NSA_HPG, HEAD_DIM
    TQ = NSA_TQ
    nqt = s // TQ
    assert NSA_KV_GROUPS == 2 and s % NSA_SEL_TK == 0 and s >= NSA_WINDOW + TQ
    nblk = s // NSA_CMP_STRIDE
    n_cmp = (s - NSA_CMP_BLOCK) // NSA_CMP_STRIDE + 1
    n_sb = s // NSA_SEL_BLOCK
    agg = np.zeros((nblk, n_sb), np.float32)
    agg[:n_cmp] = _sel_agg_matrix(n_cmp, n_sb)
    agg = jnp.asarray(agg)

    k_cmp = _nsa_compress(u3, U_NKC, pe_k, ck_w1, ck_w2, "nsa_compress_k")
    v_cmp = _nsa_compress(u3, U_NVC, pe_v, cv_w1, cv_w2, "nsa_compress_v")

    qspec = pl.BlockSpec((None, TQ, HG * dh), lambda b, g, i: (b, i, U_NQ // (HG * dh) + g))
    TC = NSA_CMP_TQ
    o_cmp, sel = pl.pallas_call(
        _nsa_cmp_body,
        grid=(bsz, G, s // TC),
        in_specs=[pl.BlockSpec((None, TC, HG * dh), lambda b, g, i: (b, i, U_NQ // (HG * dh) + g)),
                  pl.BlockSpec((None, None, nblk, dh), lambda b, g, i: (b, g, 0, 0)),
                  pl.BlockSpec((None, None, nblk, dh), lambda b, g, i: (b, g, 0, 0)),
                  pl.BlockSpec((HG, TC, nblk), lambda b, g, i: (g, i, 0)),
                  pl.BlockSpec((nblk, n_sb), lambda b, g, i: (0, 0))],
        out_specs=[pl.BlockSpec((None, TC, HG * dh), lambda b, g, i: (b, i, g)),
                   pl.BlockSpec((None, None, TC, n_sb), lambda b, g, i: (b, g, i, 0))],
        out_shape=[jax.ShapeDtypeStruct((bsz, s, G * HG * dh), F32),
                   jax.ShapeDtypeStruct((bsz, G, s, n_sb), F32)],
        compiler_params=_cparams("parallel", "parallel", "arbitrary"),
        name="nsa_compressed_attn",
    )(u3, k_cmp, v_cmp, bc, agg)

    def kv(off):
        return pl.BlockSpec((None, s, dh), lambda b, g, i: (b, 0, off // dh + g))

    y = pl.pallas_call(
        _nsa_selwin_body,
        grid=(bsz, G, nqt),
        in_specs=[qspec,
                  pl.BlockSpec((None, TQ, HG * dh), lambda b, g, i: (b, i, g)),
                  pl.BlockSpec((None, None, TQ, n_sb), lambda b, g, i: (b, g, i, 0)),
                  pl.BlockSpec((None, TQ, LANES), lambda b, g, i: (b, i, U_MISC // LANES)),
                  kv(U_NKS), kv(U_NVS), kv(U_NKW), kv(U_NVW),
                  pl.BlockSpec((HG, nqt, TQ, TQ), lambda b, g, i: (g, 0, 0, 0))],
        out_specs=pl.BlockSpec((None, TQ, HG * dh), lambda b, g, i: (b, i, g)),
        out_shape=jax.ShapeDtypeStruct((bsz, s, BRANCH_WIDTH), BF16),
        scratch_shapes=[pltpu.VMEM((s, dh), BF16), pltpu.VMEM((s, dh), BF16),
                        pltpu.VMEM((HG * TQ, 1), F32), pltpu.VMEM((HG * TQ, 1), F32),
                        pltpu.VMEM((HG * TQ, dh), F32)],
        compiler_params=_cparams("parallel", "parallel", "arbitrary"),
        name="nsa_selected_window_attn",
    )(u3, o_cmp, sel, u3, u3, u3, u3, u3, tt)
    return y


def _merge_body(x_ref, wg0, wg1, wg2, wg3, y0, y1, y2, y3, wb_ref, o_ref):
    x = x_ref[...]
    acc = None
    for b, (wg, y) in enumerate(((wg0, y0), (wg1, y1), (wg2, y2), (wg3, y3))):
        gate = _sigmoid(_dot(x, wg[...]))
        term = gate * _dot(y[...], wb_ref[b])
        acc = term if acc is None else acc + term
    o_ref[...] = acc.astype(o_ref.dtype)


def merge_branches(hb, w_gates, ys, w_branch):
    n, d = hb.shape
    tm, tn = min(512, n), min(256, d)
    nj = d // tn

    def wg(b):
        return pl.BlockSpec((d, tn), lambda i, j: (0, b * nj + j))

    yspec = pl.BlockSpec((tm, BRANCH_WIDTH), lambda i, j: (i, 0))
    return pl.pallas_call(
        _merge_body,
        grid=(n // tm, nj),
        in_specs=[pl.BlockSpec((tm, d), lambda i, j: (i, 0)), wg(0), wg(1), wg(2), wg(3),
                  yspec, yspec, yspec, yspec,
                  pl.BlockSpec((N_BRANCH, BRANCH_WIDTH, tn), lambda i, j: (0, 0, j))],
        out_specs=pl.BlockSpec((tm, tn), lambda i, j: (i, j)),
        out_shape=jax.ShapeDtypeStruct((n, d), BF16),
        compiler_params=_cparams("parallel", "arbitrary"),
        name="merge_branches",
    )(hb, w_gates, w_gates, w_gates, w_gates, *ys, w_branch)


MOE_TN = 512
EXPERTS_PER_TILE = MOE_TN // EXPERT_FF


def _router_body(x_ref, w_ref, b_ref, comb_ref, combt_ref):
    logits = _dot(x_ref[...], w_ref[...]) + b_ref[...]
    lane = lax.broadcasted_iota(jnp.int32, logits.shape, 1).astype(F32)
    work = logits
    picks, vals = [], []
    for _ in range(TOP_K):
        mx = jnp.max(work, axis=-1, keepdims=True)
        first = jnp.min(jnp.where(work == mx, lane, float(LANES)), axis=-1, keepdims=True)
        pick = lane == first
        picks.append(pick)
        vals.append(mx)
        work = jnp.where(pick, -jnp.inf, work)
    exps = [jnp.exp(v - vals[0]) for v in vals]
    den = exps[0]
    for e in exps[1:]:
        den = den + e
    comb = jnp.zeros_like(logits)
    for pick, e in zip(picks, exps):
        comb = comb + jnp.where(pick, e / den, 0.0)
    comb_ref[...] = comb
    for t in range(N_EXPERTS // EXPERTS_PER_TILE):
        combt_ref[t] = comb[:, t * EXPERTS_PER_TILE:(t + 1) * EXPERTS_PER_TILE]


def moe_router(hb, router_w, router_b):
    n, d = hb.shape
    tm = min(512, n)
    w = jnp.zeros((d, LANES), F32).at[:, :N_EXPERTS].set(router_w).astype(BF16)
    b = jnp.full((1, LANES), NEG, F32).at[0, :N_EXPERTS].set(router_b)
    nt = N_EXPERTS // EXPERTS_PER_TILE
    return pl.pallas_call(
        _router_body,
        grid=(n // tm,),
        in_specs=[pl.BlockSpec((tm, d), lambda i: (i, 0)), pl.BlockSpec((d, LANES), lambda i: (0, 0)),
                  pl.BlockSpec((1, LANES), lambda i: (0, 0))],
        out_specs=[pl.BlockSpec((tm, LANES), lambda i: (i, 0)),
                   pl.BlockSpec((nt, tm, EXPERTS_PER_TILE), lambda i: (0, i, 0))],
        out_shape=[jax.ShapeDtypeStruct((n, LANES), F32), jax.ShapeDtypeStruct((nt, n, EXPERTS_PER_TILE), F32)],
        compiler_params=_cparams("parallel"),
        name="moe_router",
    )(hb, w, b)


def _moe_up_body(x_ref, wg_ref, wu_ref, bg_ref, bu_ref, comb_ref, a_ref):
    x = x_ref[...]
    gate = jnp.minimum(_dot(x, wg_ref[...]) + bg_ref[...], SWIGLU_LIMIT)
    up = jnp.clip(_dot(x, wu_ref[...]) + bu_ref[...], -SWIGLU_LIMIT, SWIGLU_LIMIT)
    act = gate * _sigmoid(SWIGLU_ALPHA * gate) * (up + 1.0)
    for e in range(EXPERTS_PER_TILE):
        es = slice(e * EXPERT_FF, (e + 1) * EXPERT_FF)
        a_ref[:, es] = (act[:, es] * comb_ref[:, e:e + 1]).astype(BF16)


def moe_up(hb, w_gate, w_up, b_gate, b_up, comb_t):
    n, d = hb.shape
    tm = min(512, n)
    width = N_EXPERTS * EXPERT_FF
    wspec = pl.BlockSpec((d, MOE_TN), lambda i, j: (0, j))
    bspec = pl.BlockSpec((1, MOE_TN), lambda i, j: (0, j))
    return pl.pallas_call(
        _moe_up_body,
        grid=(n // tm, width // MOE_TN),
        in_specs=[pl.BlockSpec((tm, d), lambda i, j: (i, 0)), wspec, wspec, bspec, bspec,
                  pl.BlockSpec((None, tm, EXPERTS_PER_TILE), lambda i, j: (j, i, 0))],
        out_specs=pl.BlockSpec((tm, MOE_TN), lambda i, j: (i, j)),
        out_shape=jax.ShapeDtypeStruct((n, width), BF16),
        compiler_params=_cparams("parallel", "arbitrary"),
        name="moe_up",
    )(hb, w_gate, w_up, b_gate, b_up, comb_t)


def _moe_down_body(a_ref, w_ref, comb_ref, b2_ref, o_ref):
    o_ref[...] = _dot(a_ref[...], w_ref[...]) + _dot(comb_ref[...].astype(BF16), b2_ref[...])


def moe_down(a, w2, comb, b2):
    n, k = a.shape
    d = w2.shape[1]
    tm, tn = min(1024, n), min(512, d)
    return pl.pallas_call(
        _moe_down_body,
        grid=(n // tm, d // tn),
        in_specs=[pl.BlockSpec((tm, k), lambda i, j: (i, 0)), pl.BlockSpec((k, tn), lambda i, j: (0, j)),
                  pl.BlockSpec((tm, LANES), lambda i, j: (i, 0)), pl.BlockSpec((LANES, tn), lambda i, j: (0, j))],
        out_specs=pl.BlockSpec((tm, tn), lambda i, j: (i, j)),
        out_shape=jax.ShapeDtypeStruct((n, d), F32),
        compiler_params=_cparams("parallel", "arbitrary"),
        name="moe_down",
    )(a, w2, comb, b2)


def _regroup_w_in(w):
    d = w.shape[0]
    o = _SRC_OFF
    pad = U_WIDTH - U_MISC - (GLA_GATE_RANK + 2 * N_HEADS + 3 * N_HEADS)
    w_u = jnp.concatenate([
        w[:, o[0]:o[4]],
        w[:, o[5]:o[8]],
        w[:, o[10]:o[13]],
        w[:, o[13]:o[20]],
        w[:, o[4]:o[5]], w[:, o[8]:o[10]], w[:, o[20]:o[21]],
        jnp.zeros((d, pad), w.dtype),
    ], axis=1).astype(BF16)
    return w_u, w[:, SRC_GATES:].astype(BF16)


def kernel(x, rel_bias, w_in, gla_gate_w, gla_gate_b, gla_norm_g, mlstm_conv_w, mlstm_conv_b, mlstm_igate_b,
           mlstm_fgate_b, mlstm_norm_g, nsa_pe_k, nsa_pe_v, nsa_ck_w1, nsa_ck_w2, nsa_cv_w1, nsa_cv_w2, w_branch,
           w_out, ln1_g, ln1_b, router_w, router_b, exp_w1, exp_b1, exp_w2, exp_b2, ln2_g, ln2_b):
    out_dtype = x.dtype
    bsz, s, d = x.shape
    n = bsz * s
    h = x.astype(F32).reshape(n, d)
    hb = h.astype(BF16)
    thr = bucket_thresholds(max(s, BIAS_DIST_RANGE))
    tab_t = rel_bias.T
    dil_bias = dilated_bias(thr, tab_t)
    nsa_tt, nsa_bc = nsa_bias(thr, tab_t, s)
    for l in range(w_in.shape[0]):
        w_u, w_gates = _regroup_w_in(w_in[l])
        u = matmul(hb, w_u, F32, 1024, 512, "input_projection")
        u3 = u.reshape(bsz, s, U_WIDTH)
        y_a = gla_mixer(u3, gla_gate_w[l], gla_gate_b[l], gla_norm_g[l]).reshape(n, BRANCH_WIDTH)
        y_b = mlstm_mixer(u3, mlstm_conv_w[l], mlstm_conv_b[l], mlstm_igate_b[l], mlstm_fgate_b[l],
                          mlstm_norm_g[l]).reshape(n, BRANCH_WIDTH)
        y_c = dilated_mixer(u3, dil_bias).reshape(n, BRANCH_WIDTH)
        y_d = nsa_mixer(u3, nsa_tt, nsa_bc, nsa_pe_k[l], nsa_pe_v[l], nsa_ck_w1[l], nsa_ck_w2[l], nsa_cv_w1[l],
                        nsa_cv_w2[l]).reshape(n, BRANCH_WIDTH)
        merged = merge_branches(hb, w_gates, (y_a, y_b, y_c, y_d), w_branch[l].astype(BF16))
        attn = matmul(merged, w_out[l].astype(BF16), F32, 1024, 512, "output_projection")
        h, hb = ln_residual(h, attn, ln1_g[l], ln1_b[l], "layer_norm_1")

        comb, comb_t = moe_router(hb, router_w[l], router_b[l])
        w1 = exp_w1[l]
        w_gate = w1[:, :, :EXPERT_FF].transpose(1, 0, 2).reshape(d, N_EXPERTS * EXPERT_FF).astype(BF16)
        w_up = w1[:, :, EXPERT_FF:].transpose(1, 0, 2).reshape(d, N_EXPERTS * EXPERT_FF).astype(BF16)
        b_gate = exp_b1[l][:, :EXPERT_FF].reshape(1, -1)
        b_up = exp_b1[l][:, EXPERT_FF:].reshape(1, -1)
        act = moe_up(hb, w_gate, w_up, b_gate, b_up, comb_t)
        b2 = jnp.zeros((LANES, d), F32).at[:N_EXPERTS].set(exp_b2[l]).astype(BF16)
        ffn = moe_down(act, exp_w2[l].reshape(N_EXPERTS * EXPERT_FF, d).astype(BF16), comb, b2)
        h, hb = ln_residual(h, ffn, ln2_g[l], ln2_b[l], "layer_norm_2")
    return h.reshape(bsz, s, d).astype(out_dtype)
```

```python
import functools
import math

import numpy as np
import jax
import jax.numpy as jnp
from jax import lax
from jax.experimental import pallas as pl
from jax.experimental.pallas import tpu as pltpu

F32 = jnp.float32
BF16 = jnp.bfloat16
HIGHEST = lax.Precision.HIGHEST

N_LAYERS_FOR_DEEPNORM = 4
HEAD_DIM = 128
BRANCH_WIDTH = 1024
N_BRANCH = 4
N_HEADS = 8
DK = 64
CHUNK = 64
GLA_GATE_RANK = 16
GLA_TAU = 16.0
MLSTM_CONV = 4
MLSTM_IGATE_CAP = 15.0
DIL_PATTERNS = ((128, 1), (512, 4), (2048, 16))
DIL_BLOCK = 128
NSA_KV_GROUPS = 2
NSA_HPG = 4
NSA_CMP_BLOCK = 32
NSA_CMP_STRIDE = 16
NSA_CMP_HIDDEN = 256
NSA_SEL_BLOCK = 64
NSA_N_SEL = 16
NSA_WINDOW = 512
NSA_SEL_TK = 512
REL_BUCKETS = 32
REL_MAX_DIST = 2048
N_EXPERTS = 32
TOP_K = 4
EXPERT_FF = 128
SWIGLU_LIMIT = 7.0
SWIGLU_ALPHA = 1.702
DEEPNORM_ALPHA = (2 * N_LAYERS_FOR_DEEPNORM) ** 0.25
LN_EPS = 1e-5
NEG = -1e30
BIG = 1e9

LANES = 128
VMEM_LIMIT_BYTES = 56 * 1024 * 1024

U_GQ, U_GK, U_GV, U_GR = 0, 512, 1024, 2048
U_MQK, U_MV, U_MO = 3072, 4096, 5120
U_DQ, U_DK, U_DV = 6144, 9216, 10240
U_NQ = 11264
U_NKC, U_NVC, U_NKS, U_NVS, U_NKW, U_NVW = 12288, 12544, 12800, 13056, 13312, 13568
U_MISC = 13824
U_WIDTH = 14336
MISC_GA, MISC_MI, MISC_MF, MISC_NG = 0, 16, 24, 32

_SRC_SIZES = (512, 512, 1024, 1024, 16, 1024, 1024, 1024, 8, 8, 3072, 1024, 1024, 1024,
              256, 256, 256, 256, 256, 256, 24)
_SRC_OFF = np.concatenate([[0], np.cumsum(_SRC_SIZES)]).tolist()
SRC_GATES = _SRC_OFF[-1]


def _cparams(*sem):
    return pltpu.CompilerParams(dimension_semantics=sem, vmem_limit_bytes=VMEM_LIMIT_BYTES)


def _log_sigmoid(x):
    return jnp.minimum(x, 0.0) - jnp.log1p(jnp.exp(-jnp.abs(x)))


def _sigmoid(x):
    return 1.0 / (1.0 + jnp.exp(-x))


def _silu(x):
    return x * _sigmoid(x)


def _dot(a, b):
    return jnp.dot(a, b, preferred_element_type=F32)


def _dot_t(a, b):
    return lax.dot_general(a, b, (((1,), (1,)), ((), ())), preferred_element_type=F32)


def _tdot(a, b):
    return lax.dot_general(a, b, (((0,), (0,)), ((), ())), preferred_element_type=F32)


def _head_norm(o, g_row):
    mu = jnp.mean(o, axis=-1, keepdims=True)
    d = o - mu
    var = jnp.mean(d * d, axis=-1, keepdims=True)
    return d * lax.rsqrt(var + LN_EPS) * g_row


def t5_bucket(dist):
    d = jnp.maximum(dist, 0)
    exact = REL_BUCKETS // 2
    df = jnp.maximum(d, 1).astype(jnp.float32)
    large = exact + (jnp.log(df / exact) / math.log(REL_MAX_DIST / exact) * (REL_BUCKETS - exact)).astype(jnp.int32)
    return jnp.where(d < exact, d, jnp.minimum(large, REL_BUCKETS - 1))


def _mm_body(x_ref, w_ref, o_ref):
    o_ref[...] = _dot(x_ref[...], w_ref[...]).astype(o_ref.dtype)


def matmul(x, w, out_dtype, tm, tn, name):
    m, k = x.shape
    n = w.shape[1]
    tm, tn = min(tm, m), min(tn, n)
    return pl.pallas_call(
        _mm_body,
        grid=(m // tm, n // tn),
        in_specs=[pl.BlockSpec((tm, k), lambda i, j: (i, 0)), pl.BlockSpec((k, tn), lambda i, j: (0, j))],
        out_specs=pl.BlockSpec((tm, tn), lambda i, j: (i, j)),
        out_shape=jax.ShapeDtypeStruct((m, n), out_dtype),
        compiler_params=_cparams("parallel", "arbitrary"),
        name=name,
    )(x, w)


def _ln_body(h_ref, d_ref, g_ref, b_ref, o_ref, ob_ref):
    z = DEEPNORM_ALPHA * h_ref[...] + d_ref[...]
    mu = jnp.mean(z, axis=-1, keepdims=True)
    zc = z - mu
    var = jnp.mean(zc * zc, axis=-1, keepdims=True)
    y = zc * lax.rsqrt(var + LN_EPS) * g_ref[...] + b_ref[...]
    o_ref[...] = y
    ob_ref[...] = y.astype(BF16)


def ln_residual(h, delta, g, b, name):
    n, d = h.shape
    tm = min(256, n)
    row = pl.BlockSpec((tm, d), lambda i: (i, 0))
    vec = pl.BlockSpec((1, d), lambda i: (0, 0))
    return pl.pallas_call(
        _ln_body,
        grid=(n // tm,),
        in_specs=[row, row, vec, vec],
        out_specs=[row, row],
        out_shape=[jax.ShapeDtypeStruct((n, d), F32), jax.ShapeDtypeStruct((n, d), BF16)],
        compiler_params=_cparams("parallel"),
        name=name,
    )(h, delta, g.reshape(1, d), b.reshape(1, d))


def _gla_body(q_ref, k_ref, v_ref, r_ref, misc_ref, gw_ref, gb_ref, ng_ref, o_ref, st_ref):
    L = CHUNK

    @pl.when(pl.program_id(1) == 0)
    def _():
        st_ref[...] = jnp.zeros_like(st_ref)

    pre = _dot(misc_ref[...].astype(BF16), gw_ref[...]) + gb_ref[...]
    log_a = _log_sigmoid(pre) / GLA_TAU
    row = lax.broadcasted_iota(jnp.int32, (L, L), 0)
    col = lax.broadcasted_iota(jnp.int32, (L, L), 1)
    causal = col <= row
    tri = jnp.where(causal, 1.0, 0.0).astype(F32)
    b = jnp.dot(tri, log_a, precision=HIGHEST, preferred_element_type=F32)
    b_last = b[L - 1:L, :]
    q_dec = (q_ref[...] * (DK ** -0.5) * jnp.exp(b)).astype(BF16)
    k_dec = (k_ref[...] * jnp.exp(-b)).astype(BF16)
    k_end = (k_ref[...] * jnp.exp(b_last - b)).astype(BF16)
    decay = jnp.exp(b_last)

    def heads(x, width):
        return jnp.stack([x[:, h * width:(h + 1) * width] for h in range(N_HEADS)], axis=0)

    q3, k3, ke3 = heads(q_dec, DK), heads(k_dec, DK), heads(k_end, DK)
    v3 = heads(v_ref[...].astype(BF16), HEAD_DIM)
    dec3 = heads(decay, DK)
    state_t = st_ref[...]
    att = jnp.where(causal[None], jnp.einsum('hqd,hkd->hqk', q3, k3, preferred_element_type=F32), 0.0)
    o3 = (jnp.einsum('hqk,hkv->hqv', att.astype(BF16), v3, preferred_element_type=F32)
          + jnp.einsum('hqd,hvd->hqv', q3, state_t.astype(BF16), preferred_element_type=F32))
    st_ref[...] = dec3 * state_t + jnp.einsum('hkv,hkd->hvd', v3, ke3, preferred_element_type=F32)
    on3 = _head_norm(o3, heads(ng_ref[...], HEAD_DIM))
    out3 = (_silu(heads(r_ref[...], HEAD_DIM)) * on3).astype(BF16)
    for h in range(N_HEADS):
        o_ref[:, h * HEAD_DIM:(h + 1) * HEAD_DIM] = out3[h]


def gla_mixer(u3, gate_w, gate_b, norm_g):
    bsz, s, _ = u3.shape
    L = CHUNK
    gw = jnp.zeros((LANES, N_HEADS * DK), F32).at[MISC_GA:MISC_GA + GLA_GATE_RANK].set(gate_w).astype(BF16)

    def ublock(width, off):
        return pl.BlockSpec((None, L, width), lambda b, t: (b, t, off // width))

    def const(shape):
        return pl.BlockSpec(shape, lambda b, t: (0,) * len(shape))

    return pl.pallas_call(
        _gla_body,
        grid=(bsz, s // L),
        in_specs=[ublock(512, U_GQ), ublock(512, U_GK), ublock(1024, U_GV), ublock(1024, U_GR),
                  ublock(LANES, U_MISC), const((LANES, 512)), const((1, 512)), const((1, 1024))],
        out_specs=pl.BlockSpec((None, L, 1024), lambda b, t: (b, t, 0)),
        out_shape=jax.ShapeDtypeStruct((bsz, s, BRANCH_WIDTH), BF16),
        scratch_shapes=[pltpu.VMEM((N_HEADS, HEAD_DIM, DK), F32)],
        compiler_params=_cparams("parallel", "arbitrary"),
        name="gla_mixer",
    )(u3, u3, u3, u3, u3, gw, gate_b.reshape(1, -1), norm_g.reshape(1, -1))


def _mlstm_body(qk_ref, v_ref, op_ref, misc_ref, cw_ref, cb_ref, ib_ref, fb_ref, ng_ref, o_ref,
                ext_ref, c_ref, n_ref, m_ref):
    L = CHUNK
    C2 = 2 * N_HEADS * DK

    @pl.when(pl.program_id(1) == 0)
    def _():
        ext_ref[0:8, :] = jnp.zeros((8, C2), F32)
        c_ref[...] = jnp.zeros_like(c_ref)
        n_ref[...] = jnp.zeros_like(n_ref)
        m_ref[...] = jnp.zeros_like(m_ref)

    x = qk_ref[...]
    ext_ref[8:8 + L, :] = x
    y = (cb_ref[...] + cw_ref[3:4, :] * x + cw_ref[2:3, :] * ext_ref[pl.ds(7, L), :]
         + cw_ref[1:2, :] * ext_ref[pl.ds(6, L), :] + cw_ref[0:1, :] * ext_ref[pl.ds(5, L), :])
    ext_ref[0:8, :] = x[L - 8:L, :]
    qk = _silu(y)
    q_all = qk[:, :N_HEADS * DK].astype(BF16)
    k_all = qk[:, N_HEADS * DK:] * (DK ** -0.5)

    misc = misc_ref[...]
    i_g = MLSTM_IGATE_CAP * jnp.tanh((misc + ib_ref[...]) / MLSTM_IGATE_CAP)
    log_f = _log_sigmoid(misc + fb_ref[...])
    row = lax.broadcasted_iota(jnp.int32, (L, L), 0)
    col = lax.broadcasted_iota(jnp.int32, (L, L), 1)
    causal = col <= row
    tri = jnp.where(causal, 1.0, 0.0).astype(F32)
    b = jnp.dot(tri, log_f, precision=HIGHEST, preferred_element_type=F32)
    it = pltpu.roll(i_g, MISC_MF - MISC_MI, 1)
    b_last = b[L - 1:L, :]
    m_s = m_ref[0:1, :]
    a_end = b_last - b + it
    m_new = jnp.maximum(b_last + m_s, jnp.max(a_end, axis=0, keepdims=True))
    w = jnp.exp(a_end - m_new)
    sc = jnp.exp(b_last + m_s - m_new)
    inter_log = b + m_s
    c = it - b
    rows_i = lax.broadcasted_iota(jnp.int32, c.shape, 0)
    cmax = c
    for sh in (1, 2, 4, 8, 16, 32):
        cmax = jnp.maximum(cmax, jnp.where(rows_i >= sh, pltpu.roll(cmax, sh, 0), -jnp.inf))
    m_i = jnp.maximum(inter_log, b + cmax)
    w_inter = jnp.exp(inter_log - m_i)
    e_neg = jnp.exp(-m_i)
    m_ref[0:1, :] = m_new

    stats = jnp.concatenate([b - m_i, w, w_inter, e_neg, jnp.broadcast_to(sc, (8, LANES))], axis=0)
    lane = lax.broadcasted_iota(jnp.int32, stats.shape, 1)
    stats = jnp.where((lane >= MISC_MF) & (lane < MISC_MF + N_HEADS), stats, 0.0)
    sel_r = lax.broadcasted_iota(jnp.int32, (LANES, N_HEADS * LANES), 0)
    sel_c = lax.broadcasted_iota(jnp.int32, (LANES, N_HEADS * LANES), 1) // LANES
    onehot = jnp.where(sel_r == sel_c + MISC_MF, 1.0, 0.0).astype(F32)
    spread = jnp.dot(stats, onehot, precision=HIGHEST, preferred_element_type=F32)

    def heads(x, width):
        return jnp.stack([x[:, h * width:(h + 1) * width] for h in range(N_HEADS)], axis=0)

    dcol3 = heads(spread[0:L], LANES)[:, :, 0:L]
    w3 = heads(spread[L:2 * L], LANES)[:, :, 0:DK]
    wi3 = heads(spread[2 * L:3 * L], LANES)
    en3 = heads(spread[3 * L:4 * L], LANES)
    sc3 = heads(spread[4 * L:4 * L + 1], LANES)
    c_t = c.T
    crow3 = jnp.stack([c_t[MISC_MF + h:MISC_MF + h + 1, :] for h in range(N_HEADS)], axis=0)

    q3 = heads(q_all, DK)
    k3 = heads(k_all, DK)
    v3 = heads(v_ref[...].astype(BF16), HEAD_DIM)
    c_s = c_ref[...]
    n_s = n_ref[...]
    decay3 = jnp.exp(jnp.where(causal[None], dcol3 + crow3, -jnp.inf))
    qk_s = jnp.einsum('hqd,hkd->hqk', q3, k3.astype(BF16), preferred_element_type=F32) * decay3
    num = (wi3 * jnp.einsum('hqd,hdv->hqv', q3, c_s.astype(BF16), preferred_element_type=F32)
           + jnp.einsum('hqk,hkv->hqv', qk_s.astype(BF16), v3, preferred_element_type=F32))
    qn = jnp.sum(q3.astype(F32) * n_s, axis=-1, keepdims=True)
    den = wi3[:, :, 0:1] * qn + jnp.sum(qk_s, axis=-1, keepdims=True)
    hout = num / jnp.maximum(jnp.abs(den), en3[:, :, 0:1])
    wk3 = w3 * k3
    c_ref[...] = sc3 * c_s + jnp.einsum('hkd,hkv->hdv', wk3.astype(BF16), v3, preferred_element_type=F32)
    n_ref[...] = sc3[:, :, 0:DK] * n_s + jnp.sum(wk3, axis=1, keepdims=True)
    hn = _head_norm(hout, heads(ng_ref[...], HEAD_DIM))
    out3 = (_sigmoid(heads(op_ref[...], HEAD_DIM)) * hn).astype(BF16)
    for h in range(N_HEADS):
        o_ref[:, h * HEAD_DIM:(h + 1) * HEAD_DIM] = out3[h]


def mlstm_mixer(u3, conv_w, conv_b, igate_b, fgate_b, norm_g):
    bsz, s, _ = u3.shape
    L = CHUNK
    ib = jnp.zeros((1, LANES), F32).at[0, MISC_MI:MISC_MI + N_HEADS].set(igate_b)
    fb = jnp.zeros((1, LANES), F32).at[0, MISC_MF:MISC_MF + N_HEADS].set(fgate_b)

    def ublock(width, off):
        return pl.BlockSpec((None, L, width), lambda b, t: (b, t, off // width))

    def const(shape):
        return pl.BlockSpec(shape, lambda b, t: (0,) * len(shape))

    return pl.pallas_call(
        _mlstm_body,
        grid=(bsz, s // L),
        in_specs=[ublock(1024, U_MQK), ublock(1024, U_MV), ublock(1024, U_MO), ublock(LANES, U_MISC),
                  const((MLSTM_CONV, 1024)), const((1, 1024)), const((1, LANES)), const((1, LANES)),
                  const((1, 1024))],
        out_specs=pl.BlockSpec((None, L, 1024), lambda b, t: (b, t, 0)),
        out_shape=jax.ShapeDtypeStruct((bsz, s, BRANCH_WIDTH), BF16),
        scratch_shapes=[pltpu.VMEM((L + 8, 1024), F32), pltpu.VMEM((N_HEADS, DK, HEAD_DIM), F32),
                        pltpu.VMEM((N_HEADS, 1, DK), F32), pltpu.VMEM((8, LANES), F32)],
        compiler_params=_cparams("parallel", "arbitrary"),
        name="mlstm_mixer",
    )(u3, u3, u3, u3, conv_w, conv_b.reshape(1, -1), ib, fb, norm_g.reshape(1, -1))


BIAS_DIST_RANGE = 4096


def bucket_thresholds(max_dist):
    lut = t5_bucket(jnp.arange(max_dist))
    return jnp.sum(lut[None, :] < jnp.arange(REL_BUCKETS)[:, None], axis=1).astype(jnp.int32)


def _bias_table_body(thr_ref, tab_ref, o_ref, *, kind):
    a = pl.program_id(0)
    c = pl.program_id(1)
    shape = o_ref.shape
    i = lax.broadcasted_iota(jnp.int32, shape, 0)
    j = lax.broadcasted_iota(jnp.int32, shape, 1)
    if kind == "dilated":
        dil = jnp.where(a == 0, DIL_PATTERNS[0][1], jnp.where(a == 1, DIL_PATTERNS[1][1], DIL_PATTERNS[2][1]))
        dist = (i + DIL_BLOCK - j) * dil
        head = a * N_HEADS + c
    elif kind in ("causal", "window"):
        dist = c * shape[0] + i - j
        head = 3 * N_HEADS + a
    else:
        dist = c * shape[0] + i - (j * NSA_CMP_STRIDE + NSA_CMP_BLOCK - 1)
        head = 3 * N_HEADS + a
    dist_c = jnp.maximum(dist, 0)
    acc = jnp.full(shape, tab_ref[head, 0], F32)
    for k in range(1, REL_BUCKETS):
        acc = jnp.where(dist_c >= thr_ref[k], tab_ref[head, k], acc)
    if kind in ("causal", "window"):
        limit = NSA_WINDOW if kind == "window" else 2 ** 30
        dist_v = jnp.where(c < pl.num_programs(1) - 1, dist, -1)
        acc = jnp.where((dist_v >= 0) & (dist_v < limit), acc, NEG)
    o_ref[...] = acc


def bias_table(thr, tab_t, kind, out_dims, block):
    smem = pl.BlockSpec(memory_space=pltpu.SMEM)
    return pl.pallas_call(
        functools.partial(_bias_table_body, kind=kind),
        grid=out_dims[:2] if kind != "compressed" else (out_dims[0], out_dims[1] // block[0]),
        in_specs=[smem, smem],
        out_specs=(pl.BlockSpec((None, None) + block, lambda a, c: (a, c, 0, 0)) if kind != "compressed"
                   else pl.BlockSpec((None,) + block, lambda a, c: (a, c, 0))),
        out_shape=jax.ShapeDtypeStruct(out_dims, F32),
        compiler_params=_cparams("parallel", "parallel"),
        name=f"bias_table_{kind}",
    )(thr, tab_t)


DIL_SPAN = 2048


def _dil_body(q0_ref, q1_ref, q2_ref, kp_ref, kc_ref, vp_ref, vc_ref, bias_ref, o_ref, m_scr, l_scr, acc_scr):
    P = DIL_BLOCK
    NB = DIL_SPAN // P
    q_refs = (q0_ref, q1_ref, q2_ref)
    row = lax.broadcasted_iota(jnp.int32, (P, P), 0)
    col = lax.broadcasted_iota(jnp.int32, (P, P), 1)
    mask_cur = jnp.where(col <= row, 0.0, NEG)
    mask_prev = jnp.where(col >= row, 0.0, NEG)
    no_prev = jnp.where(pl.program_id(1) > 0, 0, P)
    mask_prev_first = jnp.where((col - row) >= no_prev, 0.0, NEG)

    def rows(r, n, dil):
        return pl.ds(r, n, stride=dil) if dil > 1 else pl.ds(r, n)

    def stacked(ref, dil):
        per = DIL_SPAN // dil
        return jnp.concatenate([ref[rows(r, per, dil), :].reshape(per // P, P, HEAD_DIM) for r in range(dil)], axis=0)

    def stacked_prev(cur3, prev_ref, dil):
        nblk = NB // dil
        parts = []
        for r in range(dil):
            parts.append(prev_ref[rows(r + DIL_SPAN - dil * P, P, dil), :].reshape(1, P, HEAD_DIM))
            if nblk > 1:
                parts.append(cur3[r * nblk:(r + 1) * nblk - 1])
        return jnp.concatenate(parts, axis=0)

    for g, (_, dil) in enumerate(DIL_PATTERNS):
        nblk = NB // dil
        per = DIL_SPAN // dil
        q3 = (stacked(q_refs[g], dil) * (HEAD_DIM ** -0.5)).astype(BF16)
        kc3 = stacked(kc_ref, dil)
        vc3 = stacked(vc_ref, dil)
        kp3 = stacked_prev(kc3, kp_ref, dil).astype(BF16)
        vp3 = stacked_prev(vc3, vp_ref, dil).astype(BF16)
        kc3 = kc3.astype(BF16)
        vc3 = vc3.astype(BF16)
        bias_p = bias_ref[g, :, 0:P]
        bias_prev3 = jnp.concatenate(
            [(bias_p + (mask_prev_first if b % nblk == 0 else mask_prev)).reshape(1, P, P) for b in range(NB)], axis=0)
        s_p = jnp.einsum('nqd,nkd->nqk', q3, kp3, preferred_element_type=F32) + bias_prev3
        s_c = (jnp.einsum('nqd,nkd->nqk', q3, kc3, preferred_element_type=F32)
               + (bias_ref[g, :, P:2 * P] + mask_cur)[None])
        m = jnp.maximum(jnp.max(s_p, axis=-1, keepdims=True), jnp.max(s_c, axis=-1, keepdims=True))
        p_p = jnp.exp(s_p - m)
        p_c = jnp.exp(s_c - m)
        l = jnp.sum(p_p, axis=-1, keepdims=True) + jnp.sum(p_c, axis=-1, keepdims=True)
        acc = (jnp.einsum('nqk,nkd->nqd', p_p.astype(BF16), vp3, preferred_element_type=F32)
               + jnp.einsum('nqk,nkd->nqd', p_c.astype(BF16), vc3, preferred_element_type=F32))
        for r in range(dil):
            sl = rows(g * DIL_SPAN + r, per, dil)
            bs = slice(r * nblk, (r + 1) * nblk)
            m_scr[sl, :] = jnp.broadcast_to(m[bs].reshape(per, 1), (per, LANES))
            l_scr[sl, :] = jnp.broadcast_to(l[bs].reshape(per, 1), (per, LANES))
            acc_scr[sl, :] = acc[bs].reshape(per, HEAD_DIM)

    ms = [m_scr[g * DIL_SPAN:(g + 1) * DIL_SPAN, :] for g in range(len(DIL_PATTERNS))]
    m_all = jnp.maximum(jnp.maximum(ms[0], ms[1]), ms[2])
    num = jnp.zeros((DIL_SPAN, HEAD_DIM), F32)
    den = jnp.zeros((DIL_SPAN, LANES), F32)
    for g in range(len(DIL_PATTERNS)):
        w = jnp.exp(ms[g] - m_all)
        num = num + w * acc_scr[g * DIL_SPAN:(g + 1) * DIL_SPAN, :]
        den = den + w * l_scr[g * DIL_SPAN:(g + 1) * DIL_SPAN, :]
    o_ref[...] = (num / den).astype(BF16)


def dilated_bias(thr, tab_t):
    P = DIL_BLOCK
    return bias_table(thr, tab_t, "dilated", (len(DIL_PATTERNS), N_HEADS, P, 2 * P), (P, 2 * P))


def dilated_mixer(u3, bias):
    bsz, s, _ = u3.shape
    P = DIL_BLOCK
    assert s % DIL_SPAN == 0 and all(w // d == P and DIL_SPAN % (d * P) == 0 for w, d in DIL_PATTERNS)

    def ub(off, prev):
        def idx(b, t, h):
            return (b, jnp.maximum(t - 1, 0) if prev else t, off // HEAD_DIM + h)
        return pl.BlockSpec((None, DIL_SPAN, HEAD_DIM), idx)

    return pl.pallas_call(
        _dil_body,
        grid=(bsz, s // DIL_SPAN, N_HEADS),
        in_specs=[ub(U_DQ, False), ub(U_DQ + 1024, False), ub(U_DQ + 2048, False),
                  ub(U_DK, True), ub(U_DK, False), ub(U_DV, True), ub(U_DV, False),
                  pl.BlockSpec((len(DIL_PATTERNS), None, P, 2 * P), lambda b, t, h: (0, h, 0, 0))],
        out_specs=pl.BlockSpec((None, DIL_SPAN, HEAD_DIM), lambda b, t, h: (b, t, h)),
        out_shape=jax.ShapeDtypeStruct((bsz, s, BRANCH_WIDTH), BF16),
        scratch_shapes=[pltpu.VMEM((len(DIL_PATTERNS) * DIL_SPAN, LANES), F32),
                        pltpu.VMEM((len(DIL_PATTERNS) * DIL_SPAN, LANES), F32),
                        pltpu.VMEM((len(DIL_PATTERNS) * DIL_SPAN, HEAD_DIM), F32)],
        compiler_params=_cparams("parallel", "parallel", "arbitrary"),
        name="dilated_attn",
    )(u3, u3, u3, u3, u3, u3, u3, bias)


def _gelu_tanh(x):
    return 0.5 * x * (1.0 + jnp.tanh(math.sqrt(2.0 / math.pi) * (x + 0.044715 * (x * x * x))))


def _nsa_compress_body(t_ref, pe_ref, w1_ref, w2_ref, o_ref):
    nblk = t_ref.shape[0] // NSA_CMP_STRIDE
    half = NSA_CMP_STRIDE
    t1 = jnp.zeros((nblk, NSA_CMP_HIDDEN), F32)
    t2 = jnp.zeros((nblk, NSA_CMP_HIDDEN), F32)
    for p in range(half):
        xp = t_ref[pl.ds(p, nblk, stride=NSA_CMP_STRIDE), :]
        t1 = t1 + _dot((xp + pe_ref[p:p + 1, :]).astype(BF16), w1_ref[p * HEAD_DIM:(p + 1) * HEAD_DIM, :])
        t2 = t2 + _dot((xp + pe_ref[half + p:half + p + 1, :]).astype(BF16),
                       w1_ref[(half + p) * HEAD_DIM:(half + p + 1) * HEAD_DIM, :])
    hidden = t1 + pltpu.roll(t2, nblk - 1, 0)
    o_ref[...] = _dot(_gelu_tanh(hidden).astype(BF16), w2_ref[...])


def _nsa_compress(u3, off, pe, w1, w2, name):
    bsz, s, _ = u3.shape
    nblk = s // NSA_CMP_STRIDE
    G = NSA_KV_GROUPS
    return pl.pallas_call(
        _nsa_compress_body,
        grid=(bsz, G),
        in_specs=[pl.BlockSpec((None, s, HEAD_DIM), lambda b, g: (b, 0, off // HEAD_DIM + g)),
                  pl.BlockSpec((NSA_CMP_BLOCK, HEAD_DIM), lambda b, g: (0, 0)),
                  pl.BlockSpec((NSA_CMP_BLOCK * HEAD_DIM, NSA_CMP_HIDDEN), lambda b, g: (0, 0)),
                  pl.BlockSpec((NSA_CMP_HIDDEN, HEAD_DIM), lambda b, g: (0, 0))],
        out_specs=pl.BlockSpec((None, None, nblk, HEAD_DIM), lambda b, g: (b, g, 0, 0)),
        out_shape=jax.ShapeDtypeStruct((bsz, G, nblk, HEAD_DIM), F32),
        compiler_params=_cparams("parallel", "parallel"),
        name=name,
    )(u3, pe, w1.astype(BF16), w2.astype(BF16))


def _nsa_cmp_body(q_ref, kc_ref, vc_ref, bc_ref, agg_ref, oc_ref, sel_ref):
    TQ = q_ref.shape[0]
    ncb = kc_ref.shape[0]
    nsb = agg_ref.shape[1]
    t0 = pl.program_id(2) * TQ
    tpos = t0 + lax.broadcasted_iota(jnp.int32, (TQ, ncb), 0)
    ends = lax.broadcasted_iota(jnp.int32, (TQ, ncb), 1) * NSA_CMP_STRIDE + (NSA_CMP_BLOCK - 1)
    mask = ends <= tpos
    maskf = jnp.where(mask, 1.0, 0.0).astype(F32)
    kcb = kc_ref[...].astype(BF16)
    vcb = vc_ref[...].astype(BF16)
    psum = jnp.zeros((TQ, ncb), F32)
    for hg in range(NSA_HPG):
        hs = slice(hg * HEAD_DIM, (hg + 1) * HEAD_DIM)
        q = (q_ref[:, hs] * (HEAD_DIM ** -0.5)).astype(BF16)
        s = jnp.where(mask, _dot_t(q, kcb) + bc_ref[hg], NEG)
        p = jnp.exp(s - jnp.max(s, axis=-1, keepdims=True)) * maskf
        l = jnp.sum(p, axis=-1, keepdims=True)
        p = p / jnp.where(l > 0, l, 1.0)
        oc_ref[:, hs] = _dot(p.astype(BF16), vcb)
        psum = psum + p
    imp = jnp.dot(psum, agg_ref[...], precision=HIGHEST, preferred_element_type=F32)
    jblk = lax.broadcasted_iota(jnp.int32, (TQ, nsb), 1)
    tq = t0 + lax.broadcasted_iota(jnp.int32, (TQ, nsb), 0)
    cur = tq // NSA_SEL_BLOCK
    forced = (jblk == 0) | (jblk == cur) | (jblk == cur - 1)
    score = jnp.where(forced, BIG, jnp.where(jblk * NSA_SEL_BLOCK <= tq, imp, -BIG))
    jf = jblk.astype(F32)
    sel = jnp.zeros((TQ, nsb), F32)
    for _ in range(min(NSA_N_SEL, nsb)):
        mx = jnp.max(score, axis=-1, keepdims=True)
        first = jnp.min(jnp.where(score == mx, jf, float(nsb)), axis=-1, keepdims=True)
        pick = jf == first
        sel = jnp.where(pick, 1.0, sel)
        score = jnp.where(pick, -jnp.inf, score)
    sel_ref[...] = sel


def _sel_agg_matrix(n_cmp, n_sb):
    r = NSA_SEL_BLOCK // NSA_CMP_STRIDE
    c = NSA_CMP_BLOCK // NSA_CMP_STRIDE
    jj, aa, bb = np.meshgrid(np.arange(n_sb), np.arange(r), np.arange(c), indexing='ij')
    ii = r * jj + aa + bb - 1
    ok = (ii >= 0) & (ii < n_cmp)
    mat = np.zeros((n_cmp, n_sb), np.float32)
    np.add.at(mat, (ii[ok], jj[ok]), 1.0)
    return mat


def _nsa_selwin_body(q_ref, oc_ref, sel_ref, misc_ref, ks_ref, vs_ref, kw_ref, vw_ref, tt_ref, tw_ref, o_ref,
                     ksb_ref, vsb_ref, m_ref, l_ref, acc_ref):
    TQ = q_ref.shape[0]
    TK = NSA_SEL_TK
    HG = NSA_HPG
    nsb = sel_ref.shape[1]
    sub = TK // TQ
    g = pl.program_id(1)
    qi = pl.program_id(2)

    @pl.when(qi == 0)
    def _():
        ksb_ref[:, 0:HEAD_DIM] = ks_ref[...].astype(BF16)
        blk = lax.broadcasted_iota(jnp.int32, (ksb_ref.shape[0], nsb), 0) // NSA_SEL_BLOCK
        ksb_ref[:, HEAD_DIM:HEAD_DIM + nsb] = jnp.where(
            blk == lax.broadcasted_iota(jnp.int32, (ksb_ref.shape[0], nsb), 1), 1.0, 0.0).astype(BF16)
        vsb_ref[...] = vs_ref[...].astype(BF16)

    q4 = jnp.concatenate([(q_ref[:, hg * HEAD_DIM:(hg + 1) * HEAD_DIM] * (HEAD_DIM ** -0.5)).astype(BF16)
                          for hg in range(HG)], axis=0)
    unselected = jnp.where(sel_ref[...] > 0.5, 0.0, NEG).astype(BF16)
    q4_aug = jnp.concatenate([q4, jnp.concatenate([unselected] * HG, axis=0)], axis=1)
    masked_tile = tt_ref.shape[1] - 1

    m_ref[...] = jnp.full_like(m_ref, 0.1 * NEG)
    l_ref[...] = jnp.zeros_like(l_ref)
    acc_ref[...] = jnp.zeros_like(acc_ref)

    def sel_step(kj, carry):
        k0 = pl.multiple_of(kj * TK, TK)
        k_tile = ksb_ref[pl.ds(k0, TK), :]
        v_tile = vsb_ref[pl.ds(k0, TK), :]

        def tile_of(w):
            d = qi - (kj * sub + w)
            return jnp.where(d >= 0, d, masked_tile)

        bias4 = jnp.concatenate([jnp.concatenate([tt_ref[hg, tile_of(w)] for w in range(sub)], axis=1)
                                 for hg in range(HG)], axis=0)
        s = _dot_t(q4_aug, k_tile) + bias4
        m_old = m_ref[...]
        m_new = jnp.maximum(m_old, jnp.max(s, axis=-1, keepdims=True))
        alpha = jnp.exp(m_old - m_new)
        p = jnp.exp(s - m_new)
        l_ref[...] = alpha * l_ref[...] + jnp.sum(p, axis=-1, keepdims=True)
        acc_ref[...] = alpha * acc_ref[...] + _dot(p.astype(BF16), v_tile)
        m_ref[...] = m_new
        return carry

    lax.fori_loop(0, qi // sub + 1, sel_step, 0)

    nwin = NSA_WINDOW // TQ + 1
    first = jnp.maximum(qi - (nwin - 1), 0)
    w0 = pl.multiple_of(first * TQ, TQ)
    kw_tile = kw_ref[pl.ds(w0, nwin * TQ), :].astype(BF16)
    vw_tile = vw_ref[pl.ds(w0, nwin * TQ), :].astype(BF16)

    def wtile_of(w):
        d = qi - (first + w)
        return jnp.where(d >= 0, d, tw_ref.shape[1] - 1)

    wbias4 = jnp.concatenate([jnp.concatenate([tw_ref[hg, wtile_of(w)] for w in range(nwin)], axis=1)
                              for hg in range(HG)], axis=0)
    s = _dot_t(q4, kw_tile) + wbias4
    p = jnp.exp(s - jnp.max(s, axis=-1, keepdims=True))
    o_win = _dot(p.astype(BF16), vw_tile) / jnp.sum(p, axis=-1, keepdims=True)
    o_sel = acc_ref[...] / l_ref[...]

    misc = misc_ref[...]
    gpre = jnp.where(g == 0, misc[:, MISC_NG:MISC_NG + 3 * HG], misc[:, MISC_NG + 3 * HG:MISC_NG + 6 * HG])
    gates = _sigmoid(gpre)
    for hg in range(HG):
        hs = slice(hg * HEAD_DIM, (hg + 1) * HEAD_DIM)
        rs = slice(hg * TQ, (hg + 1) * TQ)
        y = (gates[:, 3 * hg:3 * hg + 1] * oc_ref[:, hs] + gates[:, 3 * hg + 1:3 * hg + 2] * o_sel[rs]
             + gates[:, 3 * hg + 2:3 * hg + 3] * o_win[rs])
        o_ref[:, hs] = y.astype(BF16)


NSA_TQ = 128
NSA_CMP_TQ = 512


def nsa_bias(thr, tab_t, s):
    nh = NSA_KV_GROUPS * NSA_HPG
    tile = (NSA_TQ, NSA_TQ)
    tt = bias_table(thr, tab_t, "causal", (nh, s // NSA_TQ + 1) + tile, tile)
    tw = bias_table(thr, tab_t, "window", (nh, NSA_WINDOW // NSA_TQ + 2) + tile, tile)
    bc = bias_table(thr, tab_t, "compressed", (nh, s, s // NSA_CMP_STRIDE), (NSA_TQ, s // NSA_CMP_STRIDE))
    return tt, tw, bc


def nsa_mixer(u3, tt, tw, bc, pe_k, pe_v, ck_w1, ck_w2, cv_w1, cv_w2):
    bsz, s, _ = u3.shape
    G, HG, dh = NSA_KV_GROUPS, NSA_HPG, HEAD_DIM
    TQ = NSA_TQ
    nqt = s // TQ
    assert NSA_KV_GROUPS == 2 and s % NSA_SEL_TK == 0 and s >= NSA_WINDOW + TQ
    nblk = s // NSA_CMP_STRIDE
    n_cmp = (s - NSA_CMP_BLOCK) // NSA_CMP_STRIDE + 1
    n_sb = s // NSA_SEL_BLOCK
    agg = np.zeros((nblk, n_sb), np.float32)
    agg[:n_cmp] = _sel_agg_matrix(n_cmp, n_sb)
    agg = jnp.asarray(agg)

    k_cmp = _nsa_compress(u3, U_NKC, pe_k, ck_w1, ck_w2, "nsa_compress_k")
    v_cmp = _nsa_compress(u3, U_NVC, pe_v, cv_w1, cv_w2, "nsa_compress_v")

    qspec = pl.BlockSpec((None, TQ, HG * dh), lambda b, g, i: (b, i, U_NQ // (HG * dh) + g))
    TC = NSA_CMP_TQ
    o_cmp, sel = pl.pallas_call(
        _nsa_cmp_body,
        grid=(bsz, G, s // TC),
        in_specs=[pl.BlockSpec((None, TC, HG * dh), lambda b, g, i: (b, i, U_NQ // (HG * dh) + g)),
                  pl.BlockSpec((None, None, nblk, dh), lambda b, g, i: (b, g, 0, 0)),
                  pl.BlockSpec((None, None, nblk, dh), lambda b, g, i: (b, g, 0, 0)),
                  pl.BlockSpec((HG, TC, nblk), lambda b, g, i: (g, i, 0)),
                  pl.BlockSpec((nblk, n_sb), lambda b, g, i: (0, 0))],
        out_specs=[pl.BlockSpec((None, TC, HG * dh), lambda b, g, i: (b, i, g)),
                   pl.BlockSpec((None, None, TC, n_sb), lambda b, g, i: (b, g, i, 0))],
        out_shape=[jax.ShapeDtypeStruct((bsz, s, G * HG * dh), F32),
                   jax.ShapeDtypeStruct((bsz, G, s, n_sb), F32)],
        compiler_params=_cparams("parallel", "parallel", "arbitrary"),
        name="nsa_compressed_attn",
    )(u3, k_cmp, v_cmp, bc, agg)

    def kv(off):
        return pl.BlockSpec((None, s, dh), lambda b, g, i: (b, 0, off // dh + g))

    y = pl.pallas_call(
        _nsa_selwin_body,
        grid=(bsz, G, nqt),
        in_specs=[qspec,
                  pl.BlockSpec((None, TQ, HG * dh), lambda b, g, i: (b, i, g)),
                  pl.BlockSpec((None, None, TQ, n_sb), lambda b, g, i: (b, g, i, 0)),
                  pl.BlockSpec((None, TQ, LANES), lambda b, g, i: (b, i, U_MISC // LANES)),
                  kv(U_NKS), kv(U_NVS), kv(U_NKW), kv(U_NVW),
                  pl.BlockSpec((HG,) + tt.shape[1:], lambda b, g, i: (g, 0, 0, 0)),
                  pl.BlockSpec((HG,) + tw.shape[1:], lambda b, g, i: (g, 0, 0, 0))],
        out_specs=pl.BlockSpec((None, TQ, HG * dh), lambda b, g, i: (b, i, g)),
        out_shape=jax.ShapeDtypeStruct((bsz, s, BRANCH_WIDTH), BF16),
        scratch_shapes=[pltpu.VMEM((s, dh + n_sb), BF16), pltpu.VMEM((s, dh), BF16),
                        pltpu.VMEM((HG * TQ, 1), F32), pltpu.VMEM((HG * TQ, 1), F32),
                        pltpu.VMEM((HG * TQ, dh), F32)],
        compiler_params=_cparams("parallel", "parallel", "arbitrary"),
        name="nsa_selected_window_attn",
    )(u3, o_cmp, sel, u3, u3, u3, u3, u3, tt, tw)
    return y


def _merge_body(x_ref, wg0, wg1, wg2, wg3, y0, y1, y2, y3, wb_ref, o_ref):
    x = x_ref[...]
    acc = None
    for b, (wg, y) in enumerate(((wg0, y0), (wg1, y1), (wg2, y2), (wg3, y3))):
        gate = _sigmoid(_dot(x, wg[...]))
        term = gate * _dot(y[...], wb_ref[b])
        acc = term if acc is None else acc + term
    o_ref[...] = acc.astype(o_ref.dtype)


def merge_branches(hb, w_gates, ys, w_branch):
    n, d = hb.shape
    tm, tn = min(1024, n), min(256, d)
    nj = d // tn

    def wg(b):
        return pl.BlockSpec((d, tn), lambda i, j: (0, b * nj + j))

    yspec = pl.BlockSpec((tm, BRANCH_WIDTH), lambda i, j: (i, 0), pipeline_mode=pl.Buffered(1))
    return pl.pallas_call(
        _merge_body,
        grid=(n // tm, nj),
        in_specs=[pl.BlockSpec((tm, d), lambda i, j: (i, 0), pipeline_mode=pl.Buffered(1)),
                  wg(0), wg(1), wg(2), wg(3),
                  yspec, yspec, yspec, yspec,
                  pl.BlockSpec((N_BRANCH, BRANCH_WIDTH, tn), lambda i, j: (0, 0, j))],
        out_specs=pl.BlockSpec((tm, tn), lambda i, j: (i, j)),
        out_shape=jax.ShapeDtypeStruct((n, d), BF16),
        compiler_params=_cparams("parallel", "arbitrary"),
        name="merge_branches",
    )(hb, w_gates, w_gates, w_gates, w_gates, *ys, w_branch)


MOE_TN = 512
EXPERTS_PER_TILE = MOE_TN // EXPERT_FF


def _router_body(x_ref, w_ref, b_ref, comb_ref, combt_ref):
    logits = _dot(x_ref[...], w_ref[...]) + b_ref[...]
    lane = lax.broadcasted_iota(jnp.int32, logits.shape, 1).astype(F32)
    work = logits
    picks, vals = [], []
    for _ in range(TOP_K):
        mx = jnp.max(work, axis=-1, keepdims=True)
        first = jnp.min(jnp.where(work == mx, lane, float(LANES)), axis=-1, keepdims=True)
        pick = lane == first
        picks.append(pick)
        vals.append(mx)
        work = jnp.where(pick, -jnp.inf, work)
    exps = [jnp.exp(v - vals[0]) for v in vals]
    den = exps[0]
    for e in exps[1:]:
        den = den + e
    comb = jnp.zeros_like(logits)
    for pick, e in zip(picks, exps):
        comb = comb + jnp.where(pick, e / den, 0.0)
    comb_ref[...] = comb
    for t in range(N_EXPERTS // EXPERTS_PER_TILE):
        combt_ref[t] = comb[:, t * EXPERTS_PER_TILE:(t + 1) * EXPERTS_PER_TILE]


def moe_router(hb, router_w, router_b):
    n, d = hb.shape
    tm = min(512, n)
    w = jnp.zeros((d, LANES), F32).at[:, :N_EXPERTS].set(router_w).astype(BF16)
    b = jnp.full((1, LANES), NEG, F32).at[0, :N_EXPERTS].set(router_b)
    nt = N_EXPERTS // EXPERTS_PER_TILE
    return pl.pallas_call(
        _router_body,
        grid=(n // tm,),
        in_specs=[pl.BlockSpec((tm, d), lambda i: (i, 0)), pl.BlockSpec((d, LANES), lambda i: (0, 0)),
                  pl.BlockSpec((1, LANES), lambda i: (0, 0))],
        out_specs=[pl.BlockSpec((tm, LANES), lambda i: (i, 0)),
                   pl.BlockSpec((nt, tm, EXPERTS_PER_TILE), lambda i: (0, i, 0))],
        out_shape=[jax.ShapeDtypeStruct((n, LANES), F32), jax.ShapeDtypeStruct((nt, n, EXPERTS_PER_TILE), F32)],
        compiler_params=_cparams("parallel"),
        name="moe_router",
    )(hb, w, b)


def _moe_up_body(x_ref, wg_ref, wu_ref, bg_ref, bu_ref, comb_ref, a_ref):
    x = x_ref[...]
    gate = jnp.minimum(_dot(x, wg_ref[...]) + bg_ref[...], SWIGLU_LIMIT)
    up = jnp.clip(_dot(x, wu_ref[...]) + bu_ref[...], -SWIGLU_LIMIT, SWIGLU_LIMIT)
    act = gate * _sigmoid(SWIGLU_ALPHA * gate) * (up + 1.0)
    for e in range(EXPERTS_PER_TILE):
        es = slice(e * EXPERT_FF, (e + 1) * EXPERT_FF)
        a_ref[:, es] = (act[:, es] * comb_ref[:, e:e + 1]).astype(BF16)


def moe_up(hb, w_gate, w_up, b_gate, b_up, comb_t):
    n, d = hb.shape
    tm = min(512, n)
    width = N_EXPERTS * EXPERT_FF
    wspec = pl.BlockSpec((d, MOE_TN), lambda i, j: (0, j))
    bspec = pl.BlockSpec((1, MOE_TN), lambda i, j: (0, j))
    return pl.pallas_call(
        _moe_up_body,
        grid=(n // tm, width // MOE_TN),
        in_specs=[pl.BlockSpec((tm, d), lambda i, j: (i, 0)), wspec, wspec, bspec, bspec,
                  pl.BlockSpec((None, tm, EXPERTS_PER_TILE), lambda i, j: (j, i, 0))],
        out_specs=pl.BlockSpec((tm, MOE_TN), lambda i, j: (i, j)),
        out_shape=jax.ShapeDtypeStruct((n, width), BF16),
        compiler_params=_cparams("parallel", "arbitrary"),
        name="moe_up",
    )(hb, w_gate, w_up, b_gate, b_up, comb_t)


def _moe_down_body(a_ref, w_ref, comb_ref, b2_ref, o_ref):
    o_ref[...] = (_dot(a_ref[...], w_ref[...]) + _dot(comb_ref[...].astype(BF16), b2_ref[...])).astype(o_ref.dtype)


def moe_down(a, w2, comb, b2):
    n, k = a.shape
    d = w2.shape[1]
    tm, tn = min(1024, n), min(512, d)
    return pl.pallas_call(
        _moe_down_body,
        grid=(n // tm, d // tn),
        in_specs=[pl.BlockSpec((tm, k), lambda i, j: (i, 0)), pl.BlockSpec((k, tn), lambda i, j: (0, j)),
                  pl.BlockSpec((tm, LANES), lambda i, j: (i, 0)), pl.BlockSpec((LANES, tn), lambda i, j: (0, j))],
        out_specs=pl.BlockSpec((tm, tn), lambda i, j: (i, j)),
        out_shape=jax.ShapeDtypeStruct((n, d), BF16),
        compiler_params=_cparams("parallel", "arbitrary"),
        name="moe_down",
    )(a, w2, comb, b2)


def _regroup_body(w_ref, u_ref, g_ref):
    o = _SRC_OFF
    segments = ((o[0], o[4]),
                (o[5], o[8]),
                (o[10], o[13]),
                (o[13], o[20]),
                (o[4], o[5]), (o[8], o[10]), (o[20], o[21]))
    pos = 0
    for a, b in segments:
        u_ref[:, pos:pos + (b - a)] = w_ref[:, a:b].astype(BF16)
        pos += b - a
    u_ref[:, pos:] = jnp.zeros((u_ref.shape[0], U_WIDTH - pos), BF16)
    g_ref[...] = w_ref[:, SRC_GATES:].astype(BF16)


def _regroup_w_in(w_in, l):
    _, d, width = w_in.shape
    rows = LANES
    return pl.pallas_call(
        _regroup_body,
        grid=(d // rows,),
        in_specs=[pl.BlockSpec((None, rows, width), lambda i: (l, i, 0))],
        out_specs=[pl.BlockSpec((rows, U_WIDTH), lambda i: (i, 0)),
                   pl.BlockSpec((rows, width - SRC_GATES), lambda i: (i, 0))],
        out_shape=[jax.ShapeDtypeStruct((d, U_WIDTH), BF16), jax.ShapeDtypeStruct((d, width - SRC_GATES), BF16)],
        compiler_params=_cparams("parallel"),
        name="regroup_w_in",
    )(w_in)


def kernel(x, rel_bias, w_in, gla_gate_w, gla_gate_b, gla_norm_g, mlstm_conv_w, mlstm_conv_b, mlstm_igate_b,
           mlstm_fgate_b, mlstm_norm_g, nsa_pe_k, nsa_pe_v, nsa_ck_w1, nsa_ck_w2, nsa_cv_w1, nsa_cv_w2, w_branch,
           w_out, ln1_g, ln1_b, router_w, router_b, exp_w1, exp_b1, exp_w2, exp_b2, ln2_g, ln2_b):
    out_dtype = x.dtype
    bsz, s, d = x.shape
    n = bsz * s
    h = x.astype(F32).reshape(n, d)
    hb = h.astype(BF16)
    thr = bucket_thresholds(max(s, BIAS_DIST_RANGE))
    tab_t = rel_bias.T
    dil_bias = dilated_bias(thr, tab_t)
    nsa_tt, nsa_tw, nsa_bc = nsa_bias(thr, tab_t, s)
    for l in range(w_in.shape[0]):
        w_u, w_gates = _regroup_w_in(w_in, l)
        u = matmul(hb, w_u, F32, 1024, 1024, "input_projection")
        u3 = u.reshape(bsz, s, U_WIDTH)
        y_a = gla_mixer(u3, gla_gate_w[l], gla_gate_b[l], gla_norm_g[l]).reshape(n, BRANCH_WIDTH)
        y_b = mlstm_mixer(u3, mlstm_conv_w[l], mlstm_conv_b[l], mlstm_igate_b[l], mlstm_fgate_b[l],
                          mlstm_norm_g[l]).reshape(n, BRANCH_WIDTH)
        y_c = dilated_mixer(u3, dil_bias).reshape(n, BRANCH_WIDTH)
        y_d = nsa_mixer(u3, nsa_tt, nsa_tw, nsa_bc, nsa_pe_k[l], nsa_pe_v[l], nsa_ck_w1[l], nsa_ck_w2[l], nsa_cv_w1[l],
                        nsa_cv_w2[l]).reshape(n, BRANCH_WIDTH)
        merged = merge_branches(hb, w_gates, (y_a, y_b, y_c, y_d), w_branch[l].astype(BF16))
        attn = matmul(merged, w_out[l].astype(BF16), BF16, 1024, 512, "output_projection")
        h, hb = ln_residual(h, attn, ln1_g[l], ln1_b[l], "layer_norm_1")

        comb, comb_t = moe_router(hb, router_w[l], router_b[l])
        w1 = exp_w1[l]
        w_gate = w1[:, :, :EXPERT_FF].transpose(1, 0, 2).reshape(d, N_EXPERTS * EXPERT_FF).astype(BF16)
        w_up = w1[:, :, EXPERT_FF:].transpose(1, 0, 2).reshape(d, N_EXPERTS * EXPERT_FF).astype(BF16)
        b_gate = exp_b1[l][:, :EXPERT_FF].reshape(1, -1)
        b_up = exp_b1[l][:, EXPERT_FF:].reshape(1, -1)
        act = moe_up(hb, w_gate, w_up, b_gate, b_up, comb_t)
        b2 = jnp.zeros((LANES, d), F32).at[:N_EXPERTS].set(exp_b2[l]).astype(BF16)
        ffn = moe_down(act, exp_w2[l].reshape(N_EXPERTS * EXPERT_FF, d).astype(BF16), comb, b2)
        h, hb = ln_residual(h, ffn, ln2_g[l], ln2_b[l], "layer_norm_2")
    return h.reshape(bsz, s, d).astype(out_dtype)
```

```python
import functools
import math

import numpy as np
import jax
import jax.numpy as jnp
from jax import lax
from jax.experimental import pallas as pl
from jax.experimental.pallas import tpu as pltpu

F32 = jnp.float32
BF16 = jnp.bfloat16
HIGHEST = lax.Precision.HIGHEST

N_LAYERS_FOR_DEEPNORM = 4
HEAD_DIM = 128
BRANCH_WIDTH = 1024
N_BRANCH = 4
N_HEADS = 8
DK = 64
CHUNK = 64
GLA_GATE_RANK = 16
GLA_TAU = 16.0
MLSTM_CONV = 4
MLSTM_IGATE_CAP = 15.0
DIL_PATTERNS = ((128, 1), (512, 4), (2048, 16))
DIL_BLOCK = 128
NSA_KV_GROUPS = 2
NSA_HPG = 4
NSA_CMP_BLOCK = 32
NSA_CMP_STRIDE = 16
NSA_CMP_HIDDEN = 256
NSA_SEL_BLOCK = 64
NSA_N_SEL = 16
NSA_WINDOW = 512
NSA_SEL_TK = 512
REL_BUCKETS = 32
REL_MAX_DIST = 2048
N_EXPERTS = 32
TOP_K = 4
EXPERT_FF = 128
SWIGLU_LIMIT = 7.0
SWIGLU_ALPHA = 1.702
DEEPNORM_ALPHA = (2 * N_LAYERS_FOR_DEEPNORM) ** 0.25
LN_EPS = 1e-5
NEG = -1e30
BIG = 1e9

LANES = 128
VMEM_LIMIT_BYTES = 56 * 1024 * 1024

U_GQ, U_GK, U_GV, U_GR = 0, 512, 1024, 2048
U_MQK, U_MV, U_MO = 3072, 4096, 5120
U_DQ, U_DK, U_DV = 6144, 9216, 10240
U_NQ = 11264
U_NKC, U_NVC, U_NKS, U_NVS, U_NKW, U_NVW = 12288, 12544, 12800, 13056, 13312, 13568
U_MISC = 13824
U_WIDTH = 14336
MISC_GA, MISC_MI, MISC_MF, MISC_NG = 0, 16, 24, 32

_SRC_SIZES = (512, 512, 1024, 1024, 16, 1024, 1024, 1024, 8, 8, 3072, 1024, 1024, 1024,
              256, 256, 256, 256, 256, 256, 24)
_SRC_OFF = np.concatenate([[0], np.cumsum(_SRC_SIZES)]).tolist()
SRC_GATES = _SRC_OFF[-1]


def _cparams(*sem):
    return pltpu.CompilerParams(dimension_semantics=sem, vmem_limit_bytes=VMEM_LIMIT_BYTES)


def _log_sigmoid(x):
    return jnp.minimum(x, 0.0) - jnp.log1p(jnp.exp(-jnp.abs(x)))


def _sigmoid(x):
    return 1.0 / (1.0 + jnp.exp(-x))


def _silu(x):
    return x * _sigmoid(x)


def _dot(a, b):
    return jnp.dot(a, b, preferred_element_type=F32)


def _dot_t(a, b):
    return lax.dot_general(a, b, (((1,), (1,)), ((), ())), preferred_element_type=F32)


def _tdot(a, b):
    return lax.dot_general(a, b, (((0,), (0,)), ((), ())), preferred_element_type=F32)


def _head_norm(o, g_row):
    mu = jnp.mean(o, axis=-1, keepdims=True)
    d = o - mu
    var = jnp.mean(d * d, axis=-1, keepdims=True)
    return d * lax.rsqrt(var + LN_EPS) * g_row


def t5_bucket(dist):
    d = jnp.maximum(dist, 0)
    exact = REL_BUCKETS // 2
    df = jnp.maximum(d, 1).astype(jnp.float32)
    large = exact + (jnp.log(df / exact) / math.log(REL_MAX_DIST / exact) * (REL_BUCKETS - exact)).astype(jnp.int32)
    return jnp.where(d < exact, d, jnp.minimum(large, REL_BUCKETS - 1))


def _mm_body(x_ref, w_ref, o_ref):
    o_ref[...] = _dot(x_ref[...], w_ref[...]).astype(o_ref.dtype)


def matmul(x, w, out_dtype, tm, tn, name):
    m, k = x.shape
    n = w.shape[1]
    tm, tn = min(tm, m), min(tn, n)
    return pl.pallas_call(
        _mm_body,
        grid=(m // tm, n // tn),
        in_specs=[pl.BlockSpec((tm, k), lambda i, j: (i, 0)), pl.BlockSpec((k, tn), lambda i, j: (0, j))],
        out_specs=pl.BlockSpec((tm, tn), lambda i, j: (i, j)),
        out_shape=jax.ShapeDtypeStruct((m, n), out_dtype),
        compiler_params=_cparams("parallel", "arbitrary"),
        name=name,
    )(x, w)


def _ln_body(h_ref, d_ref, g_ref, b_ref, o_ref, ob_ref):
    z = DEEPNORM_ALPHA * h_ref[...] + d_ref[...]
    mu = jnp.mean(z, axis=-1, keepdims=True)
    zc = z - mu
    var = jnp.mean(zc * zc, axis=-1, keepdims=True)
    y = zc * lax.rsqrt(var + LN_EPS) * g_ref[...] + b_ref[...]
    o_ref[...] = y
    ob_ref[...] = y.astype(BF16)


def ln_residual(h, delta, g, b, name):
    n, d = h.shape
    tm = min(256, n)
    row = pl.BlockSpec((tm, d), lambda i: (i, 0))
    vec = pl.BlockSpec((1, d), lambda i: (0, 0))
    return pl.pallas_call(
        _ln_body,
        grid=(n // tm,),
        in_specs=[row, row, vec, vec],
        out_specs=[row, row],
        out_shape=[jax.ShapeDtypeStruct((n, d), F32), jax.ShapeDtypeStruct((n, d), BF16)],
        compiler_params=_cparams("parallel"),
        name=name,
    )(h, delta, g.reshape(1, d), b.reshape(1, d))


def _gla_body(q_ref, k_ref, v_ref, r_ref, misc_ref, gw_ref, gb_ref, ng_ref, o_ref, st_ref):
    L = CHUNK

    @pl.when(pl.program_id(1) == 0)
    def _():
        st_ref[...] = jnp.zeros_like(st_ref)

    pre = _dot(misc_ref[...].astype(BF16), gw_ref[...]) + gb_ref[...]
    log_a = _log_sigmoid(pre) / GLA_TAU
    row = lax.broadcasted_iota(jnp.int32, (L, L), 0)
    col = lax.broadcasted_iota(jnp.int32, (L, L), 1)
    causal = col <= row
    tri = jnp.where(causal, 1.0, 0.0).astype(F32)
    b = jnp.dot(tri, log_a, precision=HIGHEST, preferred_element_type=F32)
    b_last = b[L - 1:L, :]
    q_dec = (q_ref[...] * (DK ** -0.5) * jnp.exp(b)).astype(BF16)
    k_dec = (k_ref[...] * jnp.exp(-b)).astype(BF16)
    k_end = (k_ref[...] * jnp.exp(b_last - b)).astype(BF16)
    decay = jnp.exp(b_last)

    def heads(x, width):
        return jnp.stack([x[:, h * width:(h + 1) * width] for h in range(N_HEADS)], axis=0)

    q3, k3, ke3 = heads(q_dec, DK), heads(k_dec, DK), heads(k_end, DK)
    v3 = heads(v_ref[...].astype(BF16), HEAD_DIM)
    dec3 = heads(decay, DK)
    state_t = st_ref[...]
    att = jnp.where(causal[None], jnp.einsum('hqd,hkd->hqk', q3, k3, preferred_element_type=F32), 0.0)
    o3 = (jnp.einsum('hqk,hkv->hqv', att.astype(BF16), v3, preferred_element_type=F32)
          + jnp.einsum('hqd,hvd->hqv', q3, state_t.astype(BF16), preferred_element_type=F32))
    st_ref[...] = dec3 * state_t + jnp.einsum('hkv,hkd->hvd', v3, ke3, preferred_element_type=F32)
    on3 = _head_norm(o3, heads(ng_ref[...], HEAD_DIM))
    out3 = (_silu(heads(r_ref[...], HEAD_DIM)) * on3).astype(BF16)
    for h in range(N_HEADS):
        o_ref[:, h * HEAD_DIM:(h + 1) * HEAD_DIM] = out3[h]


def gla_mixer(u3, gate_w, gate_b, norm_g):
    bsz, s, _ = u3.shape
    L = CHUNK
    gw = jnp.zeros((LANES, N_HEADS * DK), F32).at[MISC_GA:MISC_GA + GLA_GATE_RANK].set(gate_w).astype(BF16)

    def ublock(width, off):
        return pl.BlockSpec((None, L, width), lambda b, t: (b, t, off // width))

    def const(shape):
        return pl.BlockSpec(shape, lambda b, t: (0,) * len(shape))

    return pl.pallas_call(
        _gla_body,
        grid=(bsz, s // L),
        in_specs=[ublock(512, U_GQ), ublock(512, U_GK), ublock(1024, U_GV), ublock(1024, U_GR),
                  ublock(LANES, U_MISC), const((LANES, 512)), const((1, 512)), const((1, 1024))],
        out_specs=pl.BlockSpec((None, L, 1024), lambda b, t: (b, t, 0)),
        out_shape=jax.ShapeDtypeStruct((bsz, s, BRANCH_WIDTH), BF16),
        scratch_shapes=[pltpu.VMEM((N_HEADS, HEAD_DIM, DK), F32)],
        compiler_params=_cparams("parallel", "arbitrary"),
        name="gla_mixer",
    )(u3, u3, u3, u3, u3, gw, gate_b.reshape(1, -1), norm_g.reshape(1, -1))


def _mlstm_body(qk_ref, v_ref, op_ref, misc_ref, cw_ref, cb_ref, ib_ref, fb_ref, ng_ref, o_ref,
                ext_ref, c_ref, n_ref, m_ref):
    L = CHUNK
    C2 = 2 * N_HEADS * DK

    @pl.when(pl.program_id(1) == 0)
    def _():
        ext_ref[0:8, :] = jnp.zeros((8, C2), F32)
        c_ref[...] = jnp.zeros_like(c_ref)
        n_ref[...] = jnp.zeros_like(n_ref)
        m_ref[...] = jnp.zeros_like(m_ref)

    x = qk_ref[...]
    ext_ref[8:8 + L, :] = x
    y = (cb_ref[...] + cw_ref[3:4, :] * x + cw_ref[2:3, :] * ext_ref[pl.ds(7, L), :]
         + cw_ref[1:2, :] * ext_ref[pl.ds(6, L), :] + cw_ref[0:1, :] * ext_ref[pl.ds(5, L), :])
    ext_ref[0:8, :] = x[L - 8:L, :]
    qk = _silu(y)
    q_all = qk[:, :N_HEADS * DK].astype(BF16)
    k_all = qk[:, N_HEADS * DK:] * (DK ** -0.5)

    misc = misc_ref[...]
    i_g = MLSTM_IGATE_CAP * jnp.tanh((misc + ib_ref[...]) / MLSTM_IGATE_CAP)
    log_f = _log_sigmoid(misc + fb_ref[...])
    row = lax.broadcasted_iota(jnp.int32, (L, L), 0)
    col = lax.broadcasted_iota(jnp.int32, (L, L), 1)
    causal = col <= row
    tri = jnp.where(causal, 1.0, 0.0).astype(F32)
    b = jnp.dot(tri, log_f, precision=HIGHEST, preferred_element_type=F32)
    it = pltpu.roll(i_g, MISC_MF - MISC_MI, 1)
    b_last = b[L - 1:L, :]
    m_s = m_ref[0:1, :]
    a_end = b_last - b + it
    m_new = jnp.maximum(b_last + m_s, jnp.max(a_end, axis=0, keepdims=True))
    w = jnp.exp(a_end - m_new)
    sc = jnp.exp(b_last + m_s - m_new)
    inter_log = b + m_s
    c = it - b
    rows_i = lax.broadcasted_iota(jnp.int32, c.shape, 0)
    cmax = c
    for sh in (1, 2, 4, 8, 16, 32):
        cmax = jnp.maximum(cmax, jnp.where(rows_i >= sh, pltpu.roll(cmax, sh, 0), -jnp.inf))
    m_i = jnp.maximum(inter_log, b + cmax)
    w_inter = jnp.exp(inter_log - m_i)
    e_neg = jnp.exp(-m_i)
    m_ref[0:1, :] = m_new

    stats = jnp.concatenate([b - m_i, w, w_inter, e_neg, jnp.broadcast_to(sc, (8, LANES))], axis=0)
    lane = lax.broadcasted_iota(jnp.int32, stats.shape, 1)
    stats = jnp.where((lane >= MISC_MF) & (lane < MISC_MF + N_HEADS), stats, 0.0)
    sel_r = lax.broadcasted_iota(jnp.int32, (LANES, N_HEADS * LANES), 0)
    sel_c = lax.broadcasted_iota(jnp.int32, (LANES, N_HEADS * LANES), 1) // LANES
    onehot = jnp.where(sel_r == sel_c + MISC_MF, 1.0, 0.0).astype(F32)
    spread = jnp.dot(stats, onehot, precision=HIGHEST, preferred_element_type=F32)

    def heads(x, width):
        return jnp.stack([x[:, h * width:(h + 1) * width] for h in range(N_HEADS)], axis=0)

    dcol3 = heads(spread[0:L], LANES)[:, :, 0:L]
    w3 = heads(spread[L:2 * L], LANES)[:, :, 0:DK]
    wi3 = heads(spread[2 * L:3 * L], LANES)
    en3 = heads(spread[3 * L:4 * L], LANES)
    sc3 = heads(spread[4 * L:4 * L + 1], LANES)
    c_t = c.T
    crow3 = jnp.stack([c_t[MISC_MF + h:MISC_MF + h + 1, :] for h in range(N_HEADS)], axis=0)

    q3 = heads(q_all, DK)
    k3 = heads(k_all, DK)
    v3 = heads(v_ref[...].astype(BF16), HEAD_DIM)
    c_s = c_ref[...]
    n_s = n_ref[...]
    decay3 = jnp.exp(jnp.where(causal[None], dcol3 + crow3, -jnp.inf))
    qk_s = jnp.einsum('hqd,hkd->hqk', q3, k3.astype(BF16), preferred_element_type=F32) * decay3
    num = (wi3 * jnp.einsum('hqd,hdv->hqv', q3, c_s.astype(BF16), preferred_element_type=F32)
           + jnp.einsum('hqk,hkv->hqv', qk_s.astype(BF16), v3, preferred_element_type=F32))
    qn = jnp.sum(q3.astype(F32) * n_s, axis=-1, keepdims=True)
    den = wi3[:, :, 0:1] * qn + jnp.sum(qk_s, axis=-1, keepdims=True)
    hout = num / jnp.maximum(jnp.abs(den), en3[:, :, 0:1])
    wk3 = w3 * k3
    c_ref[...] = sc3 * c_s + jnp.einsum('hkd,hkv->hdv', wk3.astype(BF16), v3, preferred_element_type=F32)
    n_ref[...] = sc3[:, :, 0:DK] * n_s + jnp.sum(wk3, axis=1, keepdims=True)
    hn = _head_norm(hout, heads(ng_ref[...], HEAD_DIM))
    out3 = (_sigmoid(heads(op_ref[...], HEAD_DIM)) * hn).astype(BF16)
    for h in range(N_HEADS):
        o_ref[:, h * HEAD_DIM:(h + 1) * HEAD_DIM] = out3[h]


def mlstm_mixer(u3, conv_w, conv_b, igate_b, fgate_b, norm_g):
    bsz, s, _ = u3.shape
    L = CHUNK
    ib = jnp.zeros((1, LANES), F32).at[0, MISC_MI:MISC_MI + N_HEADS].set(igate_b)
    fb = jnp.zeros((1, LANES), F32).at[0, MISC_MF:MISC_MF + N_HEADS].set(fgate_b)

    def ublock(width, off):
        return pl.BlockSpec((None, L, width), lambda b, t: (b, t, off // width))

    def const(shape):
        return pl.BlockSpec(shape, lambda b, t: (0,) * len(shape))

    return pl.pallas_call(
        _mlstm_body,
        grid=(bsz, s // L),
        in_specs=[ublock(1024, U_MQK), ublock(1024, U_MV), ublock(1024, U_MO), ublock(LANES, U_MISC),
                  const((MLSTM_CONV, 1024)), const((1, 1024)), const((1, LANES)), const((1, LANES)),
                  const((1, 1024))],
        out_specs=pl.BlockSpec((None, L, 1024), lambda b, t: (b, t, 0)),
        out_shape=jax.ShapeDtypeStruct((bsz, s, BRANCH_WIDTH), BF16),
        scratch_shapes=[pltpu.VMEM((L + 8, 1024), F32), pltpu.VMEM((N_HEADS, DK, HEAD_DIM), F32),
                        pltpu.VMEM((N_HEADS, 1, DK), F32), pltpu.VMEM((8, LANES), F32)],
        compiler_params=_cparams("parallel", "arbitrary"),
        name="mlstm_mixer",
    )(u3, u3, u3, u3, conv_w, conv_b.reshape(1, -1), ib, fb, norm_g.reshape(1, -1))


BIAS_DIST_RANGE = 4096


def bucket_thresholds(max_dist):
    lut = t5_bucket(jnp.arange(max_dist))
    return jnp.sum(lut[None, :] < jnp.arange(REL_BUCKETS)[:, None], axis=1).astype(jnp.int32)


def _bias_table_body(thr_ref, tab_ref, o_ref, *, kind):
    a = pl.program_id(0)
    c = pl.program_id(1)
    shape = o_ref.shape
    i = lax.broadcasted_iota(jnp.int32, shape, 0)
    j = lax.broadcasted_iota(jnp.int32, shape, 1)
    if kind == "dilated":
        dil = jnp.where(a == 0, DIL_PATTERNS[0][1], jnp.where(a == 1, DIL_PATTERNS[1][1], DIL_PATTERNS[2][1]))
        dist = (i + DIL_BLOCK - j) * dil
        head = a * N_HEADS + c
    elif kind in ("causal", "window"):
        dist = c * shape[0] + i - j
        head = 3 * N_HEADS + a
    else:
        dist = c * shape[0] + i - (j * NSA_CMP_STRIDE + NSA_CMP_BLOCK - 1)
        head = 3 * N_HEADS + a
    dist_c = jnp.maximum(dist, 0)
    acc = jnp.full(shape, tab_ref[head, 0], F32)
    for k in range(1, REL_BUCKETS):
        acc = jnp.where(dist_c >= thr_ref[k], tab_ref[head, k], acc)
    if kind in ("causal", "window"):
        limit = NSA_WINDOW if kind == "window" else 2 ** 30
        dist_v = jnp.where(c < pl.num_programs(1) - 1, dist, -1)
        acc = jnp.where((dist_v >= 0) & (dist_v < limit), acc, NEG)
    o_ref[...] = acc


def bias_table(thr, tab_t, kind, out_dims, block):
    smem = pl.BlockSpec(memory_space=pltpu.SMEM)
    return pl.pallas_call(
        functools.partial(_bias_table_body, kind=kind),
        grid=out_dims[:2] if kind != "compressed" else (out_dims[0], out_dims[1] // block[0]),
        in_specs=[smem, smem],
        out_specs=(pl.BlockSpec((None, None) + block, lambda a, c: (a, c, 0, 0)) if kind != "compressed"
                   else pl.BlockSpec((None,) + block, lambda a, c: (a, c, 0))),
        out_shape=jax.ShapeDtypeStruct(out_dims, F32),
        compiler_params=_cparams("parallel", "parallel"),
        name=f"bias_table_{kind}",
    )(thr, tab_t)


DIL_SPAN = 2048


def _dil_body(q0_ref, q1_ref, q2_ref, kp_ref, kc_ref, vp_ref, vc_ref, bias_ref, o_ref, m_scr, l_scr, acc_scr):
    P = DIL_BLOCK
    NB = DIL_SPAN // P
    q_refs = (q0_ref, q1_ref, q2_ref)
    row = lax.broadcasted_iota(jnp.int32, (P, P), 0)
    col = lax.broadcasted_iota(jnp.int32, (P, P), 1)
    mask_cur = jnp.where(col <= row, 0.0, NEG)
    mask_prev = jnp.where(col >= row, 0.0, NEG)
    no_prev = jnp.where(pl.program_id(1) > 0, 0, P)
    mask_prev_first = jnp.where((col - row) >= no_prev, 0.0, NEG)

    def rows(r, n, dil):
        return pl.ds(r, n, stride=dil) if dil > 1 else pl.ds(r, n)

    def stacked(ref, dil):
        per = DIL_SPAN // dil
        return jnp.concatenate([ref[rows(r, per, dil), :].reshape(per // P, P, HEAD_DIM) for r in range(dil)], axis=0)

    def stacked_prev(cur3, prev_ref, dil):
        nblk = NB // dil
        parts = []
        for r in range(dil):
            parts.append(prev_ref[rows(r + DIL_SPAN - dil * P, P, dil), :].reshape(1, P, HEAD_DIM))
            if nblk > 1:
                parts.append(cur3[r * nblk:(r + 1) * nblk - 1])
        return jnp.concatenate(parts, axis=0)

    for g, (_, dil) in enumerate(DIL_PATTERNS):
        nblk = NB // dil
        per = DIL_SPAN // dil
        q3 = (stacked(q_refs[g], dil) * (HEAD_DIM ** -0.5)).astype(BF16)
        kc3 = stacked(kc_ref, dil)
        vc3 = stacked(vc_ref, dil)
        kp3 = stacked_prev(kc3, kp_ref, dil).astype(BF16)
        vp3 = stacked_prev(vc3, vp_ref, dil).astype(BF16)
        kc3 = kc3.astype(BF16)
        vc3 = vc3.astype(BF16)
        bias_p = bias_ref[g, :, 0:P]
        bias_prev3 = jnp.concatenate(
            [(bias_p + (mask_prev_first if b % nblk == 0 else mask_prev)).reshape(1, P, P) for b in range(NB)], axis=0)
        s_p = jnp.einsum('nqd,nkd->nqk', q3, kp3, preferred_element_type=F32) + bias_prev3
        s_c = (jnp.einsum('nqd,nkd->nqk', q3, kc3, preferred_element_type=F32)
               + (bias_ref[g, :, P:2 * P] + mask_cur)[None])
        m = jnp.maximum(jnp.max(s_p, axis=-1, keepdims=True), jnp.max(s_c, axis=-1, keepdims=True))
        p_p = jnp.exp(s_p - m)
        p_c = jnp.exp(s_c - m)
        l = jnp.sum(p_p, axis=-1, keepdims=True) + jnp.sum(p_c, axis=-1, keepdims=True)
        acc = (jnp.einsum('nqk,nkd->nqd', p_p.astype(BF16), vp3, preferred_element_type=F32)
               + jnp.einsum('nqk,nkd->nqd', p_c.astype(BF16), vc3, preferred_element_type=F32))
        for r in range(dil):
            sl = rows(g * DIL_SPAN + r, per, dil)
            bs = slice(r * nblk, (r + 1) * nblk)
            m_scr[sl, :] = jnp.broadcast_to(m[bs].reshape(per, 1), (per, LANES))
            l_scr[sl, :] = jnp.broadcast_to(l[bs].reshape(per, 1), (per, LANES))
            acc_scr[sl, :] = acc[bs].reshape(per, HEAD_DIM)

    ms = [m_scr[g * DIL_SPAN:(g + 1) * DIL_SPAN, :] for g in range(len(DIL_PATTERNS))]
    m_all = jnp.maximum(jnp.maximum(ms[0], ms[1]), ms[2])
    num = jnp.zeros((DIL_SPAN, HEAD_DIM), F32)
    den = jnp.zeros((DIL_SPAN, LANES), F32)
    for g in range(len(DIL_PATTERNS)):
        w = jnp.exp(ms[g] - m_all)
        num = num + w * acc_scr[g * DIL_SPAN:(g + 1) * DIL_SPAN, :]
        den = den + w * l_scr[g * DIL_SPAN:(g + 1) * DIL_SPAN, :]
    o_ref[...] = (num / den).astype(BF16)


def dilated_bias(thr, tab_t):
    P = DIL_BLOCK
    return bias_table(thr, tab_t, "dilated", (len(DIL_PATTERNS), N_HEADS, P, 2 * P), (P, 2 * P))


def dilated_mixer(u3, bias):
    bsz, s, _ = u3.shape
    P = DIL_BLOCK
    assert s % DIL_SPAN == 0 and all(w // d == P and DIL_SPAN % (d * P) == 0 for w, d in DIL_PATTERNS)

    def ub(off, prev):
        def idx(b, t, h):
            return (b, jnp.maximum(t - 1, 0) if prev else t, off // HEAD_DIM + h)
        return pl.BlockSpec((None, DIL_SPAN, HEAD_DIM), idx)

    return pl.pallas_call(
        _dil_body,
        grid=(bsz, s // DIL_SPAN, N_HEADS),
        in_specs=[ub(U_DQ, False), ub(U_DQ + 1024, False), ub(U_DQ + 2048, False),
                  ub(U_DK, True), ub(U_DK, False), ub(U_DV, True), ub(U_DV, False),
                  pl.BlockSpec((len(DIL_PATTERNS), None, P, 2 * P), lambda b, t, h: (0, h, 0, 0))],
        out_specs=pl.BlockSpec((None, DIL_SPAN, HEAD_DIM), lambda b, t, h: (b, t, h)),
        out_shape=jax.ShapeDtypeStruct((bsz, s, BRANCH_WIDTH), BF16),
        scratch_shapes=[pltpu.VMEM((len(DIL_PATTERNS) * DIL_SPAN, LANES), F32),
                        pltpu.VMEM((len(DIL_PATTERNS) * DIL_SPAN, LANES), F32),
                        pltpu.VMEM((len(DIL_PATTERNS) * DIL_SPAN, HEAD_DIM), F32)],
        compiler_params=_cparams("parallel", "parallel", "arbitrary"),
        name="dilated_attn",
    )(u3, u3, u3, u3, u3, u3, u3, bias)


def _gelu_tanh(x):
    return 0.5 * x * (1.0 + jnp.tanh(math.sqrt(2.0 / math.pi) * (x + 0.044715 * (x * x * x))))


def _nsa_compress_body(t_ref, pe_ref, w1_ref, w2_ref, o_ref):
    nblk = t_ref.shape[0] // NSA_CMP_STRIDE
    half = NSA_CMP_STRIDE
    t1 = jnp.zeros((nblk, NSA_CMP_HIDDEN), F32)
    t2 = jnp.zeros((nblk, NSA_CMP_HIDDEN), F32)
    for p in range(half):
        xp = t_ref[pl.ds(p, nblk, stride=NSA_CMP_STRIDE), :]
        t1 = t1 + _dot((xp + pe_ref[p:p + 1, :]).astype(BF16), w1_ref[p * HEAD_DIM:(p + 1) * HEAD_DIM, :])
        t2 = t2 + _dot((xp + pe_ref[half + p:half + p + 1, :]).astype(BF16),
                       w1_ref[(half + p) * HEAD_DIM:(half + p + 1) * HEAD_DIM, :])
    hidden = t1 + pltpu.roll(t2, nblk - 1, 0)
    o_ref[...] = _dot(_gelu_tanh(hidden).astype(BF16), w2_ref[...])


def _nsa_compress(u3, off, pe, w1, w2, name):
    bsz, s, _ = u3.shape
    nblk = s // NSA_CMP_STRIDE
    G = NSA_KV_GROUPS
    return pl.pallas_call(
        _nsa_compress_body,
        grid=(bsz, G),
        in_specs=[pl.BlockSpec((None, s, HEAD_DIM), lambda b, g: (b, 0, off // HEAD_DIM + g)),
                  pl.BlockSpec((NSA_CMP_BLOCK, HEAD_DIM), lambda b, g: (0, 0)),
                  pl.BlockSpec((NSA_CMP_BLOCK * HEAD_DIM, NSA_CMP_HIDDEN), lambda b, g: (0, 0)),
                  pl.BlockSpec((NSA_CMP_HIDDEN, HEAD_DIM), lambda b, g: (0, 0))],
        out_specs=pl.BlockSpec((None, None, nblk, HEAD_DIM), lambda b, g: (b, g, 0, 0)),
        out_shape=jax.ShapeDtypeStruct((bsz, G, nblk, HEAD_DIM), F32),
        compiler_params=_cparams("parallel", "parallel"),
        name=name,
    )(u3, pe, w1.astype(BF16), w2.astype(BF16))


def _nsa_cmp_body(q_ref, kc_ref, vc_ref, bc_ref, agg_ref, oc_ref, sel_ref):
    TQ = q_ref.shape[0]
    ncb = kc_ref.shape[0]
    nsb = agg_ref.shape[1]
    t0 = pl.program_id(2) * TQ
    tpos = t0 + lax.broadcasted_iota(jnp.int32, (TQ, ncb), 0)
    ends = lax.broadcasted_iota(jnp.int32, (TQ, ncb), 1) * NSA_CMP_STRIDE + (NSA_CMP_BLOCK - 1)
    mask = ends <= tpos
    maskf = jnp.where(mask, 1.0, 0.0).astype(F32)
    kcb = kc_ref[...].astype(BF16)
    vcb = vc_ref[...].astype(BF16)
    psum = jnp.zeros((TQ, ncb), F32)
    for hg in range(NSA_HPG):
        hs = slice(hg * HEAD_DIM, (hg + 1) * HEAD_DIM)
        q = (q_ref[:, hs] * (HEAD_DIM ** -0.5)).astype(BF16)
        s = jnp.where(mask, _dot_t(q, kcb) + bc_ref[hg], NEG)
        p = jnp.exp(s - jnp.max(s, axis=-1, keepdims=True)) * maskf
        l = jnp.sum(p, axis=-1, keepdims=True)
        p = p / jnp.where(l > 0, l, 1.0)
        oc_ref[:, hs] = _dot(p.astype(BF16), vcb)
        psum = psum + p
    imp = jnp.dot(psum, agg_ref[...], precision=HIGHEST, preferred_element_type=F32)
    jblk = lax.broadcasted_iota(jnp.int32, (TQ, nsb), 1)
    tq = t0 + lax.broadcasted_iota(jnp.int32, (TQ, nsb), 0)
    cur = tq // NSA_SEL_BLOCK
    forced = (jblk == 0) | (jblk == cur) | (jblk == cur - 1)
    score = jnp.where(forced, BIG, jnp.where(jblk * NSA_SEL_BLOCK <= tq, imp, -BIG))
    jf = jblk.astype(F32)
    sel = jnp.zeros((TQ, nsb), F32)
    for _ in range(min(NSA_N_SEL, nsb)):
        mx = jnp.max(score, axis=-1, keepdims=True)
        first = jnp.min(jnp.where(score == mx, jf, float(nsb)), axis=-1, keepdims=True)
        pick = jf == first
        sel = jnp.where(pick, 1.0, sel)
        score = jnp.where(pick, -jnp.inf, score)
    sel_ref[...] = sel


def _sel_agg_matrix(n_cmp, n_sb):
    r = NSA_SEL_BLOCK // NSA_CMP_STRIDE
    c = NSA_CMP_BLOCK // NSA_CMP_STRIDE
    jj, aa, bb = np.meshgrid(np.arange(n_sb), np.arange(r), np.arange(c), indexing='ij')
    ii = r * jj + aa + bb - 1
    ok = (ii >= 0) & (ii < n_cmp)
    mat = np.zeros((n_cmp, n_sb), np.float32)
    np.add.at(mat, (ii[ok], jj[ok]), 1.0)
    return mat


def _nsa_selwin_body(q_ref, oc_ref, sel_ref, misc_ref, ks_ref, vs_ref, kw_ref, vw_ref, tt_ref, tw_ref, o_ref,
                     ksb_ref, vsb_ref, m_ref, l_ref, acc_ref, s_ref):
    TQ = q_ref.shape[0]
    TK = NSA_SEL_TK
    HG = NSA_HPG
    nsb = sel_ref.shape[1]
    sub = TK // TQ
    g = pl.program_id(1)
    qi = pl.program_id(2)

    @pl.when(qi == 0)
    def _():
        ksb_ref[:, 0:HEAD_DIM] = ks_ref[...].astype(BF16)
        blk = lax.broadcasted_iota(jnp.int32, (ksb_ref.shape[0], nsb), 0) // NSA_SEL_BLOCK
        ksb_ref[:, HEAD_DIM:HEAD_DIM + nsb] = jnp.where(
            blk == lax.broadcasted_iota(jnp.int32, (ksb_ref.shape[0], nsb), 1), 1.0, 0.0).astype(BF16)
        vsb_ref[...] = vs_ref[...].astype(BF16)

    q4 = jnp.concatenate([(q_ref[:, hg * HEAD_DIM:(hg + 1) * HEAD_DIM] * (HEAD_DIM ** -0.5)).astype(BF16)
                          for hg in range(HG)], axis=0)
    unselected = jnp.where(sel_ref[...] > 0.5, 0.0, NEG).astype(BF16)
    q4_aug = jnp.concatenate([q4, jnp.concatenate([unselected] * HG, axis=0)], axis=1)
    masked_tile = tt_ref.shape[1] - 1

    m_ref[...] = jnp.full_like(m_ref, 0.1 * NEG)
    l_ref[...] = jnp.zeros_like(l_ref)
    acc_ref[...] = jnp.zeros_like(acc_ref)

    last_tile = ksb_ref.shape[0] // TK - 1

    def logits(kj):
        k0 = pl.multiple_of(kj * TK, TK)

        def tile_of(w):
            d = qi - (kj * sub + w)
            return jnp.where(d >= 0, d, masked_tile)

        bias4 = jnp.concatenate([jnp.concatenate([tt_ref[hg, tile_of(w)] for w in range(sub)], axis=1)
                                 for hg in range(HG)], axis=0)
        return _dot_t(q4_aug, ksb_ref[pl.ds(k0, TK), :]) + bias4

    s_ref[...] = logits(0)

    def sel_step(kj, carry):
        s = s_ref[...]
        v_tile = vsb_ref[pl.ds(pl.multiple_of(kj * TK, TK), TK), :]
        m_old = m_ref[...]
        m_new = jnp.maximum(m_old, jnp.max(s, axis=-1, keepdims=True))
        alpha = jnp.exp(m_old - m_new)
        p = jnp.exp(s - m_new)
        l_ref[...] = alpha * l_ref[...] + jnp.sum(p, axis=-1, keepdims=True)
        acc_ref[...] = alpha * acc_ref[...] + _dot(p.astype(BF16), v_tile)
        m_ref[...] = m_new
        s_ref[...] = logits(jnp.minimum(kj + 1, last_tile))
        return carry

    lax.fori_loop(0, qi // sub + 1, sel_step, 0)

    nwin = NSA_WINDOW // TQ + 1
    first = jnp.maximum(qi - (nwin - 1), 0)
    w0 = pl.multiple_of(first * TQ, TQ)
    kw_tile = kw_ref[pl.ds(w0, nwin * TQ), :].astype(BF16)
    vw_tile = vw_ref[pl.ds(w0, nwin * TQ), :].astype(BF16)

    def wtile_of(w):
        d = qi - (first + w)
        return jnp.where(d >= 0, d, tw_ref.shape[1] - 1)

    wbias4 = jnp.concatenate([jnp.concatenate([tw_ref[hg, wtile_of(w)] for w in range(nwin)], axis=1)
                              for hg in range(HG)], axis=0)
    s = _dot_t(q4, kw_tile) + wbias4
    p = jnp.exp(s - jnp.max(s, axis=-1, keepdims=True))
    o_win = _dot(p.astype(BF16), vw_tile) / jnp.sum(p, axis=-1, keepdims=True)
    o_sel = acc_ref[...] / l_ref[...]

    misc = misc_ref[...]
    gpre = jnp.where(g == 0, misc[:, MISC_NG:MISC_NG + 3 * HG], misc[:, MISC_NG + 3 * HG:MISC_NG + 6 * HG])
    gates = _sigmoid(gpre)
    for hg in range(HG):
        hs = slice(hg * HEAD_DIM, (hg + 1) * HEAD_DIM)
        rs = slice(hg * TQ, (hg + 1) * TQ)
        y = (gates[:, 3 * hg:3 * hg + 1] * oc_ref[:, hs] + gates[:, 3 * hg + 1:3 * hg + 2] * o_sel[rs]
             + gates[:, 3 * hg + 2:3 * hg + 3] * o_win[rs])
        o_ref[:, hs] = y.astype(BF16)


NSA_TQ = 128
NSA_CMP_TQ = 512


def nsa_bias(thr, tab_t, s):
    nh = NSA_KV_GROUPS * NSA_HPG
    tile = (NSA_TQ, NSA_TQ)
    tt = bias_table(thr, tab_t, "causal", (nh, s // NSA_TQ + 1) + tile, tile)
    tw = bias_table(thr, tab_t, "window", (nh, NSA_WINDOW // NSA_TQ + 2) + tile, tile)
    bc = bias_table(thr, tab_t, "compressed", (nh, s, s // NSA_CMP_STRIDE), (NSA_TQ, s // NSA_CMP_STRIDE))
    return tt, tw, bc


def nsa_mixer(u3, tt, tw, bc, pe_k, pe_v, ck_w1, ck_w2, cv_w1, cv_w2):
    bsz, s, _ = u3.shape
    G, HG, dh = NSA_KV_GROUPS, NSA_HPG, HEAD_DIM
    TQ = NSA_TQ
    nqt = s // TQ
    assert NSA_KV_GROUPS == 2 and s % NSA_SEL_TK == 0 and s >= NSA_WINDOW + TQ
    nblk = s // NSA_CMP_STRIDE
    n_cmp = (s - NSA_CMP_BLOCK) // NSA_CMP_STRIDE + 1
    n_sb = s // NSA_SEL_BLOCK
    agg = np.zeros((nblk, n_sb), np.float32)
    agg[:n_cmp] = _sel_agg_matrix(n_cmp, n_sb)
    agg = jnp.asarray(agg)

    k_cmp = _nsa_compress(u3, U_NKC, pe_k, ck_w1, ck_w2, "nsa_compress_k")
    v_cmp = _nsa_compress(u3, U_NVC, pe_v, cv_w1, cv_w2, "nsa_compress_v")

    qspec = pl.BlockSpec((None, TQ, HG * dh), lambda b, g, i: (b, i, U_NQ // (HG * dh) + g))
    TC = NSA_CMP_TQ
    o_cmp, sel = pl.pallas_call(
        _nsa_cmp_body,
        grid=(bsz, G, s // TC),
        in_specs=[pl.BlockSpec((None, TC, HG * dh), lambda b, g, i: (b, i, U_NQ // (HG * dh) + g)),
                  pl.BlockSpec((None, None, nblk, dh), lambda b, g, i: (b, g, 0, 0)),
                  pl.BlockSpec((None, None, nblk, dh), lambda b, g, i: (b, g, 0, 0)),
                  pl.BlockSpec((HG, TC, nblk), lambda b, g, i: (g, i, 0)),
                  pl.BlockSpec((nblk, n_sb), lambda b, g, i: (0, 0))],
        out_specs=[pl.BlockSpec((None, TC, HG * dh), lambda b, g, i: (b, i, g)),
                   pl.BlockSpec((None, None, TC, n_sb), lambda b, g, i: (b, g, i, 0))],
        out_shape=[jax.ShapeDtypeStruct((bsz, s, G * HG * dh), F32),
                   jax.ShapeDtypeStruct((bsz, G, s, n_sb), F32)],
        compiler_params=_cparams("parallel", "parallel", "arbitrary"),
        name="nsa_compressed_attn",
    )(u3, k_cmp, v_cmp, bc, agg)

    def kv(off):
        return pl.BlockSpec((None, s, dh), lambda b, g, i: (b, 0, off // dh + g))

    y = pl.pallas_call(
        _nsa_selwin_body,
        grid=(bsz, G, nqt),
        in_specs=[qspec,
                  pl.BlockSpec((None, TQ, HG * dh), lambda b, g, i: (b, i, g)),
                  pl.BlockSpec((None, None, TQ, n_sb), lambda b, g, i: (b, g, i, 0)),
                  pl.BlockSpec((None, TQ, LANES), lambda b, g, i: (b, i, U_MISC // LANES)),
                  kv(U_NKS), kv(U_NVS), kv(U_NKW), kv(U_NVW),
                  pl.BlockSpec((HG,) + tt.shape[1:], lambda b, g, i: (g, 0, 0, 0)),
                  pl.BlockSpec((HG,) + tw.shape[1:], lambda b, g, i: (g, 0, 0, 0))],
        out_specs=pl.BlockSpec((None, TQ, HG * dh), lambda b, g, i: (b, i, g)),
        out_shape=jax.ShapeDtypeStruct((bsz, s, BRANCH_WIDTH), BF16),
        scratch_shapes=[pltpu.VMEM((s, dh + n_sb), BF16), pltpu.VMEM((s, dh), BF16),
                        pltpu.VMEM((HG * TQ, 1), F32), pltpu.VMEM((HG * TQ, 1), F32),
                        pltpu.VMEM((HG * TQ, dh), F32), pltpu.VMEM((HG * TQ, NSA_SEL_TK), F32)],
        compiler_params=_cparams("parallel", "parallel", "arbitrary"),
        name="nsa_selected_window_attn",
    )(u3, o_cmp, sel, u3, u3, u3, u3, u3, tt, tw)
    return y


def _merge_body(x_ref, wg0, wg1, wg2, wg3, y0, y1, y2, y3, wb_ref, o_ref):
    x = x_ref[...]
    acc = None
    for b, (wg, y) in enumerate(((wg0, y0), (wg1, y1), (wg2, y2), (wg3, y3))):
        gate = _sigmoid(_dot(x, wg[...]))
        term = gate * _dot(y[...], wb_ref[b])
        acc = term if acc is None else acc + term
    o_ref[...] = acc.astype(o_ref.dtype)


def merge_branches(hb, w_gates, ys, w_branch):
    n, d = hb.shape
    tm, tn = min(1024, n), min(256, d)
    nj = d // tn

    def wg(b):
        return pl.BlockSpec((d, tn), lambda i, j: (0, b * nj + j))

    yspec = pl.BlockSpec((tm, BRANCH_WIDTH), lambda i, j: (i, 0), pipeline_mode=pl.Buffered(1))
    return pl.pallas_call(
        _merge_body,
        grid=(n // tm, nj),
        in_specs=[pl.BlockSpec((tm, d), lambda i, j: (i, 0), pipeline_mode=pl.Buffered(1)),
                  wg(0), wg(1), wg(2), wg(3),
                  yspec, yspec, yspec, yspec,
                  pl.BlockSpec((N_BRANCH, BRANCH_WIDTH, tn), lambda i, j: (0, 0, j))],
        out_specs=pl.BlockSpec((tm, tn), lambda i, j: (i, j)),
        out_shape=jax.ShapeDtypeStruct((n, d), BF16),
        compiler_params=_cparams("parallel", "arbitrary"),
        name="merge_branches",
    )(hb, w_gates, w_gates, w_gates, w_gates, *ys, w_branch)


MOE_TN = 512
EXPERTS_PER_TILE = MOE_TN // EXPERT_FF


def _router_body(x_ref, w_ref, b_ref, comb_ref, combt_ref):
    logits = _dot(x_ref[...], w_ref[...]) + b_ref[...]
    lane = lax.broadcasted_iota(jnp.int32, logits.shape, 1).astype(F32)
    work = logits
    picks, vals = [], []
    for _ in range(TOP_K):
        mx = jnp.max(work, axis=-1, keepdims=True)
        first = jnp.min(jnp.where(work == mx, lane, float(LANES)), axis=-1, keepdims=True)
        pick = lane == first
        picks.append(pick)
        vals.append(mx)
        work = jnp.where(pick, -jnp.inf, work)
    exps = [jnp.exp(v - vals[0]) for v in vals]
    den = exps[0]
    for e in exps[1:]:
        den = den + e
    comb = jnp.zeros_like(logits)
    for pick, e in zip(picks, exps):
        comb = comb + jnp.where(pick, e / den, 0.0)
    comb_ref[...] = comb
    for t in range(N_EXPERTS // EXPERTS_PER_TILE):
        combt_ref[t] = comb[:, t * EXPERTS_PER_TILE:(t + 1) * EXPERTS_PER_TILE]


def moe_router(hb, router_w, router_b):
    n, d = hb.shape
    tm = min(512, n)
    w = jnp.zeros((d, LANES), F32).at[:, :N_EXPERTS].set(router_w).astype(BF16)
    b = jnp.full((1, LANES), NEG, F32).at[0, :N_EXPERTS].set(router_b)
    nt = N_EXPERTS // EXPERTS_PER_TILE
    return pl.pallas_call(
        _router_body,
        grid=(n // tm,),
        in_specs=[pl.BlockSpec((tm, d), lambda i: (i, 0)), pl.BlockSpec((d, LANES), lambda i: (0, 0)),
                  pl.BlockSpec((1, LANES), lambda i: (0, 0))],
        out_specs=[pl.BlockSpec((tm, LANES), lambda i: (i, 0)),
                   pl.BlockSpec((nt, tm, EXPERTS_PER_TILE), lambda i: (0, i, 0))],
        out_shape=[jax.ShapeDtypeStruct((n, LANES), F32), jax.ShapeDtypeStruct((nt, n, EXPERTS_PER_TILE), F32)],
        compiler_params=_cparams("parallel"),
        name="moe_router",
    )(hb, w, b)


def _moe_up_body(x_ref, wg_ref, wu_ref, bg_ref, bu_ref, comb_ref, a_ref):
    x = x_ref[...]
    gate = jnp.minimum(_dot(x, wg_ref[...]) + bg_ref[...], SWIGLU_LIMIT)
    up = jnp.clip(_dot(x, wu_ref[...]) + bu_ref[...], -SWIGLU_LIMIT, SWIGLU_LIMIT)
    act = gate * _sigmoid(SWIGLU_ALPHA * gate) * (up + 1.0)
    for e in range(EXPERTS_PER_TILE):
        es = slice(e * EXPERT_FF, (e + 1) * EXPERT_FF)
        a_ref[:, es] = (act[:, es] * comb_ref[:, e:e + 1]).astype(BF16)


def moe_up(hb, w_gate, w_up, b_gate, b_up, comb_t):
    n, d = hb.shape
    tm = min(1024, n)
    width = N_EXPERTS * EXPERT_FF
    wspec = pl.BlockSpec((d, MOE_TN), lambda i, j: (0, j))
    bspec = pl.BlockSpec((1, MOE_TN), lambda i, j: (0, j))
    return pl.pallas_call(
        _moe_up_body,
        grid=(n // tm, width // MOE_TN),
        in_specs=[pl.BlockSpec((tm, d), lambda i, j: (i, 0)), wspec, wspec, bspec, bspec,
                  pl.BlockSpec((None, tm, EXPERTS_PER_TILE), lambda i, j: (j, i, 0))],
        out_specs=pl.BlockSpec((tm, MOE_TN), lambda i, j: (i, j)),
        out_shape=jax.ShapeDtypeStruct((n, width), BF16),
        compiler_params=_cparams("parallel", "arbitrary"),
        name="moe_up",
    )(hb, w_gate, w_up, b_gate, b_up, comb_t)


def _moe_down_body(a_ref, w_ref, comb_ref, b2_ref, o_ref):
    o_ref[...] = (_dot(a_ref[...], w_ref[...]) + _dot(comb_ref[...].astype(BF16), b2_ref[...])).astype(o_ref.dtype)


def moe_down(a, w2, comb, b2):
    n, k = a.shape
    d = w2.shape[1]
    tm, tn = min(1024, n), min(1024, d)
    return pl.pallas_call(
        _moe_down_body,
        grid=(n // tm, d // tn),
        in_specs=[pl.BlockSpec((tm, k), lambda i, j: (i, 0)), pl.BlockSpec((k, tn), lambda i, j: (0, j)),
                  pl.BlockSpec((tm, LANES), lambda i, j: (i, 0)), pl.BlockSpec((LANES, tn), lambda i, j: (0, j))],
        out_specs=pl.BlockSpec((tm, tn), lambda i, j: (i, j)),
        out_shape=jax.ShapeDtypeStruct((n, d), BF16),
        compiler_params=_cparams("parallel", "arbitrary"),
        name="moe_down",
    )(a, w2, comb, b2)


def _regroup_body(w_ref, u_ref, g_ref):
    o = _SRC_OFF
    segments = ((o[0], o[4]),
                (o[5], o[8]),
                (o[10], o[13]),
                (o[13], o[20]),
                (o[4], o[5]), (o[8], o[10]), (o[20], o[21]))
    pos = 0
    for a, b in segments:
        u_ref[:, pos:pos + (b - a)] = w_ref[:, a:b].astype(BF16)
        pos += b - a
    u_ref[:, pos:] = jnp.zeros((u_ref.shape[0], U_WIDTH - pos), BF16)
    g_ref[...] = w_ref[:, SRC_GATES:].astype(BF16)


def _regroup_w_in(w_in, l):
    _, d, width = w_in.shape
    rows = LANES
    return pl.pallas_call(
        _regroup_body,
        grid=(d // rows,),
        in_specs=[pl.BlockSpec((None, rows, width), lambda i: (l, i, 0))],
        out_specs=[pl.BlockSpec((rows, U_WIDTH), lambda i: (i, 0)),
                   pl.BlockSpec((rows, width - SRC_GATES), lambda i: (i, 0))],
        out_shape=[jax.ShapeDtypeStruct((d, U_WIDTH), BF16), jax.ShapeDtypeStruct((d, width - SRC_GATES), BF16)],
        compiler_params=_cparams("parallel"),
        name="regroup_w_in",
    )(w_in)


def kernel(x, rel_bias, w_in, gla_gate_w, gla_gate_b, gla_norm_g, mlstm_conv_w, mlstm_conv_b, mlstm_igate_b,
           mlstm_fgate_b, mlstm_norm_g, nsa_pe_k, nsa_pe_v, nsa_ck_w1, nsa_ck_w2, nsa_cv_w1, nsa_cv_w2, w_branch,
           w_out, ln1_g, ln1_b, router_w, router_b, exp_w1, exp_b1, exp_w2, exp_b2, ln2_g, ln2_b):
    out_dtype = x.dtype
    bsz, s, d = x.shape
    n = bsz * s
    h = x.astype(F32).reshape(n, d)
    hb = h.astype(BF16)
    thr = bucket_thresholds(max(s, BIAS_DIST_RANGE))
    tab_t = rel_bias.T
    dil_bias = dilated_bias(thr, tab_t)
    nsa_tt, nsa_tw, nsa_bc = nsa_bias(thr, tab_t, s)
    for l in range(w_in.shape[0]):
        w_u, w_gates = _regroup_w_in(w_in, l)
        u = matmul(hb, w_u, F32, 1024, 1024, "input_projection")
        u3 = u.reshape(bsz, s, U_WIDTH)
        y_a = gla_mixer(u3, gla_gate_w[l], gla_gate_b[l], gla_norm_g[l]).reshape(n, BRANCH_WIDTH)
        y_b = mlstm_mixer(u3, mlstm_conv_w[l], mlstm_conv_b[l], mlstm_igate_b[l], mlstm_fgate_b[l],
                          mlstm_norm_g[l]).reshape(n, BRANCH_WIDTH)
        y_c = dilated_mixer(u3, dil_bias).reshape(n, BRANCH_WIDTH)
        y_d = nsa_mixer(u3, nsa_tt, nsa_tw, nsa_bc, nsa_pe_k[l], nsa_pe_v[l], nsa_ck_w1[l], nsa_ck_w2[l], nsa_cv_w1[l],
                        nsa_cv_w2[l]).reshape(n, BRANCH_WIDTH)
        merged = merge_branches(hb, w_gates, (y_a, y_b, y_c, y_d), w_branch[l].astype(BF16))
        attn = matmul(merged, w_out[l].astype(BF16), BF16, 1024, 1024, "output_projection")
        h, hb = ln_residual(h, attn, ln1_g[l], ln1_b[l], "layer_norm_1")

        comb, comb_t = moe_router(hb, router_w[l], router_b[l])
        w1 = exp_w1[l]
        w_gate = w1[:, :, :EXPERT_FF].transpose(1, 0, 2).reshape(d, N_EXPERTS * EXPERT_FF).astype(BF16)
        w_up = w1[:, :, EXPERT_FF:].transpose(1, 0, 2).reshape(d, N_EXPERTS * EXPERT_FF).astype(BF16)
        b_gate = exp_b1[l][:, :EXPERT_FF].reshape(1, -1)
        b_up = exp_b1[l][:, EXPERT_FF:].reshape(1, -1)
        act = moe_up(hb, w_gate, w_up, b_gate, b_up, comb_t)
        b2 = jnp.zeros((LANES, d), F32).at[:N_EXPERTS].set(exp_b2[l]).astype(BF16)
        ffn = moe_down(act, exp_w2[l].reshape(N_EXPERTS * EXPERT_FF, d).astype(BF16), comb, b2)
        h, hb = ln_residual(h, ffn, ln2_g[l], ln2_b[l], "layer_norm_2")
    return h.reshape(bsz, s, d).astype(out_dtype)
```

```python
import functools
import math

import numpy as np
import jax
import jax.numpy as jnp
from jax import lax
from jax.experimental import pallas as pl
from jax.experimental.pallas import tpu as pltpu

F32 = jnp.float32
BF16 = jnp.bfloat16
HIGHEST = lax.Precision.HIGHEST

N_LAYERS_FOR_DEEPNORM = 4
HEAD_DIM = 128
BRANCH_WIDTH = 1024
N_BRANCH = 4
N_HEADS = 8
DK = 64
CHUNK = 64
GLA_GATE_RANK = 16
GLA_TAU = 16.0
MLSTM_CONV = 4
MLSTM_IGATE_CAP = 15.0
DIL_PATTERNS = ((128, 1), (512, 4), (2048, 16))
DIL_BLOCK = 128
NSA_KV_GROUPS = 2
NSA_HPG = 4
NSA_CMP_BLOCK = 32
NSA_CMP_STRIDE = 16
NSA_CMP_HIDDEN = 256
NSA_SEL_BLOCK = 64
NSA_N_SEL = 16
NSA_WINDOW = 512
NSA_SEL_TK = 512
REL_BUCKETS = 32
REL_MAX_DIST = 2048
N_EXPERTS = 32
TOP_K = 4
EXPERT_FF = 128
SWIGLU_LIMIT = 7.0
SWIGLU_ALPHA = 1.702
DEEPNORM_ALPHA = (2 * N_LAYERS_FOR_DEEPNORM) ** 0.25
LN_EPS = 1e-5
NEG = -1e30
BIG = 1e9

LANES = 128
VMEM_LIMIT_BYTES = 56 * 1024 * 1024

U_GQ, U_GK, U_GV, U_GR = 0, 512, 1024, 2048
U_MQK, U_MV, U_MO = 3072, 4096, 5120
U_DQ, U_DK, U_DV = 6144, 9216, 10240
U_NQ = 11264
U_NKC, U_NVC, U_NKS, U_NVS, U_NKW, U_NVW = 12288, 12544, 12800, 13056, 13312, 13568
U_MISC = 13824
U_WIDTH = 14336
MISC_GA, MISC_MI, MISC_MF, MISC_NG = 0, 16, 24, 32

_SRC_SIZES = (512, 512, 1024, 1024, 16, 1024, 1024, 1024, 8, 8, 3072, 1024, 1024, 1024,
              256, 256, 256, 256, 256, 256, 24)
_SRC_OFF = np.concatenate([[0], np.cumsum(_SRC_SIZES)]).tolist()
SRC_GATES = _SRC_OFF[-1]


def _cparams(*sem):
    return pltpu.CompilerParams(dimension_semantics=sem, vmem_limit_bytes=VMEM_LIMIT_BYTES)


def _log_sigmoid(x):
    return jnp.minimum(x, 0.0) - jnp.log1p(jnp.exp(-jnp.abs(x)))


def _sigmoid(x):
    return 1.0 / (1.0 + jnp.exp(-x))


def _silu(x):
    return x * _sigmoid(x)


def _dot(a, b):
    return jnp.dot(a, b, preferred_element_type=F32)


def _dot_t(a, b):
    return lax.dot_general(a, b, (((1,), (1,)), ((), ())), preferred_element_type=F32)


def _tdot(a, b):
    return lax.dot_general(a, b, (((0,), (0,)), ((), ())), preferred_element_type=F32)


def _head_norm(o, g_row):
    mu = jnp.mean(o, axis=-1, keepdims=True)
    d = o - mu
    var = jnp.mean(d * d, axis=-1, keepdims=True)
    return d * lax.rsqrt(var + LN_EPS) * g_row


def t5_bucket(dist):
    d = jnp.maximum(dist, 0)
    exact = REL_BUCKETS // 2
    df = jnp.maximum(d, 1).astype(jnp.float32)
    large = exact + (jnp.log(df / exact) / math.log(REL_MAX_DIST / exact) * (REL_BUCKETS - exact)).astype(jnp.int32)
    return jnp.where(d < exact, d, jnp.minimum(large, REL_BUCKETS - 1))


def _mm_body(x_ref, w_ref, o_ref):
    o_ref[...] = _dot(x_ref[...], w_ref[...]).astype(o_ref.dtype)


def matmul(x, w, out_dtype, tm, tn, name):
    m, k = x.shape
    n = w.shape[1]
    tm, tn = min(tm, m), min(tn, n)
    return pl.pallas_call(
        _mm_body,
        grid=(m // tm, n // tn),
        in_specs=[pl.BlockSpec((tm, k), lambda i, j: (i, 0)), pl.BlockSpec((k, tn), lambda i, j: (0, j))],
        out_specs=pl.BlockSpec((tm, tn), lambda i, j: (i, j)),
        out_shape=jax.ShapeDtypeStruct((m, n), out_dtype),
        compiler_params=_cparams("parallel", "arbitrary"),
        name=name,
    )(x, w)


def _ln_body(h_ref, d_ref, g_ref, b_ref, o_ref, ob_ref):
    z = DEEPNORM_ALPHA * h_ref[...] + d_ref[...]
    mu = jnp.mean(z, axis=-1, keepdims=True)
    zc = z - mu
    var = jnp.mean(zc * zc, axis=-1, keepdims=True)
    y = zc * lax.rsqrt(var + LN_EPS) * g_ref[...] + b_ref[...]
    o_ref[...] = y
    ob_ref[...] = y.astype(BF16)


def ln_residual(h, delta, g, b, name):
    n, d = h.shape
    tm = min(256, n)
    row = pl.BlockSpec((tm, d), lambda i: (i, 0))
    vec = pl.BlockSpec((1, d), lambda i: (0, 0))
    return pl.pallas_call(
        _ln_body,
        grid=(n // tm,),
        in_specs=[row, row, vec, vec],
        out_specs=[row, row],
        out_shape=[jax.ShapeDtypeStruct((n, d), F32), jax.ShapeDtypeStruct((n, d), BF16)],
        compiler_params=_cparams("parallel"),
        name=name,
    )(h, delta, g.reshape(1, d), b.reshape(1, d))


def _gla_body(q_ref, k_ref, v_ref, r_ref, misc_ref, gw_ref, gb_ref, ng_ref, o_ref, st_ref):
    L = CHUNK

    @pl.when(pl.program_id(1) == 0)
    def _():
        st_ref[...] = jnp.zeros_like(st_ref)

    pre = _dot(misc_ref[...].astype(BF16), gw_ref[...]) + gb_ref[...]
    log_a = _log_sigmoid(pre) / GLA_TAU
    row = lax.broadcasted_iota(jnp.int32, (L, L), 0)
    col = lax.broadcasted_iota(jnp.int32, (L, L), 1)
    causal = col <= row
    tri = jnp.where(causal, 1.0, 0.0).astype(F32)
    b = jnp.dot(tri, log_a, precision=HIGHEST, preferred_element_type=F32)
    b_last = b[L - 1:L, :]
    q_dec = (q_ref[...] * (DK ** -0.5) * jnp.exp(b)).astype(BF16)
    k_dec = (k_ref[...] * jnp.exp(-b)).astype(BF16)
    k_end = (k_ref[...] * jnp.exp(b_last - b)).astype(BF16)
    decay = jnp.exp(b_last)

    def heads(x, width):
        return jnp.stack([x[:, h * width:(h + 1) * width] for h in range(N_HEADS)], axis=0)

    q3, k3, ke3 = heads(q_dec, DK), heads(k_dec, DK), heads(k_end, DK)
    v3 = heads(v_ref[...].astype(BF16), HEAD_DIM)
    dec3 = heads(decay, DK)
    state_t = st_ref[...]
    att = jnp.where(causal[None], jnp.einsum('hqd,hkd->hqk', q3, k3, preferred_element_type=F32), 0.0)
    o3 = (jnp.einsum('hqk,hkv->hqv', att.astype(BF16), v3, preferred_element_type=F32)
          + jnp.einsum('hqd,hvd->hqv', q3, state_t.astype(BF16), preferred_element_type=F32))
    st_ref[...] = dec3 * state_t + jnp.einsum('hkv,hkd->hvd', v3, ke3, preferred_element_type=F32)
    on3 = _head_norm(o3, heads(ng_ref[...], HEAD_DIM))
    out3 = (_silu(heads(r_ref[...], HEAD_DIM)) * on3).astype(BF16)
    for h in range(N_HEADS):
        o_ref[:, h * HEAD_DIM:(h + 1) * HEAD_DIM] = out3[h]


def gla_mixer(u3, gate_w, gate_b, norm_g):
    bsz, s, _ = u3.shape
    L = CHUNK
    gw = jnp.zeros((LANES, N_HEADS * DK), F32).at[MISC_GA:MISC_GA + GLA_GATE_RANK].set(gate_w).astype(BF16)

    def ublock(width, off):
        return pl.BlockSpec((None, L, width), lambda b, t: (b, t, off // width))

    def const(shape):
        return pl.BlockSpec(shape, lambda b, t: (0,) * len(shape))

    return pl.pallas_call(
        _gla_body,
        grid=(bsz, s // L),
        in_specs=[ublock(512, U_GQ), ublock(512, U_GK), ublock(1024, U_GV), ublock(1024, U_GR),
                  ublock(LANES, U_MISC), const((LANES, 512)), const((1, 512)), const((1, 1024))],
        out_specs=pl.BlockSpec((None, L, 1024), lambda b, t: (b, t, 0)),
        out_shape=jax.ShapeDtypeStruct((bsz, s, BRANCH_WIDTH), BF16),
        scratch_shapes=[pltpu.VMEM((N_HEADS, HEAD_DIM, DK), F32)],
        compiler_params=_cparams("parallel", "arbitrary"),
        name="gla_mixer",
    )(u3, u3, u3, u3, u3, gw, gate_b.reshape(1, -1), norm_g.reshape(1, -1))


def _mlstm_body(qk_ref, v_ref, op_ref, misc_ref, cw_ref, cb_ref, ib_ref, fb_ref, ng_ref, o_ref,
                ext_ref, c_ref, n_ref, m_ref):
    L = CHUNK
    C2 = 2 * N_HEADS * DK

    @pl.when(pl.program_id(1) == 0)
    def _():
        ext_ref[0:8, :] = jnp.zeros((8, C2), F32)
        c_ref[...] = jnp.zeros_like(c_ref)
        n_ref[...] = jnp.zeros_like(n_ref)
        m_ref[...] = jnp.zeros_like(m_ref)

    x = qk_ref[...]
    ext_ref[8:8 + L, :] = x
    y = (cb_ref[...] + cw_ref[3:4, :] * x + cw_ref[2:3, :] * ext_ref[pl.ds(7, L), :]
         + cw_ref[1:2, :] * ext_ref[pl.ds(6, L), :] + cw_ref[0:1, :] * ext_ref[pl.ds(5, L), :])
    ext_ref[0:8, :] = x[L - 8:L, :]
    qk = _silu(y)
    q_all = qk[:, :N_HEADS * DK].astype(BF16)
    k_all = qk[:, N_HEADS * DK:] * (DK ** -0.5)

    misc = misc_ref[...]
    i_g = MLSTM_IGATE_CAP * jnp.tanh((misc + ib_ref[...]) / MLSTM_IGATE_CAP)
    log_f = _log_sigmoid(misc + fb_ref[...])
    row = lax.broadcasted_iota(jnp.int32, (L, L), 0)
    col = lax.broadcasted_iota(jnp.int32, (L, L), 1)
    causal = col <= row
    tri = jnp.where(causal, 1.0, 0.0).astype(F32)
    b = jnp.dot(tri, log_f, precision=HIGHEST, preferred_element_type=F32)
    it = pltpu.roll(i_g, MISC_MF - MISC_MI, 1)
    b_last = b[L - 1:L, :]
    m_s = m_ref[0:1, :]
    a_end = b_last - b + it
    m_new = jnp.maximum(b_last + m_s, jnp.max(a_end, axis=0, keepdims=True))
    w = jnp.exp(a_end - m_new)
    sc = jnp.exp(b_last + m_s - m_new)
    inter_log = b + m_s
    c = it - b
    rows_i = lax.broadcasted_iota(jnp.int32, c.shape, 0)
    cmax = c
    for sh in (1, 2, 4, 8, 16, 32):
        cmax = jnp.maximum(cmax, jnp.where(rows_i >= sh, pltpu.roll(cmax, sh, 0), -jnp.inf))
    m_i = jnp.maximum(inter_log, b + cmax)
    w_inter = jnp.exp(inter_log - m_i)
    e_neg = jnp.exp(-m_i)
    m_ref[0:1, :] = m_new

    stats = jnp.concatenate([b - m_i, w, w_inter, e_neg, jnp.broadcast_to(sc, (8, LANES))], axis=0)
    lane = lax.broadcasted_iota(jnp.int32, stats.shape, 1)
    stats = jnp.where((lane >= MISC_MF) & (lane < MISC_MF + N_HEADS), stats, 0.0)
    sel_r = lax.broadcasted_iota(jnp.int32, (LANES, N_HEADS * LANES), 0)
    sel_c = lax.broadcasted_iota(jnp.int32, (LANES, N_HEADS * LANES), 1) // LANES
    onehot = jnp.where(sel_r == sel_c + MISC_MF, 1.0, 0.0).astype(F32)
    spread = jnp.dot(stats, onehot, precision=HIGHEST, preferred_element_type=F32)

    def heads(x, width):
        return jnp.stack([x[:, h * width:(h + 1) * width] for h in range(N_HEADS)], axis=0)

    dcol3 = heads(spread[0:L], LANES)[:, :, 0:L]
    w3 = heads(spread[L:2 * L], LANES)[:, :, 0:DK]
    wi3 = heads(spread[2 * L:3 * L], LANES)
    en3 = heads(spread[3 * L:4 * L], LANES)
    sc3 = heads(spread[4 * L:4 * L + 1], LANES)
    c_t = c.T
    crow3 = jnp.stack([c_t[MISC_MF + h:MISC_MF + h + 1, :] for h in range(N_HEADS)], axis=0)

    q3 = heads(q_all, DK)
    k3 = heads(k_all, DK)
    v3 = heads(v_ref[...].astype(BF16), HEAD_DIM)
    c_s = c_ref[...]
    n_s = n_ref[...]
    decay3 = jnp.exp(jnp.where(causal[None], dcol3 + crow3, -jnp.inf))
    qk_s = jnp.einsum('hqd,hkd->hqk', q3, k3.astype(BF16), preferred_element_type=F32) * decay3
    num = (wi3 * jnp.einsum('hqd,hdv->hqv', q3, c_s.astype(BF16), preferred_element_type=F32)
           + jnp.einsum('hqk,hkv->hqv', qk_s.astype(BF16), v3, preferred_element_type=F32))
    qn = jnp.sum(q3.astype(F32) * n_s, axis=-1, keepdims=True)
    den = wi3[:, :, 0:1] * qn + jnp.sum(qk_s, axis=-1, keepdims=True)
    hout = num / jnp.maximum(jnp.abs(den), en3[:, :, 0:1])
    wk3 = w3 * k3
    c_ref[...] = sc3 * c_s + jnp.einsum('hkd,hkv->hdv', wk3.astype(BF16), v3, preferred_element_type=F32)
    n_ref[...] = sc3[:, :, 0:DK] * n_s + jnp.sum(wk3, axis=1, keepdims=True)
    hn = _head_norm(hout, heads(ng_ref[...], HEAD_DIM))
    out3 = (_sigmoid(heads(op_ref[...], HEAD_DIM)) * hn).astype(BF16)
    for h in range(N_HEADS):
        o_ref[:, h * HEAD_DIM:(h + 1) * HEAD_DIM] = out3[h]


def mlstm_mixer(u3, conv_w, conv_b, igate_b, fgate_b, norm_g):
    bsz, s, _ = u3.shape
    L = CHUNK
    ib = jnp.zeros((1, LANES), F32).at[0, MISC_MI:MISC_MI + N_HEADS].set(igate_b)
    fb = jnp.zeros((1, LANES), F32).at[0, MISC_MF:MISC_MF + N_HEADS].set(fgate_b)

    def ublock(width, off):
        return pl.BlockSpec((None, L, width), lambda b, t: (b, t, off // width))

    def const(shape):
        return pl.BlockSpec(shape, lambda b, t: (0,) * len(shape))

    return pl.pallas_call(
        _mlstm_body,
        grid=(bsz, s // L),
        in_specs=[ublock(1024, U_MQK), ublock(1024, U_MV), ublock(1024, U_MO), ublock(LANES, U_MISC),
                  const((MLSTM_CONV, 1024)), const((1, 1024)), const((1, LANES)), const((1, LANES)),
                  const((1, 1024))],
        out_specs=pl.BlockSpec((None, L, 1024), lambda b, t: (b, t, 0)),
        out_shape=jax.ShapeDtypeStruct((bsz, s, BRANCH_WIDTH), BF16),
        scratch_shapes=[pltpu.VMEM((L + 8, 1024), F32), pltpu.VMEM((N_HEADS, DK, HEAD_DIM), F32),
                        pltpu.VMEM((N_HEADS, 1, DK), F32), pltpu.VMEM((8, LANES), F32)],
        compiler_params=_cparams("parallel", "arbitrary"),
        name="mlstm_mixer",
    )(u3, u3, u3, u3, conv_w, conv_b.reshape(1, -1), ib, fb, norm_g.reshape(1, -1))


BIAS_DIST_RANGE = 4096


def bucket_thresholds(max_dist):
    lut = t5_bucket(jnp.arange(max_dist))
    return jnp.sum(lut[None, :] < jnp.arange(REL_BUCKETS)[:, None], axis=1).astype(jnp.int32)


def _bias_table_body(thr_ref, tab_ref, o_ref, *, kind):
    a = pl.program_id(0)
    c = pl.program_id(1)
    shape = o_ref.shape
    i = lax.broadcasted_iota(jnp.int32, shape, 0)
    j = lax.broadcasted_iota(jnp.int32, shape, 1)
    if kind == "dilated":
        dil = jnp.where(a == 0, DIL_PATTERNS[0][1], jnp.where(a == 1, DIL_PATTERNS[1][1], DIL_PATTERNS[2][1]))
        dist = (i + DIL_BLOCK - j) * dil
        head = a * N_HEADS + c
    elif kind in ("causal", "window"):
        dist = c * shape[0] + i - j
        head = 3 * N_HEADS + a
    else:
        dist = c * shape[0] + i - (j * NSA_CMP_STRIDE + NSA_CMP_BLOCK - 1)
        head = 3 * N_HEADS + a
    dist_c = jnp.maximum(dist, 0)
    acc = jnp.full(shape, tab_ref[head, 0], F32)
    for k in range(1, REL_BUCKETS):
        acc = jnp.where(dist_c >= thr_ref[k], tab_ref[head, k], acc)
    if kind in ("causal", "window"):
        limit = NSA_WINDOW if kind == "window" else 2 ** 30
        dist_v = jnp.where(c < pl.num_programs(1) - 1, dist, -1)
        acc = jnp.where((dist_v >= 0) & (dist_v < limit), acc, NEG)
    o_ref[...] = acc


def bias_table(thr, tab_t, kind, out_dims, block):
    smem = pl.BlockSpec(memory_space=pltpu.SMEM)
    return pl.pallas_call(
        functools.partial(_bias_table_body, kind=kind),
        grid=out_dims[:2] if kind != "compressed" else (out_dims[0], out_dims[1] // block[0]),
        in_specs=[smem, smem],
        out_specs=(pl.BlockSpec((None, None) + block, lambda a, c: (a, c, 0, 0)) if kind != "compressed"
                   else pl.BlockSpec((None,) + block, lambda a, c: (a, c, 0))),
        out_shape=jax.ShapeDtypeStruct(out_dims, F32),
        compiler_params=_cparams("parallel", "parallel"),
        name=f"bias_table_{kind}",
    )(thr, tab_t)


DIL_SPAN = 2048


def _dil_body(q0_ref, q1_ref, q2_ref, kp_ref, kc_ref, vp_ref, vc_ref, bias_ref, o_ref, m_scr, l_scr, acc_scr):
    P = DIL_BLOCK
    NB = DIL_SPAN // P
    q_refs = (q0_ref, q1_ref, q2_ref)
    row = lax.broadcasted_iota(jnp.int32, (P, P), 0)
    col = lax.broadcasted_iota(jnp.int32, (P, P), 1)
    mask_cur = jnp.where(col <= row, 0.0, NEG)
    mask_prev = jnp.where(col >= row, 0.0, NEG)
    no_prev = jnp.where(pl.program_id(1) > 0, 0, P)
    mask_prev_first = jnp.where((col - row) >= no_prev, 0.0, NEG)

    def rows(r, n, dil):
        return pl.ds(r, n, stride=dil) if dil > 1 else pl.ds(r, n)

    def stacked(ref, dil):
        per = DIL_SPAN // dil
        return jnp.concatenate([ref[rows(r, per, dil), :].reshape(per // P, P, HEAD_DIM) for r in range(dil)], axis=0)

    def stacked_prev(cur3, prev_ref, dil):
        nblk = NB // dil
        parts = []
        for r in range(dil):
            parts.append(prev_ref[rows(r + DIL_SPAN - dil * P, P, dil), :].reshape(1, P, HEAD_DIM))
            if nblk > 1:
                parts.append(cur3[r * nblk:(r + 1) * nblk - 1])
        return jnp.concatenate(parts, axis=0)

    for g, (_, dil) in enumerate(DIL_PATTERNS):
        nblk = NB // dil
        per = DIL_SPAN // dil
        q3 = (stacked(q_refs[g], dil) * (HEAD_DIM ** -0.5)).astype(BF16)
        kc3 = stacked(kc_ref, dil)
        vc3 = stacked(vc_ref, dil)
        kp3 = stacked_prev(kc3, kp_ref, dil).astype(BF16)
        vp3 = stacked_prev(vc3, vp_ref, dil).astype(BF16)
        kc3 = kc3.astype(BF16)
        vc3 = vc3.astype(BF16)
        bias_p = bias_ref[g, :, 0:P]
        bias_prev3 = jnp.concatenate(
            [(bias_p + (mask_prev_first if b % nblk == 0 else mask_prev)).reshape(1, P, P) for b in range(NB)], axis=0)
        s_p = jnp.einsum('nqd,nkd->nqk', q3, kp3, preferred_element_type=F32) + bias_prev3
        s_c = (jnp.einsum('nqd,nkd->nqk', q3, kc3, preferred_element_type=F32)
               + (bias_ref[g, :, P:2 * P] + mask_cur)[None])
        m = jnp.maximum(jnp.max(s_p, axis=-1, keepdims=True), jnp.max(s_c, axis=-1, keepdims=True))
        p_p = jnp.exp(s_p - m)
        p_c = jnp.exp(s_c - m)
        l = jnp.sum(p_p, axis=-1, keepdims=True) + jnp.sum(p_c, axis=-1, keepdims=True)
        acc = (jnp.einsum('nqk,nkd->nqd', p_p.astype(BF16), vp3, preferred_element_type=F32)
               + jnp.einsum('nqk,nkd->nqd', p_c.astype(BF16), vc3, preferred_element_type=F32))
        for r in range(dil):
            sl = rows(g * DIL_SPAN + r, per, dil)
            bs = slice(r * nblk, (r + 1) * nblk)
            m_scr[sl, :] = jnp.broadcast_to(m[bs].reshape(per, 1), (per, LANES))
            l_scr[sl, :] = jnp.broadcast_to(l[bs].reshape(per, 1), (per, LANES))
            acc_scr[sl, :] = acc[bs].reshape(per, HEAD_DIM)

    ms = [m_scr[g * DIL_SPAN:(g + 1) * DIL_SPAN, :] for g in range(len(DIL_PATTERNS))]
    m_all = jnp.maximum(jnp.maximum(ms[0], ms[1]), ms[2])
    num = jnp.zeros((DIL_SPAN, HEAD_DIM), F32)
    den = jnp.zeros((DIL_SPAN, LANES), F32)
    for g in range(len(DIL_PATTERNS)):
        w = jnp.exp(ms[g] - m_all)
        num = num + w * acc_scr[g * DIL_SPAN:(g + 1) * DIL_SPAN, :]
        den = den + w * l_scr[g * DIL_SPAN:(g + 1) * DIL_SPAN, :]
    o_ref[...] = (num / den).astype(BF16)


def dilated_bias(thr, tab_t):
    P = DIL_BLOCK
    return bias_table(thr, tab_t, "dilated", (len(DIL_PATTERNS), N_HEADS, P, 2 * P), (P, 2 * P))


def dilated_mixer(u3, bias):
    bsz, s, _ = u3.shape
    P = DIL_BLOCK
    assert s % DIL_SPAN == 0 and all(w // d == P and DIL_SPAN % (d * P) == 0 for w, d in DIL_PATTERNS)

    def ub(off, prev):
        def idx(b, t, h):
            return (b, jnp.maximum(t - 1, 0) if prev else t, off // HEAD_DIM + h)
        return pl.BlockSpec((None, DIL_SPAN, HEAD_DIM), idx)

    return pl.pallas_call(
        _dil_body,
        grid=(bsz, s // DIL_SPAN, N_HEADS),
        in_specs=[ub(U_DQ, False), ub(U_DQ + 1024, False), ub(U_DQ + 2048, False),
                  ub(U_DK, True), ub(U_DK, False), ub(U_DV, True), ub(U_DV, False),
                  pl.BlockSpec((len(DIL_PATTERNS), None, P, 2 * P), lambda b, t, h: (0, h, 0, 0))],
        out_specs=pl.BlockSpec((None, DIL_SPAN, HEAD_DIM), lambda b, t, h: (b, t, h)),
        out_shape=jax.ShapeDtypeStruct((bsz, s, BRANCH_WIDTH), BF16),
        scratch_shapes=[pltpu.VMEM((len(DIL_PATTERNS) * DIL_SPAN, LANES), F32),
                        pltpu.VMEM((len(DIL_PATTERNS) * DIL_SPAN, LANES), F32),
                        pltpu.VMEM((len(DIL_PATTERNS) * DIL_SPAN, HEAD_DIM), F32)],
        compiler_params=_cparams("parallel", "parallel", "arbitrary"),
        name="dilated_attn",
    )(u3, u3, u3, u3, u3, u3, u3, bias)


def _gelu_tanh(x):
    return 0.5 * x * (1.0 + jnp.tanh(math.sqrt(2.0 / math.pi) * (x + 0.044715 * (x * x * x))))


def _nsa_compress_body(t_ref, pe_ref, w1_ref, w2_ref, o_ref):
    nblk = t_ref.shape[0] // NSA_CMP_STRIDE
    half = NSA_CMP_STRIDE
    t1 = jnp.zeros((nblk, NSA_CMP_HIDDEN), F32)
    t2 = jnp.zeros((nblk, NSA_CMP_HIDDEN), F32)
    for p in range(half):
        xp = t_ref[pl.ds(p, nblk, stride=NSA_CMP_STRIDE), :]
        t1 = t1 + _dot((xp + pe_ref[p:p + 1, :]).astype(BF16), w1_ref[p * HEAD_DIM:(p + 1) * HEAD_DIM, :])
        t2 = t2 + _dot((xp + pe_ref[half + p:half + p + 1, :]).astype(BF16),
                       w1_ref[(half + p) * HEAD_DIM:(half + p + 1) * HEAD_DIM, :])
    hidden = t1 + pltpu.roll(t2, nblk - 1, 0)
    o_ref[...] = _dot(_gelu_tanh(hidden).astype(BF16), w2_ref[...])


def _nsa_compress(u3, off, pe, w1, w2, name):
    bsz, s, _ = u3.shape
    nblk = s // NSA_CMP_STRIDE
    G = NSA_KV_GROUPS
    return pl.pallas_call(
        _nsa_compress_body,
        grid=(bsz, G),
        in_specs=[pl.BlockSpec((None, s, HEAD_DIM), lambda b, g: (b, 0, off // HEAD_DIM + g)),
                  pl.BlockSpec((NSA_CMP_BLOCK, HEAD_DIM), lambda b, g: (0, 0)),
                  pl.BlockSpec((NSA_CMP_BLOCK * HEAD_DIM, NSA_CMP_HIDDEN), lambda b, g: (0, 0)),
                  pl.BlockSpec((NSA_CMP_HIDDEN, HEAD_DIM), lambda b, g: (0, 0))],
        out_specs=pl.BlockSpec((None, None, nblk, HEAD_DIM), lambda b, g: (b, g, 0, 0)),
        out_shape=jax.ShapeDtypeStruct((bsz, G, nblk, HEAD_DIM), F32),
        compiler_params=_cparams("parallel", "parallel"),
        name=name,
    )(u3, pe, w1.astype(BF16), w2.astype(BF16))


def _nsa_cmp_body(q_ref, kc_ref, vc_ref, bc_ref, agg_ref, oc_ref, sel_ref):
    TQ = q_ref.shape[0]
    ncb = kc_ref.shape[0]
    nsb = agg_ref.shape[1]
    t0 = pl.program_id(2) * TQ
    tpos = t0 + lax.broadcasted_iota(jnp.int32, (TQ, ncb), 0)
    ends = lax.broadcasted_iota(jnp.int32, (TQ, ncb), 1) * NSA_CMP_STRIDE + (NSA_CMP_BLOCK - 1)
    mask = ends <= tpos
    maskf = jnp.where(mask, 1.0, 0.0).astype(F32)
    kcb = kc_ref[...].astype(BF16)
    vcb = vc_ref[...].astype(BF16)
    psum = jnp.zeros((TQ, ncb), F32)
    for hg in range(NSA_HPG):
        hs = slice(hg * HEAD_DIM, (hg + 1) * HEAD_DIM)
        q = (q_ref[:, hs] * (HEAD_DIM ** -0.5)).astype(BF16)
        s = jnp.where(mask, _dot_t(q, kcb) + bc_ref[hg], NEG)
        p = jnp.exp(s - jnp.max(s, axis=-1, keepdims=True)) * maskf
        l = jnp.sum(p, axis=-1, keepdims=True)
        p = p / jnp.where(l > 0, l, 1.0)
        oc_ref[:, hs] = _dot(p.astype(BF16), vcb)
        psum = psum + p
    imp = jnp.dot(psum, agg_ref[...], precision=HIGHEST, preferred_element_type=F32)
    jblk = lax.broadcasted_iota(jnp.int32, (TQ, nsb), 1)
    tq = t0 + lax.broadcasted_iota(jnp.int32, (TQ, nsb), 0)
    cur = tq // NSA_SEL_BLOCK
    forced = (jblk == 0) | (jblk == cur) | (jblk == cur - 1)
    score = jnp.where(forced, BIG, jnp.where(jblk * NSA_SEL_BLOCK <= tq, imp, -BIG))
    jf = jblk.astype(F32)
    sel = jnp.zeros((TQ, nsb), F32)
    for _ in range(min(NSA_N_SEL, nsb)):
        mx = jnp.max(score, axis=-1, keepdims=True)
        first = jnp.min(jnp.where(score == mx, jf, float(nsb)), axis=-1, keepdims=True)
        pick = jf == first
        sel = jnp.where(pick, 1.0, sel)
        score = jnp.where(pick, -jnp.inf, score)
    sel_ref[...] = sel


def _sel_agg_matrix(n_cmp, n_sb):
    r = NSA_SEL_BLOCK // NSA_CMP_STRIDE
    c = NSA_CMP_BLOCK // NSA_CMP_STRIDE
    jj, aa, bb = np.meshgrid(np.arange(n_sb), np.arange(r), np.arange(c), indexing='ij')
    ii = r * jj + aa + bb - 1
    ok = (ii >= 0) & (ii < n_cmp)
    mat = np.zeros((n_cmp, n_sb), np.float32)
    np.add.at(mat, (ii[ok], jj[ok]), 1.0)
    return mat


def _nsa_selwin_body(q_ref, oc_ref, sel_ref, misc_ref, ks_ref, vs_ref, kw_ref, vw_ref, tt_ref, tw_ref, o_ref,
                     ksb_ref, vsb_ref, m_ref, l_ref, acc_ref, s_ref):
    TQ = q_ref.shape[0]
    TK = NSA_SEL_TK
    HG = NSA_HPG
    nsb = sel_ref.shape[1]
    sub = TK // TQ
    g = pl.program_id(1)
    qi = pl.program_id(2)

    @pl.when(qi == 0)
    def _():
        ksb_ref[:, 0:HEAD_DIM] = ks_ref[...].astype(BF16)
        blk = lax.broadcasted_iota(jnp.int32, (ksb_ref.shape[0], nsb), 0) // NSA_SEL_BLOCK
        ksb_ref[:, HEAD_DIM:HEAD_DIM + nsb] = jnp.where(
            blk == lax.broadcasted_iota(jnp.int32, (ksb_ref.shape[0], nsb), 1), 1.0, 0.0).astype(BF16)
        vsb_ref[...] = vs_ref[...].astype(BF16)

    q4 = jnp.concatenate([(q_ref[:, hg * HEAD_DIM:(hg + 1) * HEAD_DIM] * (HEAD_DIM ** -0.5)).astype(BF16)
                          for hg in range(HG)], axis=0)
    unselected = jnp.where(sel_ref[...] > 0.5, 0.0, NEG).astype(BF16)
    q4_aug = jnp.concatenate([q4, jnp.concatenate([unselected] * HG, axis=0)], axis=1)
    masked_tile = tt_ref.shape[1] - 1

    m_ref[...] = jnp.full_like(m_ref, 0.1 * NEG)
    l_ref[...] = jnp.zeros_like(l_ref)
    acc_ref[...] = jnp.zeros_like(acc_ref)

    last_tile = ksb_ref.shape[0] // TK - 1

    def logits(kj):
        k0 = pl.multiple_of(kj * TK, TK)

        def tile_of(w):
            d = qi - (kj * sub + w)
            return jnp.where(d >= 0, d, masked_tile)

        bias4 = jnp.concatenate([jnp.concatenate([tt_ref[hg, tile_of(w)] for w in range(sub)], axis=1)
                                 for hg in range(HG)], axis=0)
        return _dot_t(q4_aug, ksb_ref[pl.ds(k0, TK), :]) + bias4

    s_ref[...] = logits(0)

    def sel_step(kj, carry):
        s = s_ref[...]
        v_tile = vsb_ref[pl.ds(pl.multiple_of(kj * TK, TK), TK), :]
        m_old = m_ref[...]
        m_new = jnp.maximum(m_old, jnp.max(s, axis=-1, keepdims=True))
        alpha = jnp.exp(m_old - m_new)
        p = jnp.exp(s - m_new)
        l_ref[...] = alpha * l_ref[...] + jnp.sum(p, axis=-1, keepdims=True)
        acc_ref[...] = alpha * acc_ref[...] + _dot(p.astype(BF16), v_tile)
        m_ref[...] = m_new
        s_ref[...] = logits(jnp.minimum(kj + 1, last_tile))
        return carry

    lax.fori_loop(0, qi // sub + 1, sel_step, 0)

    nwin = NSA_WINDOW // TQ + 1
    first = jnp.maximum(qi - (nwin - 1), 0)
    w0 = pl.multiple_of(first * TQ, TQ)
    kw_tile = kw_ref[pl.ds(w0, nwin * TQ), :].astype(BF16)
    vw_tile = vw_ref[pl.ds(w0, nwin * TQ), :].astype(BF16)

    def wtile_of(w):
        d = qi - (first + w)
        return jnp.where(d >= 0, d, tw_ref.shape[1] - 1)

    wbias4 = jnp.concatenate([jnp.concatenate([tw_ref[hg, wtile_of(w)] for w in range(nwin)], axis=1)
                              for hg in range(HG)], axis=0)
    s = _dot_t(q4, kw_tile) + wbias4
    p = jnp.exp(s - jnp.max(s, axis=-1, keepdims=True))
    o_win = _dot(p.astype(BF16), vw_tile) / jnp.sum(p, axis=-1, keepdims=True)
    o_sel = acc_ref[...] / l_ref[...]

    misc = misc_ref[...]
    gpre = jnp.where(g == 0, misc[:, MISC_NG:MISC_NG + 3 * HG], misc[:, MISC_NG + 3 * HG:MISC_NG + 6 * HG])
    gates = _sigmoid(gpre)
    for hg in range(HG):
        hs = slice(hg * HEAD_DIM, (hg + 1) * HEAD_DIM)
        rs = slice(hg * TQ, (hg + 1) * TQ)
        y = (gates[:, 3 * hg:3 * hg + 1] * oc_ref[:, hs] + gates[:, 3 * hg + 1:3 * hg + 2] * o_sel[rs]
             + gates[:, 3 * hg + 2:3 * hg + 3] * o_win[rs])
        o_ref[:, hs] = y.astype(BF16)


NSA_TQ = 128
NSA_CMP_TQ = 512


def nsa_bias(thr, tab_t, s):
    nh = NSA_KV_GROUPS * NSA_HPG
    tile = (NSA_TQ, NSA_TQ)
    tt = bias_table(thr, tab_t, "causal", (nh, s // NSA_TQ + 1) + tile, tile)
    tw = bias_table(thr, tab_t, "window", (nh, NSA_WINDOW // NSA_TQ + 2) + tile, tile)
    bc = bias_table(thr, tab_t, "compressed", (nh, s, s // NSA_CMP_STRIDE), (NSA_TQ, s // NSA_CMP_STRIDE))
    return tt, tw, bc


def nsa_mixer(u3, tt, tw, bc, pe_k, pe_v, ck_w1, ck_w2, cv_w1, cv_w2):
    bsz, s, _ = u3.shape
    G, HG, dh = NSA_KV_GROUPS, NSA_HPG, HEAD_DIM
    TQ = NSA_TQ
    nqt = s // TQ
    assert NSA_KV_GROUPS == 2 and s % NSA_SEL_TK == 0 and s >= NSA_WINDOW + TQ
    nblk = s // NSA_CMP_STRIDE
    n_cmp = (s - NSA_CMP_BLOCK) // NSA_CMP_STRIDE + 1
    n_sb = s // NSA_SEL_BLOCK
    agg = np.zeros((nblk, n_sb), np.float32)
    agg[:n_cmp] = _sel_agg_matrix(n_cmp, n_sb)
    agg = jnp.asarray(agg)

    k_cmp = _nsa_compress(u3, U_NKC, pe_k, ck_w1, ck_w2, "nsa_compress_k")
    v_cmp = _nsa_compress(u3, U_NVC, pe_v, cv_w1, cv_w2, "nsa_compress_v")

    qspec = pl.BlockSpec((None, TQ, HG * dh), lambda b, g, i: (b, i, U_NQ // (HG * dh) + g))
    TC = NSA_CMP_TQ
    o_cmp, sel = pl.pallas_call(
        _nsa_cmp_body,
        grid=(bsz, G, s // TC),
        in_specs=[pl.BlockSpec((None, TC, HG * dh), lambda b, g, i: (b, i, U_NQ // (HG * dh) + g)),
                  pl.BlockSpec((None, None, nblk, dh), lambda b, g, i: (b, g, 0, 0)),
                  pl.BlockSpec((None, None, nblk, dh), lambda b, g, i: (b, g, 0, 0)),
                  pl.BlockSpec((HG, TC, nblk), lambda b, g, i: (g, i, 0)),
                  pl.BlockSpec((nblk, n_sb), lambda b, g, i: (0, 0))],
        out_specs=[pl.BlockSpec((None, TC, HG * dh), lambda b, g, i: (b, i, g)),
                   pl.BlockSpec((None, None, TC, n_sb), lambda b, g, i: (b, g, i, 0))],
        out_shape=[jax.ShapeDtypeStruct((bsz, s, G * HG * dh), F32),
                   jax.ShapeDtypeStruct((bsz, G, s, n_sb), F32)],
        compiler_params=_cparams("parallel", "parallel", "arbitrary"),
        name="nsa_compressed_attn",
    )(u3, k_cmp, v_cmp, bc, agg)

    def kv(off):
        return pl.BlockSpec((None, s, dh), lambda b, g, i: (b, 0, off // dh + g))

    y = pl.pallas_call(
        _nsa_selwin_body,
        grid=(bsz, G, nqt),
        in_specs=[qspec,
                  pl.BlockSpec((None, TQ, HG * dh), lambda b, g, i: (b, i, g)),
                  pl.BlockSpec((None, None, TQ, n_sb), lambda b, g, i: (b, g, i, 0)),
                  pl.BlockSpec((None, TQ, LANES), lambda b, g, i: (b, i, U_MISC // LANES)),
                  kv(U_NKS), kv(U_NVS), kv(U_NKW), kv(U_NVW),
                  pl.BlockSpec((HG,) + tt.shape[1:], lambda b, g, i: (g, 0, 0, 0)),
                  pl.BlockSpec((HG,) + tw.shape[1:], lambda b, g, i: (g, 0, 0, 0))],
        out_specs=pl.BlockSpec((None, TQ, HG * dh), lambda b, g, i: (b, i, g)),
        out_shape=jax.ShapeDtypeStruct((bsz, s, BRANCH_WIDTH), BF16),
        scratch_shapes=[pltpu.VMEM((s, dh + n_sb), BF16), pltpu.VMEM((s, dh), BF16),
                        pltpu.VMEM((HG * TQ, 1), F32), pltpu.VMEM((HG * TQ, 1), F32),
                        pltpu.VMEM((HG * TQ, dh), F32), pltpu.VMEM((HG * TQ, NSA_SEL_TK), F32)],
        compiler_params=_cparams("parallel", "parallel", "arbitrary"),
        name="nsa_selected_window_attn",
    )(u3, o_cmp, sel, u3, u3, u3, u3, u3, tt, tw)
    return y


def _merge_body(x_ref, wg0, wg1, wg2, wg3, y0, y1, y2, y3, wb_ref, o_ref):
    x = x_ref[...]
    acc = None
    for b, (wg, y) in enumerate(((wg0, y0), (wg1, y1), (wg2, y2), (wg3, y3))):
        gate = _sigmoid(_dot(x, wg[...]))
        term = gate * _dot(y[...], wb_ref[b])
        acc = term if acc is None else acc + term
    o_ref[...] = acc.astype(o_ref.dtype)


def merge_branches(hb, w_gates, ys, w_branch):
    n, d = hb.shape
    tm, tn = min(1024, n), min(256, d)
    nj = d // tn

    def wg(b):
        return pl.BlockSpec((d, tn), lambda i, j: (0, b * nj + j))

    yspec = pl.BlockSpec((tm, BRANCH_WIDTH), lambda i, j: (i, 0), pipeline_mode=pl.Buffered(1))
    return pl.pallas_call(
        _merge_body,
        grid=(n // tm, nj),
        in_specs=[pl.BlockSpec((tm, d), lambda i, j: (i, 0), pipeline_mode=pl.Buffered(1)),
                  wg(0), wg(1), wg(2), wg(3),
                  yspec, yspec, yspec, yspec,
                  pl.BlockSpec((N_BRANCH, BRANCH_WIDTH, tn), lambda i, j: (0, 0, j))],
        out_specs=pl.BlockSpec((tm, tn), lambda i, j: (i, j)),
        out_shape=jax.ShapeDtypeStruct((n, d), BF16),
        compiler_params=_cparams("parallel", "arbitrary"),
        name="merge_branches",
    )(hb, w_gates, w_gates, w_gates, w_gates, *ys, w_branch)


MOE_TN = 512
EXPERTS_PER_TILE = MOE_TN // EXPERT_FF


def _router_body(x_ref, w_ref, b_ref, comb_ref, combt_ref):
    logits = _dot(x_ref[...], w_ref[...]) + b_ref[...]
    lane = lax.broadcasted_iota(jnp.int32, logits.shape, 1).astype(F32)
    work = logits
    picks, vals = [], []
    for _ in range(TOP_K):
        mx = jnp.max(work, axis=-1, keepdims=True)
        first = jnp.min(jnp.where(work == mx, lane, float(LANES)), axis=-1, keepdims=True)
        pick = lane == first
        picks.append(pick)
        vals.append(mx)
        work = jnp.where(pick, -jnp.inf, work)
    exps = [jnp.exp(v - vals[0]) for v in vals]
    den = exps[0]
    for e in exps[1:]:
        den = den + e
    comb = jnp.zeros_like(logits)
    for pick, e in zip(picks, exps):
        comb = comb + jnp.where(pick, e / den, 0.0)
    comb_ref[...] = comb
    for t in range(N_EXPERTS // EXPERTS_PER_TILE):
        combt_ref[t] = comb[:, t * EXPERTS_PER_TILE:(t + 1) * EXPERTS_PER_TILE]


def moe_router(hb, router_w, router_b):
    n, d = hb.shape
    tm = min(512, n)
    w = jnp.zeros((d, LANES), F32).at[:, :N_EXPERTS].set(router_w).astype(BF16)
    b = jnp.full((1, LANES), NEG, F32).at[0, :N_EXPERTS].set(router_b)
    nt = N_EXPERTS // EXPERTS_PER_TILE
    return pl.pallas_call(
        _router_body,
        grid=(n // tm,),
        in_specs=[pl.BlockSpec((tm, d), lambda i: (i, 0)), pl.BlockSpec((d, LANES), lambda i: (0, 0)),
                  pl.BlockSpec((1, LANES), lambda i: (0, 0))],
        out_specs=[pl.BlockSpec((tm, LANES), lambda i: (i, 0)),
                   pl.BlockSpec((nt, tm, EXPERTS_PER_TILE), lambda i: (0, i, 0))],
        out_shape=[jax.ShapeDtypeStruct((n, LANES), F32), jax.ShapeDtypeStruct((nt, n, EXPERTS_PER_TILE), F32)],
        compiler_params=_cparams("parallel"),
        name="moe_router",
    )(hb, w, b)


def _moe_up_body(x_ref, wg_ref, wu_ref, bg_ref, bu_ref, comb_ref, a_ref):
    x = x_ref[...]
    gate = jnp.minimum(_dot(x, wg_ref[...]) + bg_ref[...], SWIGLU_LIMIT)
    up = jnp.clip(_dot(x, wu_ref[...]) + bu_ref[...], -SWIGLU_LIMIT, SWIGLU_LIMIT)
    act = gate * _sigmoid(SWIGLU_ALPHA * gate) * (up + 1.0)
    for e in range(EXPERTS_PER_TILE):
        es = slice(e * EXPERT_FF, (e + 1) * EXPERT_FF)
        a_ref[:, es] = (act[:, es] * comb_ref[:, e:e + 1]).astype(BF16)


def moe_up(hb, w_gate, w_up, b_gate, b_up, comb_t):
    n, d = hb.shape
    tm = min(1024, n)
    width = N_EXPERTS * EXPERT_FF
    wspec = pl.BlockSpec((d, MOE_TN), lambda i, j: (0, j))
    bspec = pl.BlockSpec((1, MOE_TN), lambda i, j: (0, j))
    return pl.pallas_call(
        _moe_up_body,
        grid=(n // tm, width // MOE_TN),
        in_specs=[pl.BlockSpec((tm, d), lambda i, j: (i, 0)), wspec, wspec, bspec, bspec,
                  pl.BlockSpec((None, tm, EXPERTS_PER_TILE), lambda i, j: (j, i, 0))],
        out_specs=pl.BlockSpec((tm, MOE_TN), lambda i, j: (i, j)),
        out_shape=jax.ShapeDtypeStruct((n, width), BF16),
        compiler_params=_cparams("parallel", "arbitrary"),
        name="moe_up",
    )(hb, w_gate, w_up, b_gate, b_up, comb_t)


def _moe_down_body(a_ref, w_ref, comb_ref, b2_ref, o_ref):
    o_ref[...] = (_dot(a_ref[...], w_ref[...]) + _dot(comb_ref[...].astype(BF16), b2_ref[...])).astype(o_ref.dtype)


def moe_down(a, w2, comb, b2):
    n, k = a.shape
    d = w2.shape[1]
    tm, tn = min(1024, n), min(1024, d)
    return pl.pallas_call(
        _moe_down_body,
        grid=(n // tm, d // tn),
        in_specs=[pl.BlockSpec((tm, k), lambda i, j: (i, 0)), pl.BlockSpec((k, tn), lambda i, j: (0, j)),
                  pl.BlockSpec((tm, LANES), lambda i, j: (i, 0)), pl.BlockSpec((LANES, tn), lambda i, j: (0, j))],
        out_specs=pl.BlockSpec((tm, tn), lambda i, j: (i, j)),
        out_shape=jax.ShapeDtypeStruct((n, d), BF16),
        compiler_params=_cparams("parallel", "arbitrary"),
        name="moe_down",
    )(a, w2, comb, b2)


def _regroup_body(w_ref, u_ref, g_ref):
    o = _SRC_OFF
    segments = ((o[0], o[4]),
                (o[5], o[8]),
                (o[10], o[13]),
                (o[13], o[20]),
                (o[4], o[5]), (o[8], o[10]), (o[20], o[21]))
    pos = 0
    for a, b in segments:
        u_ref[:, pos:pos + (b - a)] = w_ref[:, a:b]
        pos += b - a
    u_ref[:, pos:] = jnp.zeros((u_ref.shape[0], U_WIDTH - pos), BF16)
    g_ref[...] = w_ref[:, SRC_GATES:]


def _regroup_w_in(w):
    d, width = w.shape
    rows = LANES
    return pl.pallas_call(
        _regroup_body,
        grid=(d // rows,),
        in_specs=[pl.BlockSpec((rows, width), lambda i: (i, 0))],
        out_specs=[pl.BlockSpec((rows, U_WIDTH), lambda i: (i, 0)),
                   pl.BlockSpec((rows, width - SRC_GATES), lambda i: (i, 0))],
        out_shape=[jax.ShapeDtypeStruct((d, U_WIDTH), BF16), jax.ShapeDtypeStruct((d, width - SRC_GATES), BF16)],
        compiler_params=_cparams("parallel"),
        name="regroup_w_in",
    )(w)


def kernel(x, rel_bias, w_in, gla_gate_w, gla_gate_b, gla_norm_g, mlstm_conv_w, mlstm_conv_b, mlstm_igate_b,
           mlstm_fgate_b, mlstm_norm_g, nsa_pe_k, nsa_pe_v, nsa_ck_w1, nsa_ck_w2, nsa_cv_w1, nsa_cv_w2, w_branch,
           w_out, ln1_g, ln1_b, router_w, router_b, exp_w1, exp_b1, exp_w2, exp_b2, ln2_g, ln2_b):
    out_dtype = x.dtype
    bsz, s, d = x.shape
    n = bsz * s
    h = x.astype(F32).reshape(n, d)
    hb = h.astype(BF16)
    thr = bucket_thresholds(max(s, BIAS_DIST_RANGE))
    tab_t = rel_bias.T
    dil_bias = dilated_bias(thr, tab_t)
    nsa_tt, nsa_tw, nsa_bc = nsa_bias(thr, tab_t, s)
    for l in range(w_in.shape[0]):
        w_u, w_gates = _regroup_w_in(w_in[l].astype(BF16))
        u = matmul(hb, w_u, F32, 1024, 1024, "input_projection")
        u3 = u.reshape(bsz, s, U_WIDTH)
        y_a = gla_mixer(u3, gla_gate_w[l], gla_gate_b[l], gla_norm_g[l]).reshape(n, BRANCH_WIDTH)
        y_b = mlstm_mixer(u3, mlstm_conv_w[l], mlstm_conv_b[l], mlstm_igate_b[l], mlstm_fgate_b[l],
                          mlstm_norm_g[l]).reshape(n, BRANCH_WIDTH)
        y_c = dilated_mixer(u3, dil_bias).reshape(n, BRANCH_WIDTH)
        y_d = nsa_mixer(u3, nsa_tt, nsa_tw, nsa_bc, nsa_pe_k[l], nsa_pe_v[l], nsa_ck_w1[l], nsa_ck_w2[l], nsa_cv_w1[l],
                        nsa_cv_w2[l]).reshape(n, BRANCH_WIDTH)
        merged = merge_branches(hb, w_gates, (y_a, y_b, y_c, y_d), w_branch[l].astype(BF16))
        attn = matmul(merged, w_out[l].astype(BF16), BF16, 1024, 1024, "output_projection")
        h, hb = ln_residual(h, attn, ln1_g[l], ln1_b[l], "layer_norm_1")

        comb, comb_t = moe_router(hb, router_w[l], router_b[l])
        w1 = exp_w1[l]
        w_gate = w1[:, :, :EXPERT_FF].transpose(1, 0, 2).reshape(d, N_EXPERTS * EXPERT_FF).astype(BF16)
        w_up = w1[:, :, EXPERT_FF:].transpose(1, 0, 2).reshape(d, N_EXPERTS * EXPERT_FF).astype(BF16)
        b_gate = exp_b1[l][:, :EXPERT_FF].reshape(1, -1)
        b_up = exp_b1[l][:, EXPERT_FF:].reshape(1, -1)
        act = moe_up(hb, w_gate, w_up, b_gate, b_up, comb_t)
        b2 = jnp.zeros((LANES, d), F32).at[:N_EXPERTS].set(exp_b2[l]).astype(BF16)
        ffn = moe_down(act, exp_w2[l].reshape(N_EXPERTS * EXPERT_FF, d).astype(BF16), comb, b2)
        h, hb = ln_residual(h, ffn, ln2_g[l], ln2_b[l], "layer_norm_2")
    return h.reshape(bsz, s, d).astype(out_dtype)
```

```python
import functools
import math

import numpy as np
import jax
import jax.numpy as jnp
from jax import lax
from jax.experimental import pallas as pl
from jax.experimental.pallas import tpu as pltpu

F32 = jnp.float32
BF16 = jnp.bfloat16
HIGHEST = lax.Precision.HIGHEST

N_LAYERS_FOR_DEEPNORM = 4
HEAD_DIM = 128
BRANCH_WIDTH = 1024
N_BRANCH = 4
N_HEADS = 8
DK = 64
CHUNK = 64
GLA_GATE_RANK = 16
GLA_TAU = 16.0
MLSTM_CONV = 4
MLSTM_IGATE_CAP = 15.0
DIL_PATTERNS = ((128, 1), (512, 4), (2048, 16))
DIL_BLOCK = 128
NSA_KV_GROUPS = 2
NSA_HPG = 4
NSA_CMP_BLOCK = 32
NSA_CMP_STRIDE = 16
NSA_CMP_HIDDEN = 256
NSA_SEL_BLOCK = 64
NSA_N_SEL = 16
NSA_WINDOW = 512
NSA_SEL_TK = 512
REL_BUCKETS = 32
REL_MAX_DIST = 2048
N_EXPERTS = 32
TOP_K = 4
EXPERT_FF = 128
SWIGLU_LIMIT = 7.0
SWIGLU_ALPHA = 1.702
DEEPNORM_ALPHA = (2 * N_LAYERS_FOR_DEEPNORM) ** 0.25
LN_EPS = 1e-5
NEG = -1e30
BIG = 1e9

LANES = 128
VMEM_LIMIT_BYTES = 56 * 1024 * 1024

U_GQ, U_GK, U_GV, U_GR = 0, 512, 1024, 2048
U_MQK, U_MV, U_MO = 3072, 4096, 5120
U_DQ, U_DK, U_DV = 6144, 9216, 10240
U_NQ = 11264
U_NKC, U_NVC, U_NKS, U_NVS, U_NKW, U_NVW = 12288, 12544, 12800, 13056, 13312, 13568
U_MISC = 13824
U_WIDTH = 14336
MISC_GA, MISC_MI, MISC_MF, MISC_NG = 0, 16, 24, 32

_SRC_SIZES = (512, 512, 1024, 1024, 16, 1024, 1024, 1024, 8, 8, 3072, 1024, 1024, 1024,
              256, 256, 256, 256, 256, 256, 24)
_SRC_OFF = np.concatenate([[0], np.cumsum(_SRC_SIZES)]).tolist()
SRC_GATES = _SRC_OFF[-1]


def _cparams(*sem):
    return pltpu.CompilerParams(dimension_semantics=sem, vmem_limit_bytes=VMEM_LIMIT_BYTES)


def _log_sigmoid(x):
    return jnp.minimum(x, 0.0) - jnp.log1p(jnp.exp(-jnp.abs(x)))


def _sigmoid(x):
    return 1.0 / (1.0 + jnp.exp(-x))


def _silu(x):
    return x * _sigmoid(x)


def _dot(a, b):
    return jnp.dot(a, b, preferred_element_type=F32)


def _dot_t(a, b):
    return lax.dot_general(a, b, (((1,), (1,)), ((), ())), preferred_element_type=F32)


def _tdot(a, b):
    return lax.dot_general(a, b, (((0,), (0,)), ((), ())), preferred_element_type=F32)


def _head_norm(o, g_row):
    mu = jnp.mean(o, axis=-1, keepdims=True)
    d = o - mu
    var = jnp.mean(d * d, axis=-1, keepdims=True)
    return d * lax.rsqrt(var + LN_EPS) * g_row


def t5_bucket(dist):
    d = jnp.maximum(dist, 0)
    exact = REL_BUCKETS // 2
    df = jnp.maximum(d, 1).astype(jnp.float32)
    large = exact + (jnp.log(df / exact) / math.log(REL_MAX_DIST / exact) * (REL_BUCKETS - exact)).astype(jnp.int32)
    return jnp.where(d < exact, d, jnp.minimum(large, REL_BUCKETS - 1))


def _mm_body(x_ref, w_ref, o_ref):
    o_ref[...] = _dot(x_ref[...], w_ref[...]).astype(o_ref.dtype)


def matmul(x, w, out_dtype, tm, tn, name):
    m, k = x.shape
    n = w.shape[1]
    tm, tn = min(tm, m), min(tn, n)
    return pl.pallas_call(
        _mm_body,
        grid=(m // tm, n // tn),
        in_specs=[pl.BlockSpec((tm, k), lambda i, j: (i, 0)), pl.BlockSpec((k, tn), lambda i, j: (0, j))],
        out_specs=pl.BlockSpec((tm, tn), lambda i, j: (i, j)),
        out_shape=jax.ShapeDtypeStruct((m, n), out_dtype),
        compiler_params=_cparams("parallel", "arbitrary"),
        name=name,
    )(x, w)


def _ln_body(h_ref, d_ref, g_ref, b_ref, o_ref, ob_ref):
    z = DEEPNORM_ALPHA * h_ref[...] + d_ref[...]
    mu = jnp.mean(z, axis=-1, keepdims=True)
    zc = z - mu
    var = jnp.mean(zc * zc, axis=-1, keepdims=True)
    y = zc * lax.rsqrt(var + LN_EPS) * g_ref[...] + b_ref[...]
    o_ref[...] = y
    ob_ref[...] = y.astype(BF16)


def ln_residual(h, delta, g, b, name):
    n, d = h.shape
    tm = min(256, n)
    row = pl.BlockSpec((tm, d), lambda i: (i, 0))
    vec = pl.BlockSpec((1, d), lambda i: (0, 0))
    return pl.pallas_call(
        _ln_body,
        grid=(n // tm,),
        in_specs=[row, row, vec, vec],
        out_specs=[row, row],
        out_shape=[jax.ShapeDtypeStruct((n, d), F32), jax.ShapeDtypeStruct((n, d), BF16)],
        compiler_params=_cparams("parallel"),
        name=name,
    )(h, delta, g.reshape(1, d), b.reshape(1, d))


def _gla_body(q_ref, k_ref, v_ref, r_ref, misc_ref, gw_ref, gb_ref, ng_ref, o_ref, st_ref):
    L = CHUNK

    @pl.when(pl.program_id(1) == 0)
    def _():
        st_ref[...] = jnp.zeros_like(st_ref)

    pre = _dot(misc_ref[...].astype(BF16), gw_ref[...]) + gb_ref[...]
    log_a = _log_sigmoid(pre) / GLA_TAU
    row = lax.broadcasted_iota(jnp.int32, (L, L), 0)
    col = lax.broadcasted_iota(jnp.int32, (L, L), 1)
    causal = col <= row
    tri = jnp.where(causal, 1.0, 0.0).astype(F32)
    b = jnp.dot(tri, log_a, precision=HIGHEST, preferred_element_type=F32)
    b_last = b[L - 1:L, :]
    q_dec = (q_ref[...] * (DK ** -0.5) * jnp.exp(b)).astype(BF16)
    k_dec = (k_ref[...] * jnp.exp(-b)).astype(BF16)
    k_end = (k_ref[...] * jnp.exp(b_last - b)).astype(BF16)
    decay = jnp.exp(b_last)

    def heads(x, width):
        return jnp.stack([x[:, h * width:(h + 1) * width] for h in range(N_HEADS)], axis=0)

    q3, k3, ke3 = heads(q_dec, DK), heads(k_dec, DK), heads(k_end, DK)
    v3 = heads(v_ref[...].astype(BF16), HEAD_DIM)
    dec3 = heads(decay, DK)
    state_t = st_ref[...]
    att = jnp.where(causal[None], jnp.einsum('hqd,hkd->hqk', q3, k3, preferred_element_type=F32), 0.0)
    o3 = (jnp.einsum('hqk,hkv->hqv', att.astype(BF16), v3, preferred_element_type=F32)
          + jnp.einsum('hqd,hvd->hqv', q3, state_t.astype(BF16), preferred_element_type=F32))
    st_ref[...] = dec3 * state_t + jnp.einsum('hkv,hkd->hvd', v3, ke3, preferred_element_type=F32)
    on3 = _head_norm(o3, heads(ng_ref[...], HEAD_DIM))
    out3 = (_silu(heads(r_ref[...], HEAD_DIM)) * on3).astype(BF16)
    for h in range(N_HEADS):
        o_ref[:, h * HEAD_DIM:(h + 1) * HEAD_DIM] = out3[h]


def gla_mixer(u3, gate_w, gate_b, norm_g):
    bsz, s, _ = u3.shape
    L = CHUNK
    gw = jnp.zeros((LANES, N_HEADS * DK), F32).at[MISC_GA:MISC_GA + GLA_GATE_RANK].set(gate_w).astype(BF16)

    def ublock(width, off):
        return pl.BlockSpec((None, L, width), lambda b, t: (b, t, off // width))

    def const(shape):
        return pl.BlockSpec(shape, lambda b, t: (0,) * len(shape))

    return pl.pallas_call(
        _gla_body,
        grid=(bsz, s // L),
        in_specs=[ublock(512, U_GQ), ublock(512, U_GK), ublock(1024, U_GV), ublock(1024, U_GR),
                  ublock(LANES, U_MISC), const((LANES, 512)), const((1, 512)), const((1, 1024))],
        out_specs=pl.BlockSpec((None, L, 1024), lambda b, t: (b, t, 0)),
        out_shape=jax.ShapeDtypeStruct((bsz, s, BRANCH_WIDTH), BF16),
        scratch_shapes=[pltpu.VMEM((N_HEADS, HEAD_DIM, DK), F32)],
        compiler_params=_cparams("parallel", "arbitrary"),
        name="gla_mixer",
    )(u3, u3, u3, u3, u3, gw, gate_b.reshape(1, -1), norm_g.reshape(1, -1))


def _mlstm_body(qk_ref, v_ref, op_ref, misc_ref, cw_ref, cb_ref, ib_ref, fb_ref, ng_ref, o_ref,
                ext_ref, c_ref, n_ref, m_ref):
    L = CHUNK
    C2 = 2 * N_HEADS * DK

    @pl.when(pl.program_id(1) == 0)
    def _():
        ext_ref[0:8, :] = jnp.zeros((8, C2), F32)
        c_ref[...] = jnp.zeros_like(c_ref)
        n_ref[...] = jnp.zeros_like(n_ref)
        m_ref[...] = jnp.zeros_like(m_ref)

    x = qk_ref[...]
    ext_ref[8:8 + L, :] = x
    y = (cb_ref[...] + cw_ref[3:4, :] * x + cw_ref[2:3, :] * ext_ref[pl.ds(7, L), :]
         + cw_ref[1:2, :] * ext_ref[pl.ds(6, L), :] + cw_ref[0:1, :] * ext_ref[pl.ds(5, L), :])
    ext_ref[0:8, :] = x[L - 8:L, :]
    qk = _silu(y)
    q_all = qk[:, :N_HEADS * DK].astype(BF16)
    k_all = qk[:, N_HEADS * DK:] * (DK ** -0.5)

    misc = misc_ref[...]
    i_g = MLSTM_IGATE_CAP * jnp.tanh((misc + ib_ref[...]) / MLSTM_IGATE_CAP)
    log_f = _log_sigmoid(misc + fb_ref[...])
    row = lax.broadcasted_iota(jnp.int32, (L, L), 0)
    col = lax.broadcasted_iota(jnp.int32, (L, L), 1)
    causal = col <= row
    tri = jnp.where(causal, 1.0, 0.0).astype(F32)
    b = jnp.dot(tri, log_f, precision=HIGHEST, preferred_element_type=F32)
    it = pltpu.roll(i_g, MISC_MF - MISC_MI, 1)
    b_last = b[L - 1:L, :]
    m_s = m_ref[0:1, :]
    a_end = b_last - b + it
    m_new = jnp.maximum(b_last + m_s, jnp.max(a_end, axis=0, keepdims=True))
    w = jnp.exp(a_end - m_new)
    sc = jnp.exp(b_last + m_s - m_new)
    inter_log = b + m_s
    c = it - b
    rows_i = lax.broadcasted_iota(jnp.int32, c.shape, 0)
    cmax = c
    for sh in (1, 2, 4, 8, 16, 32):
        cmax = jnp.maximum(cmax, jnp.where(rows_i >= sh, pltpu.roll(cmax, sh, 0), -jnp.inf))
    m_i = jnp.maximum(inter_log, b + cmax)
    w_inter = jnp.exp(inter_log - m_i)
    e_neg = jnp.exp(-m_i)
    m_ref[0:1, :] = m_new

    stats = jnp.concatenate([b - m_i, w, w_inter, e_neg, jnp.broadcast_to(sc, (8, LANES))], axis=0)
    lane = lax.broadcasted_iota(jnp.int32, stats.shape, 1)
    stats = jnp.where((lane >= MISC_MF) & (lane < MISC_MF + N_HEADS), stats, 0.0)
    sel_r = lax.broadcasted_iota(jnp.int32, (LANES, N_HEADS * LANES), 0)
    sel_c = lax.broadcasted_iota(jnp.int32, (LANES, N_HEADS * LANES), 1) // LANES
    onehot = jnp.where(sel_r == sel_c + MISC_MF, 1.0, 0.0).astype(F32)
    spread = jnp.dot(stats, onehot, precision=HIGHEST, preferred_element_type=F32)

    def heads(x, width):
        return jnp.stack([x[:, h * width:(h + 1) * width] for h in range(N_HEADS)], axis=0)

    dcol3 = heads(spread[0:L], LANES)[:, :, 0:L]
    w3 = heads(spread[L:2 * L], LANES)[:, :, 0:DK]
    wi3 = heads(spread[2 * L:3 * L], LANES)
    en3 = heads(spread[3 * L:4 * L], LANES)
    sc3 = heads(spread[4 * L:4 * L + 1], LANES)
    c_t = c.T
    crow3 = jnp.stack([c_t[MISC_MF + h:MISC_MF + h + 1, :] for h in range(N_HEADS)], axis=0)

    q3 = heads(q_all, DK)
    k3 = heads(k_all, DK)
    v3 = heads(v_ref[...].astype(BF16), HEAD_DIM)
    c_s = c_ref[...]
    n_s = n_ref[...]
    decay3 = jnp.exp(jnp.where(causal[None], dcol3 + crow3, -jnp.inf))
    qk_s = jnp.einsum('hqd,hkd->hqk', q3, k3.astype(BF16), preferred_element_type=F32) * decay3
    num = (wi3 * jnp.einsum('hqd,hdv->hqv', q3, c_s.astype(BF16), preferred_element_type=F32)
           + jnp.einsum('hqk,hkv->hqv', qk_s.astype(BF16), v3, preferred_element_type=F32))
    qn = jnp.sum(q3.astype(F32) * n_s, axis=-1, keepdims=True)
    den = wi3[:, :, 0:1] * qn + jnp.sum(qk_s, axis=-1, keepdims=True)
    hout = num / jnp.maximum(jnp.abs(den), en3[:, :, 0:1])
    wk3 = w3 * k3
    c_ref[...] = sc3 * c_s + jnp.einsum('hkd,hkv->hdv', wk3.astype(BF16), v3, preferred_element_type=F32)
    n_ref[...] = sc3[:, :, 0:DK] * n_s + jnp.sum(wk3, axis=1, keepdims=True)
    hn = _head_norm(hout, heads(ng_ref[...], HEAD_DIM))
    out3 = (_sigmoid(heads(op_ref[...], HEAD_DIM)) * hn).astype(BF16)
    for h in range(N_HEADS):
        o_ref[:, h * HEAD_DIM:(h + 1) * HEAD_DIM] = out3[h]


def mlstm_mixer(u3, conv_w, conv_b, igate_b, fgate_b, norm_g):
    bsz, s, _ = u3.shape
    L = CHUNK
    ib = jnp.zeros((1, LANES), F32).at[0, MISC_MI:MISC_MI + N_HEADS].set(igate_b)
    fb = jnp.zeros((1, LANES), F32).at[0, MISC_MF:MISC_MF + N_HEADS].set(fgate_b)

    def ublock(width, off):
        return pl.BlockSpec((None, L, width), lambda b, t: (b, t, off // width))

    def const(shape):
        return pl.BlockSpec(shape, lambda b, t: (0,) * len(shape))

    return pl.pallas_call(
        _mlstm_body,
        grid=(bsz, s // L),
        in_specs=[ublock(1024, U_MQK), ublock(1024, U_MV), ublock(1024, U_MO), ublock(LANES, U_MISC),
                  const((MLSTM_CONV, 1024)), const((1, 1024)), const((1, LANES)), const((1, LANES)),
                  const((1, 1024))],
        out_specs=pl.BlockSpec((None, L, 1024), lambda b, t: (b, t, 0)),
        out_shape=jax.ShapeDtypeStruct((bsz, s, BRANCH_WIDTH), BF16),
        scratch_shapes=[pltpu.VMEM((L + 8, 1024), F32), pltpu.VMEM((N_HEADS, DK, HEAD_DIM), F32),
                        pltpu.VMEM((N_HEADS, 1, DK), F32), pltpu.VMEM((8, LANES), F32)],
        compiler_params=_cparams("parallel", "arbitrary"),
        name="mlstm_mixer",
    )(u3, u3, u3, u3, conv_w, conv_b.reshape(1, -1), ib, fb, norm_g.reshape(1, -1))


BIAS_DIST_RANGE = 4096


def bucket_thresholds(max_dist):
    lut = t5_bucket(jnp.arange(max_dist))
    return jnp.sum(lut[None, :] < jnp.arange(REL_BUCKETS)[:, None], axis=1).astype(jnp.int32)


def _bias_table_body(thr_ref, tab_ref, o_ref, *, kind):
    a = pl.program_id(0)
    c = pl.program_id(1)
    shape = o_ref.shape
    i = lax.broadcasted_iota(jnp.int32, shape, 0)
    j = lax.broadcasted_iota(jnp.int32, shape, 1)
    if kind == "dilated":
        dil = jnp.where(a == 0, DIL_PATTERNS[0][1], jnp.where(a == 1, DIL_PATTERNS[1][1], DIL_PATTERNS[2][1]))
        dist = (i + DIL_BLOCK - j) * dil
        head = a * N_HEADS + c
    elif kind in ("causal", "window"):
        dist = c * shape[0] + i - j
        head = 3 * N_HEADS + a
    else:
        dist = c * shape[0] + i - (j * NSA_CMP_STRIDE + NSA_CMP_BLOCK - 1)
        head = 3 * N_HEADS + a
    dist_c = jnp.maximum(dist, 0)
    acc = jnp.full(shape, tab_ref[head, 0], F32)
    for k in range(1, REL_BUCKETS):
        acc = jnp.where(dist_c >= thr_ref[k], tab_ref[head, k], acc)
    if kind in ("causal", "window"):
        limit = NSA_WINDOW if kind == "window" else 2 ** 30
        dist_v = jnp.where(c < pl.num_programs(1) - 1, dist, -1)
        acc = jnp.where((dist_v >= 0) & (dist_v < limit), acc, NEG)
    o_ref[...] = acc


def bias_table(thr, tab_t, kind, out_dims, block):
    smem = pl.BlockSpec(memory_space=pltpu.SMEM)
    return pl.pallas_call(
        functools.partial(_bias_table_body, kind=kind),
        grid=out_dims[:2] if kind != "compressed" else (out_dims[0], out_dims[1] // block[0]),
        in_specs=[smem, smem],
        out_specs=(pl.BlockSpec((None, None) + block, lambda a, c: (a, c, 0, 0)) if kind != "compressed"
                   else pl.BlockSpec((None,) + block, lambda a, c: (a, c, 0))),
        out_shape=jax.ShapeDtypeStruct(out_dims, F32),
        compiler_params=_cparams("parallel", "parallel"),
        name=f"bias_table_{kind}",
    )(thr, tab_t)


DIL_SPAN = 2048


def _dil_body(q0_ref, q1_ref, q2_ref, kp_ref, kc_ref, vp_ref, vc_ref, bias_ref, o_ref, m_scr, l_scr, acc_scr):
    P = DIL_BLOCK
    NB = DIL_SPAN // P
    q_refs = (q0_ref, q1_ref, q2_ref)
    row = lax.broadcasted_iota(jnp.int32, (P, P), 0)
    col = lax.broadcasted_iota(jnp.int32, (P, P), 1)
    mask_cur = jnp.where(col <= row, 0.0, NEG)
    mask_prev = jnp.where(col >= row, 0.0, NEG)
    no_prev = jnp.where(pl.program_id(1) > 0, 0, P)
    mask_prev_first = jnp.where((col - row) >= no_prev, 0.0, NEG)

    def rows(r, n, dil):
        return pl.ds(r, n, stride=dil) if dil > 1 else pl.ds(r, n)

    def stacked(ref, dil):
        per = DIL_SPAN // dil
        return jnp.concatenate([ref[rows(r, per, dil), :].reshape(per // P, P, HEAD_DIM) for r in range(dil)], axis=0)

    def stacked_prev(cur3, prev_ref, dil):
        nblk = NB // dil
        parts = []
        for r in range(dil):
            parts.append(prev_ref[rows(r + DIL_SPAN - dil * P, P, dil), :].reshape(1, P, HEAD_DIM))
            if nblk > 1:
                parts.append(cur3[r * nblk:(r + 1) * nblk - 1])
        return jnp.concatenate(parts, axis=0)

    for g, (_, dil) in enumerate(DIL_PATTERNS):
        nblk = NB // dil
        per = DIL_SPAN // dil
        q3 = (stacked(q_refs[g], dil) * (HEAD_DIM ** -0.5)).astype(BF16)
        kc3 = stacked(kc_ref, dil)
        vc3 = stacked(vc_ref, dil)
        kp3 = stacked_prev(kc3, kp_ref, dil).astype(BF16)
        vp3 = stacked_prev(vc3, vp_ref, dil).astype(BF16)
        kc3 = kc3.astype(BF16)
        vc3 = vc3.astype(BF16)
        bias_p = bias_ref[g, :, 0:P]
        bias_prev3 = jnp.concatenate(
            [(bias_p + (mask_prev_first if b % nblk == 0 else mask_prev)).reshape(1, P, P) for b in range(NB)], axis=0)
        s_p = jnp.einsum('nqd,nkd->nqk', q3, kp3, preferred_element_type=F32) + bias_prev3
        s_c = (jnp.einsum('nqd,nkd->nqk', q3, kc3, preferred_element_type=F32)
               + (bias_ref[g, :, P:2 * P] + mask_cur)[None])
        m = jnp.maximum(jnp.max(s_p, axis=-1, keepdims=True), jnp.max(s_c, axis=-1, keepdims=True))
        p_p = jnp.exp(s_p - m)
        p_c = jnp.exp(s_c - m)
        l = jnp.sum(p_p, axis=-1, keepdims=True) + jnp.sum(p_c, axis=-1, keepdims=True)
        acc = (jnp.einsum('nqk,nkd->nqd', p_p.astype(BF16), vp3, preferred_element_type=F32)
               + jnp.einsum('nqk,nkd->nqd', p_c.astype(BF16), vc3, preferred_element_type=F32))
        for r in range(dil):
            sl = rows(g * DIL_SPAN + r, per, dil)
            bs = slice(r * nblk, (r + 1) * nblk)
            m_scr[sl, :] = jnp.broadcast_to(m[bs].reshape(per, 1), (per, LANES))
            l_scr[sl, :] = jnp.broadcast_to(l[bs].reshape(per, 1), (per, LANES))
            acc_scr[sl, :] = acc[bs].reshape(per, HEAD_DIM)

    ms = [m_scr[g * DIL_SPAN:(g + 1) * DIL_SPAN, :] for g in range(len(DIL_PATTERNS))]
    m_all = jnp.maximum(jnp.maximum(ms[0], ms[1]), ms[2])
    num = jnp.zeros((DIL_SPAN, HEAD_DIM), F32)
    den = jnp.zeros((DIL_SPAN, LANES), F32)
    for g in range(len(DIL_PATTERNS)):
        w = jnp.exp(ms[g] - m_all)
        num = num + w * acc_scr[g * DIL_SPAN:(g + 1) * DIL_SPAN, :]
        den = den + w * l_scr[g * DIL_SPAN:(g + 1) * DIL_SPAN, :]
    o_ref[...] = (num / den).astype(BF16)


def dilated_bias(thr, tab_t):
    P = DIL_BLOCK
    return bias_table(thr, tab_t, "dilated", (len(DIL_PATTERNS), N_HEADS, P, 2 * P), (P, 2 * P))


def dilated_mixer(u3, bias):
    bsz, s, _ = u3.shape
    P = DIL_BLOCK
    assert s % DIL_SPAN == 0 and all(w // d == P and DIL_SPAN % (d * P) == 0 for w, d in DIL_PATTERNS)

    def ub(off, prev):
        def idx(b, t, h):
            return (b, jnp.maximum(t - 1, 0) if prev else t, off // HEAD_DIM + h)
        return pl.BlockSpec((None, DIL_SPAN, HEAD_DIM), idx)

    return pl.pallas_call(
        _dil_body,
        grid=(bsz, s // DIL_SPAN, N_HEADS),
        in_specs=[ub(U_DQ, False), ub(U_DQ + 1024, False), ub(U_DQ + 2048, False),
                  ub(U_DK, True), ub(U_DK, False), ub(U_DV, True), ub(U_DV, False),
                  pl.BlockSpec((len(DIL_PATTERNS), None, P, 2 * P), lambda b, t, h: (0, h, 0, 0))],
        out_specs=pl.BlockSpec((None, DIL_SPAN, HEAD_DIM), lambda b, t, h: (b, t, h)),
        out_shape=jax.ShapeDtypeStruct((bsz, s, BRANCH_WIDTH), BF16),
        scratch_shapes=[pltpu.VMEM((len(DIL_PATTERNS) * DIL_SPAN, LANES), F32),
                        pltpu.VMEM((len(DIL_PATTERNS) * DIL_SPAN, LANES), F32),
                        pltpu.VMEM((len(DIL_PATTERNS) * DIL_SPAN, HEAD_DIM), F32)],
        compiler_params=_cparams("parallel", "parallel", "arbitrary"),
        name="dilated_attn",
    )(u3, u3, u3, u3, u3, u3, u3, bias)


def _gelu_tanh(x):
    return 0.5 * x * (1.0 + jnp.tanh(math.sqrt(2.0 / math.pi) * (x + 0.044715 * (x * x * x))))


def _nsa_compress_body(t_ref, pe_ref, w1_ref, w2_ref, o_ref):
    nblk = t_ref.shape[0] // NSA_CMP_STRIDE
    half = NSA_CMP_STRIDE
    t1 = jnp.zeros((nblk, NSA_CMP_HIDDEN), F32)
    t2 = jnp.zeros((nblk, NSA_CMP_HIDDEN), F32)
    for p in range(half):
        xp = t_ref[pl.ds(p, nblk, stride=NSA_CMP_STRIDE), :]
        t1 = t1 + _dot((xp + pe_ref[p:p + 1, :]).astype(BF16), w1_ref[p * HEAD_DIM:(p + 1) * HEAD_DIM, :])
        t2 = t2 + _dot((xp + pe_ref[half + p:half + p + 1, :]).astype(BF16),
                       w1_ref[(half + p) * HEAD_DIM:(half + p + 1) * HEAD_DIM, :])
    hidden = t1 + pltpu.roll(t2, nblk - 1, 0)
    o_ref[...] = _dot(_gelu_tanh(hidden).astype(BF16), w2_ref[...])


def _nsa_compress(u3, off, pe, w1, w2, name):
    bsz, s, _ = u3.shape
    nblk = s // NSA_CMP_STRIDE
    G = NSA_KV_GROUPS
    return pl.pallas_call(
        _nsa_compress_body,
        grid=(bsz, G),
        in_specs=[pl.BlockSpec((None, s, HEAD_DIM), lambda b, g: (b, 0, off // HEAD_DIM + g)),
                  pl.BlockSpec((NSA_CMP_BLOCK, HEAD_DIM), lambda b, g: (0, 0)),
                  pl.BlockSpec((NSA_CMP_BLOCK * HEAD_DIM, NSA_CMP_HIDDEN), lambda b, g: (0, 0)),
                  pl.BlockSpec((NSA_CMP_HIDDEN, HEAD_DIM), lambda b, g: (0, 0))],
        out_specs=pl.BlockSpec((None, None, nblk, HEAD_DIM), lambda b, g: (b, g, 0, 0)),
        out_shape=jax.ShapeDtypeStruct((bsz, G, nblk, HEAD_DIM), F32),
        compiler_params=_cparams("parallel", "parallel"),
        name=name,
    )(u3, pe, w1.astype(BF16), w2.astype(BF16))


def _nsa_cmp_body(q_ref, kc_ref, vc_ref, bc_ref, agg_ref, oc_ref, sel_ref):
    TQ = q_ref.shape[0]
    ncb = kc_ref.shape[0]
    nsb = agg_ref.shape[1]
    t0 = pl.program_id(2) * TQ
    tpos = t0 + lax.broadcasted_iota(jnp.int32, (TQ, ncb), 0)
    ends = lax.broadcasted_iota(jnp.int32, (TQ, ncb), 1) * NSA_CMP_STRIDE + (NSA_CMP_BLOCK - 1)
    mask = ends <= tpos
    maskf = jnp.where(mask, 1.0, 0.0).astype(F32)
    kcb = kc_ref[...].astype(BF16)
    vcb = vc_ref[...].astype(BF16)
    psum = jnp.zeros((TQ, ncb), F32)
    for hg in range(NSA_HPG):
        hs = slice(hg * HEAD_DIM, (hg + 1) * HEAD_DIM)
        q = (q_ref[:, hs] * (HEAD_DIM ** -0.5)).astype(BF16)
        s = jnp.where(mask, _dot_t(q, kcb) + bc_ref[hg], NEG)
        p = jnp.exp(s - jnp.max(s, axis=-1, keepdims=True)) * maskf
        l = jnp.sum(p, axis=-1, keepdims=True)
        p = p / jnp.where(l > 0, l, 1.0)
        oc_ref[:, hs] = _dot(p.astype(BF16), vcb)
        psum = psum + p
    imp = jnp.dot(psum, agg_ref[...], precision=HIGHEST, preferred_element_type=F32)
    jblk = lax.broadcasted_iota(jnp.int32, (TQ, nsb), 1)
    tq = t0 + lax.broadcasted_iota(jnp.int32, (TQ, nsb), 0)
    cur = tq // NSA_SEL_BLOCK
    forced = (jblk == 0) | (jblk == cur) | (jblk == cur - 1)
    score = jnp.where(forced, BIG, jnp.where(jblk * NSA_SEL_BLOCK <= tq, imp, -BIG))
    jf = jblk.astype(F32)
    sel = jnp.zeros((TQ, nsb), F32)
    for _ in range(min(NSA_N_SEL, nsb)):
        mx = jnp.max(score, axis=-1, keepdims=True)
        first = jnp.min(jnp.where(score == mx, jf, float(nsb)), axis=-1, keepdims=True)
        pick = jf == first
        sel = jnp.where(pick, 1.0, sel)
        score = jnp.where(pick, -jnp.inf, score)
    sel_ref[...] = sel


def _sel_agg_matrix(n_cmp, n_sb):
    r = NSA_SEL_BLOCK // NSA_CMP_STRIDE
    c = NSA_CMP_BLOCK // NSA_CMP_STRIDE
    jj, aa, bb = np.meshgrid(np.arange(n_sb), np.arange(r), np.arange(c), indexing='ij')
    ii = r * jj + aa + bb - 1
    ok = (ii >= 0) & (ii < n_cmp)
    mat = np.zeros((n_cmp, n_sb), np.float32)
    np.add.at(mat, (ii[ok], jj[ok]), 1.0)
    return mat


def _nsa_selwin_body(q_ref, oc_ref, sel_ref, misc_ref, ks_ref, vs_ref, kw_ref, vw_ref, tt_ref, tw_ref, o_ref,
                     ksb_ref, vsb_ref, m_ref, l_ref, acc_ref, s_ref):
    TQ = q_ref.shape[0]
    TK = NSA_SEL_TK
    HG = NSA_HPG
    nsb = sel_ref.shape[1]
    sub = TK // TQ
    g = pl.program_id(1)
    qi = pl.program_id(2)

    @pl.when(qi == 0)
    def _():
        ksb_ref[:, 0:HEAD_DIM] = ks_ref[...].astype(BF16)
        blk = lax.broadcasted_iota(jnp.int32, (ksb_ref.shape[0], nsb), 0) // NSA_SEL_BLOCK
        ksb_ref[:, HEAD_DIM:HEAD_DIM + nsb] = jnp.where(
            blk == lax.broadcasted_iota(jnp.int32, (ksb_ref.shape[0], nsb), 1), 1.0, 0.0).astype(BF16)
        vsb_ref[...] = vs_ref[...].astype(BF16)

    q4 = jnp.concatenate([(q_ref[:, hg * HEAD_DIM:(hg + 1) * HEAD_DIM] * (HEAD_DIM ** -0.5)).astype(BF16)
                          for hg in range(HG)], axis=0)
    unselected = jnp.where(sel_ref[...] > 0.5, 0.0, NEG).astype(BF16)
    q4_aug = jnp.concatenate([q4, jnp.concatenate([unselected] * HG, axis=0)], axis=1)
    masked_tile = tt_ref.shape[1] - 1

    m_ref[...] = jnp.full_like(m_ref, 0.1 * NEG)
    l_ref[...] = jnp.zeros_like(l_ref)
    acc_ref[...] = jnp.zeros_like(acc_ref)

    last_tile = ksb_ref.shape[0] // TK - 1

    def logits(kj):
        k0 = pl.multiple_of(kj * TK, TK)

        def tile_of(w):
            d = qi - (kj * sub + w)
            return jnp.where(d >= 0, d, masked_tile)

        bias4 = jnp.concatenate([jnp.concatenate([tt_ref[hg, tile_of(w)] for w in range(sub)], axis=1)
                                 for hg in range(HG)], axis=0)
        return _dot_t(q4_aug, ksb_ref[pl.ds(k0, TK), :]) + bias4

    s_ref[...] = logits(0)

    def sel_step(kj, carry):
        s = s_ref[...]
        v_tile = vsb_ref[pl.ds(pl.multiple_of(kj * TK, TK), TK), :]
        m_old = m_ref[...]
        m_new = jnp.maximum(m_old, jnp.max(s, axis=-1, keepdims=True))
        alpha = jnp.exp(m_old - m_new)
        p = jnp.exp(s - m_new)
        l_ref[...] = alpha * l_ref[...] + jnp.sum(p, axis=-1, keepdims=True)
        acc_ref[...] = alpha * acc_ref[...] + _dot(p.astype(BF16), v_tile)
        m_ref[...] = m_new
        s_ref[...] = logits(jnp.minimum(kj + 1, last_tile))
        return carry

    lax.fori_loop(0, qi // sub + 1, sel_step, 0)

    nwin = NSA_WINDOW // TQ + 1
    first = jnp.maximum(qi - (nwin - 1), 0)
    w0 = pl.multiple_of(first * TQ, TQ)
    kw_tile = kw_ref[pl.ds(w0, nwin * TQ), :].astype(BF16)
    vw_tile = vw_ref[pl.ds(w0, nwin * TQ), :].astype(BF16)

    def wtile_of(w):
        d = qi - (first + w)
        return jnp.where(d >= 0, d, tw_ref.shape[1] - 1)

    wbias4 = jnp.concatenate([jnp.concatenate([tw_ref[hg, wtile_of(w)] for w in range(nwin)], axis=1)
                              for hg in range(HG)], axis=0)
    s = _dot_t(q4, kw_tile) + wbias4
    p = jnp.exp(s - jnp.max(s, axis=-1, keepdims=True))
    o_win = _dot(p.astype(BF16), vw_tile) / jnp.sum(p, axis=-1, keepdims=True)
    o_sel = acc_ref[...] / l_ref[...]

    misc = misc_ref[...]
    gpre = jnp.where(g == 0, misc[:, MISC_NG:MISC_NG + 3 * HG], misc[:, MISC_NG + 3 * HG:MISC_NG + 6 * HG])
    gates = _sigmoid(gpre)
    for hg in range(HG):
        hs = slice(hg * HEAD_DIM, (hg + 1) * HEAD_DIM)
        rs = slice(hg * TQ, (hg + 1) * TQ)
        y = (gates[:, 3 * hg:3 * hg + 1] * oc_ref[:, hs] + gates[:, 3 * hg + 1:3 * hg + 2] * o_sel[rs]
             + gates[:, 3 * hg + 2:3 * hg + 3] * o_win[rs])
        o_ref[:, hs] = y.astype(BF16)


NSA_TQ = 128
NSA_CMP_TQ = 512


def nsa_bias(thr, tab_t, s):
    nh = NSA_KV_GROUPS * NSA_HPG
    tile = (NSA_TQ, NSA_TQ)
    tt = bias_table(thr, tab_t, "causal", (nh, s // NSA_TQ + 1) + tile, tile)
    tw = bias_table(thr, tab_t, "window", (nh, NSA_WINDOW // NSA_TQ + 2) + tile, tile)
    bc = bias_table(thr, tab_t, "compressed", (nh, s, s // NSA_CMP_STRIDE), (NSA_TQ, s // NSA_CMP_STRIDE))
    return tt, tw, bc


def nsa_mixer(u3, tt, tw, bc, pe_k, pe_v, ck_w1, ck_w2, cv_w1, cv_w2):
    bsz, s, _ = u3.shape
    G, HG, dh = NSA_KV_GROUPS, NSA_HPG, HEAD_DIM
    TQ = NSA_TQ
    nqt = s // TQ
    assert NSA_KV_GROUPS == 2 and s % NSA_SEL_TK == 0 and s >= NSA_WINDOW + TQ
    nblk = s // NSA_CMP_STRIDE
    n_cmp = (s - NSA_CMP_BLOCK) // NSA_CMP_STRIDE + 1
    n_sb = s // NSA_SEL_BLOCK
    agg = np.zeros((nblk, n_sb), np.float32)
    agg[:n_cmp] = _sel_agg_matrix(n_cmp, n_sb)
    agg = jnp.asarray(agg)

    k_cmp = _nsa_compress(u3, U_NKC, pe_k, ck_w1, ck_w2, "nsa_compress_k")
    v_cmp = _nsa_compress(u3, U_NVC, pe_v, cv_w1, cv_w2, "nsa_compress_v")

    qspec = pl.BlockSpec((None, TQ, HG * dh), lambda b, g, i: (b, i, U_NQ // (HG * dh) + g))
    TC = NSA_CMP_TQ
    o_cmp, sel = pl.pallas_call(
        _nsa_cmp_body,
        grid=(bsz, G, s // TC),
        in_specs=[pl.BlockSpec((None, TC, HG * dh), lambda b, g, i: (b, i, U_NQ // (HG * dh) + g)),
                  pl.BlockSpec((None, None, nblk, dh), lambda b, g, i: (b, g, 0, 0)),
                  pl.BlockSpec((None, None, nblk, dh), lambda b, g, i: (b, g, 0, 0)),
                  pl.BlockSpec((HG, TC, nblk), lambda b, g, i: (g, i, 0)),
                  pl.BlockSpec((nblk, n_sb), lambda b, g, i: (0, 0))],
        out_specs=[pl.BlockSpec((None, TC, HG * dh), lambda b, g, i: (b, i, g)),
                   pl.BlockSpec((None, None, TC, n_sb), lambda b, g, i: (b, g, i, 0))],
        out_shape=[jax.ShapeDtypeStruct((bsz, s, G * HG * dh), F32),
                   jax.ShapeDtypeStruct((bsz, G, s, n_sb), F32)],
        compiler_params=_cparams("parallel", "parallel", "arbitrary"),
        name="nsa_compressed_attn",
    )(u3, k_cmp, v_cmp, bc, agg)

    def kv(off):
        return pl.BlockSpec((None, s, dh), lambda b, g, i: (b, 0, off // dh + g))

    y = pl.pallas_call(
        _nsa_selwin_body,
        grid=(bsz, G, nqt),
        in_specs=[qspec,
                  pl.BlockSpec((None, TQ, HG * dh), lambda b, g, i: (b, i, g)),
                  pl.BlockSpec((None, None, TQ, n_sb), lambda b, g, i: (b, g, i, 0)),
                  pl.BlockSpec((None, TQ, LANES), lambda b, g, i: (b, i, U_MISC // LANES)),
                  kv(U_NKS), kv(U_NVS), kv(U_NKW), kv(U_NVW),
                  pl.BlockSpec((HG,) + tt.shape[1:], lambda b, g, i: (g, 0, 0, 0)),
                  pl.BlockSpec((HG,) + tw.shape[1:], lambda b, g, i: (g, 0, 0, 0))],
        out_specs=pl.BlockSpec((None, TQ, HG * dh), lambda b, g, i: (b, i, g)),
        out_shape=jax.ShapeDtypeStruct((bsz, s, BRANCH_WIDTH), BF16),
        scratch_shapes=[pltpu.VMEM((s, dh + n_sb), BF16), pltpu.VMEM((s, dh), BF16),
                        pltpu.VMEM((HG * TQ, 1), F32), pltpu.VMEM((HG * TQ, 1), F32),
                        pltpu.VMEM((HG * TQ, dh), F32), pltpu.VMEM((HG * TQ, NSA_SEL_TK), F32)],
        compiler_params=_cparams("parallel", "parallel", "arbitrary"),
        name="nsa_selected_window_attn",
    )(u3, o_cmp, sel, u3, u3, u3, u3, u3, tt, tw)
    return y


def _merge_body(x_ref, wg0, wg1, wg2, wg3, y0, y1, y2, y3, wb_ref, o_ref):
    x = x_ref[...]
    acc = None
    for b, (wg, y) in enumerate(((wg0, y0), (wg1, y1), (wg2, y2), (wg3, y3))):
        gate = _sigmoid(_dot(x, wg[...]))
        term = gate * _dot(y[...], wb_ref[b])
        acc = term if acc is None else acc + term
    o_ref[...] = acc.astype(o_ref.dtype)


def merge_branches(hb, w_gates, ys, w_branch):
    n, d = hb.shape
    tm, tn = min(1024, n), min(256, d)
    nj = d // tn

    def wg(b):
        return pl.BlockSpec((d, tn), lambda i, j: (0, b * nj + j))

    yspec = pl.BlockSpec((tm, BRANCH_WIDTH), lambda i, j: (i, 0), pipeline_mode=pl.Buffered(1))
    return pl.pallas_call(
        _merge_body,
        grid=(n // tm, nj),
        in_specs=[pl.BlockSpec((tm, d), lambda i, j: (i, 0), pipeline_mode=pl.Buffered(1)),
                  wg(0), wg(1), wg(2), wg(3),
                  yspec, yspec, yspec, yspec,
                  pl.BlockSpec((N_BRANCH, BRANCH_WIDTH, tn), lambda i, j: (0, 0, j))],
        out_specs=pl.BlockSpec((tm, tn), lambda i, j: (i, j)),
        out_shape=jax.ShapeDtypeStruct((n, d), BF16),
        compiler_params=_cparams("parallel", "arbitrary"),
        name="merge_branches",
    )(hb, w_gates, w_gates, w_gates, w_gates, *ys, w_branch)


MOE_TN = 512
EXPERTS_PER_TILE = MOE_TN // EXPERT_FF


def _router_body(x_ref, w_ref, b_ref, comb_ref, combt_ref):
    logits = _dot(x_ref[...], w_ref[...]) + b_ref[...]
    lane = lax.broadcasted_iota(jnp.int32, logits.shape, 1).astype(F32)
    work = logits
    picks, vals = [], []
    for _ in range(TOP_K):
        mx = jnp.max(work, axis=-1, keepdims=True)
        first = jnp.min(jnp.where(work == mx, lane, float(LANES)), axis=-1, keepdims=True)
        pick = lane == first
        picks.append(pick)
        vals.append(mx)
        work = jnp.where(pick, -jnp.inf, work)
    exps = [jnp.exp(v - vals[0]) for v in vals]
    den = exps[0]
    for e in exps[1:]:
        den = den + e
    comb = jnp.zeros_like(logits)
    for pick, e in zip(picks, exps):
        comb = comb + jnp.where(pick, e / den, 0.0)
    comb_ref[...] = comb
    for t in range(N_EXPERTS // EXPERTS_PER_TILE):
        combt_ref[t] = comb[:, t * EXPERTS_PER_TILE:(t + 1) * EXPERTS_PER_TILE]


def moe_router(hb, router_w, router_b):
    n, d = hb.shape
    tm = min(512, n)
    w = jnp.zeros((d, LANES), F32).at[:, :N_EXPERTS].set(router_w).astype(BF16)
    b = jnp.full((1, LANES), NEG, F32).at[0, :N_EXPERTS].set(router_b)
    nt = N_EXPERTS // EXPERTS_PER_TILE
    return pl.pallas_call(
        _router_body,
        grid=(n // tm,),
        in_specs=[pl.BlockSpec((tm, d), lambda i: (i, 0)), pl.BlockSpec((d, LANES), lambda i: (0, 0)),
                  pl.BlockSpec((1, LANES), lambda i: (0, 0))],
        out_specs=[pl.BlockSpec((tm, LANES), lambda i: (i, 0)),
                   pl.BlockSpec((nt, tm, EXPERTS_PER_TILE), lambda i: (0, i, 0))],
        out_shape=[jax.ShapeDtypeStruct((n, LANES), F32), jax.ShapeDtypeStruct((nt, n, EXPERTS_PER_TILE), F32)],
        compiler_params=_cparams("parallel"),
        name="moe_router",
    )(hb, w, b)


def _moe_up_body(x_ref, wg_ref, wu_ref, bg_ref, bu_ref, comb_ref, a_ref):
    x = x_ref[...]
    gate = jnp.minimum(_dot(x, wg_ref[...]) + bg_ref[...], SWIGLU_LIMIT)
    up = jnp.clip(_dot(x, wu_ref[...]) + bu_ref[...], -SWIGLU_LIMIT, SWIGLU_LIMIT)
    act = gate * _sigmoid(SWIGLU_ALPHA * gate) * (up + 1.0)
    for e in range(EXPERTS_PER_TILE):
        es = slice(e * EXPERT_FF, (e + 1) * EXPERT_FF)
        a_ref[:, es] = (act[:, es] * comb_ref[:, e:e + 1]).astype(BF16)


def moe_up(hb, w_gate, w_up, b_gate, b_up, comb_t):
    n, d = hb.shape
    tm = min(1024, n)
    width = N_EXPERTS * EXPERT_FF
    wspec = pl.BlockSpec((d, MOE_TN), lambda i, j: (0, j))
    bspec = pl.BlockSpec((1, MOE_TN), lambda i, j: (0, j))
    return pl.pallas_call(
        _moe_up_body,
        grid=(n // tm, width // MOE_TN),
        in_specs=[pl.BlockSpec((tm, d), lambda i, j: (i, 0)), wspec, wspec, bspec, bspec,
                  pl.BlockSpec((None, tm, EXPERTS_PER_TILE), lambda i, j: (j, i, 0))],
        out_specs=pl.BlockSpec((tm, MOE_TN), lambda i, j: (i, j)),
        out_shape=jax.ShapeDtypeStruct((n, width), BF16),
        compiler_params=_cparams("parallel", "arbitrary"),
        name="moe_up",
    )(hb, w_gate, w_up, b_gate, b_up, comb_t)


def _moe_down_body(a_ref, w_ref, comb_ref, b2_ref, o_ref):
    o_ref[...] = (_dot(a_ref[...], w_ref[...]) + _dot(comb_ref[...].astype(BF16), b2_ref[...])).astype(o_ref.dtype)


def moe_down(a, w2, comb, b2):
    n, k = a.shape
    d = w2.shape[1]
    tm, tn = min(1024, n), min(1024, d)
    return pl.pallas_call(
        _moe_down_body,
        grid=(n // tm, d // tn),
        in_specs=[pl.BlockSpec((tm, k), lambda i, j: (i, 0)), pl.BlockSpec((k, tn), lambda i, j: (0, j)),
                  pl.BlockSpec((tm, LANES), lambda i, j: (i, 0)), pl.BlockSpec((LANES, tn), lambda i, j: (0, j))],
        out_specs=pl.BlockSpec((tm, tn), lambda i, j: (i, j)),
        out_shape=jax.ShapeDtypeStruct((n, d), BF16),
        compiler_params=_cparams("parallel", "arbitrary"),
        name="moe_down",
    )(a, w2, comb, b2)


def _regroup_body(w_ref, u_ref, g_ref):
    o = _SRC_OFF
    segments = ((o[0], o[4]),
                (o[5], o[8]),
                (o[10], o[13]),
                (o[13], o[20]),
                (o[4], o[5]), (o[8], o[10]), (o[20], o[21]))
    pos = 0
    for a, b in segments:
        u_ref[:, pos:pos + (b - a)] = w_ref[:, a:b]
        pos += b - a
    u_ref[:, pos:] = jnp.zeros((u_ref.shape[0], U_WIDTH - pos), BF16)
    g_ref[...] = w_ref[:, SRC_GATES:SRC_GATES + g_ref.shape[1]]


def _regroup_w_in(w_layer):
    d, width = w_layer.shape
    w = jnp.pad(w_layer.astype(BF16), ((0, 0), (0, -width % LANES)))
    rows = LANES
    return pl.pallas_call(
        _regroup_body,
        grid=(d // rows,),
        in_specs=[pl.BlockSpec((rows, w.shape[1]), lambda i: (i, 0))],
        out_specs=[pl.BlockSpec((rows, U_WIDTH), lambda i: (i, 0)),
                   pl.BlockSpec((rows, width - SRC_GATES), lambda i: (i, 0))],
        out_shape=[jax.ShapeDtypeStruct((d, U_WIDTH), BF16), jax.ShapeDtypeStruct((d, width - SRC_GATES), BF16)],
        compiler_params=_cparams("parallel"),
        name="regroup_w_in",
    )(w)


def kernel(x, rel_bias, w_in, gla_gate_w, gla_gate_b, gla_norm_g, mlstm_conv_w, mlstm_conv_b, mlstm_igate_b,
           mlstm_fgate_b, mlstm_norm_g, nsa_pe_k, nsa_pe_v, nsa_ck_w1, nsa_ck_w2, nsa_cv_w1, nsa_cv_w2, w_branch,
           w_out, ln1_g, ln1_b, router_w, router_b, exp_w1, exp_b1, exp_w2, exp_b2, ln2_g, ln2_b):
    out_dtype = x.dtype
    bsz, s, d = x.shape
    n = bsz * s
    h = x.astype(F32).reshape(n, d)
    hb = h.astype(BF16)
    thr = bucket_thresholds(max(s, BIAS_DIST_RANGE))
    tab_t = rel_bias.T
    dil_bias = dilated_bias(thr, tab_t)
    nsa_tt, nsa_tw, nsa_bc = nsa_bias(thr, tab_t, s)
    for l in range(w_in.shape[0]):
        w_u, w_gates = _regroup_w_in(w_in[l])
        u = matmul(hb, w_u, F32, 1024, 1024, "input_projection")
        u3 = u.reshape(bsz, s, U_WIDTH)
        y_a = gla_mixer(u3, gla_gate_w[l], gla_gate_b[l], gla_norm_g[l]).reshape(n, BRANCH_WIDTH)
        y_b = mlstm_mixer(u3, mlstm_conv_w[l], mlstm_conv_b[l], mlstm_igate_b[l], mlstm_fgate_b[l],
                          mlstm_norm_g[l]).reshape(n, BRANCH_WIDTH)
        y_c = dilated_mixer(u3, dil_bias).reshape(n, BRANCH_WIDTH)
        y_d = nsa_mixer(u3, nsa_tt, nsa_tw, nsa_bc, nsa_pe_k[l], nsa_pe_v[l], nsa_ck_w1[l], nsa_ck_w2[l], nsa_cv_w1[l],
                        nsa_cv_w2[l]).reshape(n, BRANCH_WIDTH)
        merged = merge_branches(hb, w_gates, (y_a, y_b, y_c, y_d), w_branch[l].astype(BF16))
        attn = matmul(merged, w_out[l].astype(BF16), BF16, 1024, 1024, "output_projection")
        h, hb = ln_residual(h, attn, ln1_g[l], ln1_b[l], "layer_norm_1")

        comb, comb_t = moe_router(hb, router_w[l], router_b[l])
        w1 = exp_w1[l]
        w_gate = w1[:, :, :EXPERT_FF].transpose(1, 0, 2).reshape(d, N_EXPERTS * EXPERT_FF).astype(BF16)
        w_up = w1[:, :, EXPERT_FF:].transpose(1, 0, 2).reshape(d, N_EXPERTS * EXPERT_FF).astype(BF16)
        b_gate = exp_b1[l][:, :EXPERT_FF].reshape(1, -1)
        b_up = exp_b1[l][:, EXPERT_FF:].reshape(1, -1)
        act = moe_up(hb, w_gate, w_up, b_gate, b_up, comb_t)
        b2 = jnp.zeros((LANES, d), F32).at[:N_EXPERTS].set(exp_b2[l]).astype(BF16)
        ffn = moe_down(act, exp_w2[l].reshape(N_EXPERTS * EXPERT_FF, d).astype(BF16), comb, b2)
        h, hb = ln_residual(h, ffn, ln2_g[l], ln2_b[l], "layer_norm_2")
    return h.reshape(bsz, s, d).astype(out_dtype)
```

```python
import functools
import math

import numpy as np
import jax
import jax.numpy as jnp
from jax import lax
from jax.experimental import pallas as pl
from jax.experimental.pallas import tpu as pltpu

F32 = jnp.float32
BF16 = jnp.bfloat16
HIGHEST = lax.Precision.HIGHEST

N_LAYERS_FOR_DEEPNORM = 4
HEAD_DIM = 128
BRANCH_WIDTH = 1024
N_BRANCH = 4
N_HEADS = 8
DK = 64
CHUNK = 64
GLA_GATE_RANK = 16
GLA_TAU = 16.0
MLSTM_CONV = 4
MLSTM_IGATE_CAP = 15.0
DIL_PATTERNS = ((128, 1), (512, 4), (2048, 16))
DIL_BLOCK = 128
NSA_KV_GROUPS = 2
NSA_HPG = 4
NSA_CMP_BLOCK = 32
NSA_CMP_STRIDE = 16
NSA_CMP_HIDDEN = 256
NSA_SEL_BLOCK = 64
NSA_N_SEL = 16
NSA_WINDOW = 512
NSA_SEL_TK = 512
REL_BUCKETS = 32
REL_MAX_DIST = 2048
N_EXPERTS = 32
TOP_K = 4
EXPERT_FF = 128
SWIGLU_LIMIT = 7.0
SWIGLU_ALPHA = 1.702
DEEPNORM_ALPHA = (2 * N_LAYERS_FOR_DEEPNORM) ** 0.25
LN_EPS = 1e-5
NEG = -1e30
BIG = 1e9

LANES = 128
VMEM_LIMIT_BYTES = 56 * 1024 * 1024

U_GQ, U_GK, U_GV, U_GR = 0, 512, 1024, 2048
U_MQK, U_MV, U_MO = 3072, 4096, 5120
U_DQ, U_DK, U_DV = 6144, 9216, 10240
U_NQ = 11264
U_NKC, U_NVC, U_NKS, U_NVS, U_NKW, U_NVW = 12288, 12544, 12800, 13056, 13312, 13568
U_MISC = 13824
U_WIDTH = 14336
MISC_GA, MISC_MI, MISC_MF, MISC_NG = 0, 16, 24, 32

_SRC_SIZES = (512, 512, 1024, 1024, 16, 1024, 1024, 1024, 8, 8, 3072, 1024, 1024, 1024,
              256, 256, 256, 256, 256, 256, 24)
_SRC_OFF = np.concatenate([[0], np.cumsum(_SRC_SIZES)]).tolist()
SRC_GATES = _SRC_OFF[-1]


def _cparams(*sem):
    return pltpu.CompilerParams(dimension_semantics=sem, vmem_limit_bytes=VMEM_LIMIT_BYTES)


def _log_sigmoid(x):
    return jnp.minimum(x, 0.0) - jnp.log1p(jnp.exp(-jnp.abs(x)))


def _sigmoid(x):
    return 1.0 / (1.0 + jnp.exp(-x))


def _silu(x):
    return x * _sigmoid(x)


def _dot(a, b):
    return jnp.dot(a, b, preferred_element_type=F32)


def _dot_t(a, b):
    return lax.dot_general(a, b, (((1,), (1,)), ((), ())), preferred_element_type=F32)


def _tdot(a, b):
    return lax.dot_general(a, b, (((0,), (0,)), ((), ())), preferred_element_type=F32)


def _head_norm(o, g_row):
    mu = jnp.mean(o, axis=-1, keepdims=True)
    d = o - mu
    var = jnp.mean(d * d, axis=-1, keepdims=True)
    return d * lax.rsqrt(var + LN_EPS) * g_row


def t5_bucket(dist):
    d = jnp.maximum(dist, 0)
    exact = REL_BUCKETS // 2
    df = jnp.maximum(d, 1).astype(jnp.float32)
    large = exact + (jnp.log(df / exact) / math.log(REL_MAX_DIST / exact) * (REL_BUCKETS - exact)).astype(jnp.int32)
    return jnp.where(d < exact, d, jnp.minimum(large, REL_BUCKETS - 1))


def _mm_body(x_ref, w_ref, o_ref):
    o_ref[...] = _dot(x_ref[...], w_ref[...]).astype(o_ref.dtype)


def matmul(x, w, out_dtype, tm, tn, name):
    m, k = x.shape
    n = w.shape[1]
    tm, tn = min(tm, m), min(tn, n)
    return pl.pallas_call(
        _mm_body,
        grid=(m // tm, n // tn),
        in_specs=[pl.BlockSpec((tm, k), lambda i, j: (i, 0)), pl.BlockSpec((k, tn), lambda i, j: (0, j))],
        out_specs=pl.BlockSpec((tm, tn), lambda i, j: (i, j)),
        out_shape=jax.ShapeDtypeStruct((m, n), out_dtype),
        compiler_params=_cparams("parallel", "arbitrary"),
        name=name,
    )(x, w)


def _ln_body(h_ref, d_ref, g_ref, b_ref, o_ref, ob_ref):
    z = DEEPNORM_ALPHA * h_ref[...] + d_ref[...]
    mu = jnp.mean(z, axis=-1, keepdims=True)
    zc = z - mu
    var = jnp.mean(zc * zc, axis=-1, keepdims=True)
    y = zc * lax.rsqrt(var + LN_EPS) * g_ref[...] + b_ref[...]
    o_ref[...] = y
    ob_ref[...] = y.astype(BF16)


def ln_residual(h, delta, g, b, name):
    n, d = h.shape
    tm = min(256, n)
    row = pl.BlockSpec((tm, d), lambda i: (i, 0))
    vec = pl.BlockSpec((1, d), lambda i: (0, 0))
    return pl.pallas_call(
        _ln_body,
        grid=(n // tm,),
        in_specs=[row, row, vec, vec],
        out_specs=[row, row],
        out_shape=[jax.ShapeDtypeStruct((n, d), F32), jax.ShapeDtypeStruct((n, d), BF16)],
        compiler_params=_cparams("parallel"),
        name=name,
    )(h, delta, g.reshape(1, d), b.reshape(1, d))


def _gla_body(q_ref, k_ref, v_ref, r_ref, misc_ref, gw_ref, gb_ref, ng_ref, o_ref, st_ref):
    L = CHUNK

    @pl.when(pl.program_id(1) == 0)
    def _():
        st_ref[...] = jnp.zeros_like(st_ref)

    pre = _dot(misc_ref[...].astype(BF16), gw_ref[...]) + gb_ref[...]
    log_a = _log_sigmoid(pre) / GLA_TAU
    row = lax.broadcasted_iota(jnp.int32, (L, L), 0)
    col = lax.broadcasted_iota(jnp.int32, (L, L), 1)
    causal = col <= row
    tri = jnp.where(causal, 1.0, 0.0).astype(F32)
    b = jnp.dot(tri, log_a, precision=HIGHEST, preferred_element_type=F32)
    b_last = b[L - 1:L, :]
    q_dec = (q_ref[...] * (DK ** -0.5) * jnp.exp(b)).astype(BF16)
    k_dec = (k_ref[...] * jnp.exp(-b)).astype(BF16)
    k_end = (k_ref[...] * jnp.exp(b_last - b)).astype(BF16)
    decay = jnp.exp(b_last)

    def heads(x, width):
        return jnp.stack([x[:, h * width:(h + 1) * width] for h in range(N_HEADS)], axis=0)

    q3, k3, ke3 = heads(q_dec, DK), heads(k_dec, DK), heads(k_end, DK)
    v3 = heads(v_ref[...].astype(BF16), HEAD_DIM)
    dec3 = heads(decay, DK)
    state_t = st_ref[...]
    att = jnp.where(causal[None], jnp.einsum('hqd,hkd->hqk', q3, k3, preferred_element_type=F32), 0.0)
    o3 = (jnp.einsum('hqk,hkv->hqv', att.astype(BF16), v3, preferred_element_type=F32)
          + jnp.einsum('hqd,hvd->hqv', q3, state_t.astype(BF16), preferred_element_type=F32))
    st_ref[...] = dec3 * state_t + jnp.einsum('hkv,hkd->hvd', v3, ke3, preferred_element_type=F32)
    on3 = _head_norm(o3, heads(ng_ref[...], HEAD_DIM))
    out3 = (_silu(heads(r_ref[...], HEAD_DIM)) * on3).astype(BF16)
    for h in range(N_HEADS):
        o_ref[:, h * HEAD_DIM:(h + 1) * HEAD_DIM] = out3[h]


def gla_mixer(u3, gate_w, gate_b, norm_g):
    bsz, s, _ = u3.shape
    L = CHUNK
    gw = jnp.zeros((LANES, N_HEADS * DK), F32).at[MISC_GA:MISC_GA + GLA_GATE_RANK].set(gate_w).astype(BF16)

    def ublock(width, off):
        return pl.BlockSpec((None, L, width), lambda b, t: (b, t, off // width))

    def const(shape):
        return pl.BlockSpec(shape, lambda b, t: (0,) * len(shape))

    return pl.pallas_call(
        _gla_body,
        grid=(bsz, s // L),
        in_specs=[ublock(512, U_GQ), ublock(512, U_GK), ublock(1024, U_GV), ublock(1024, U_GR),
                  ublock(LANES, U_MISC), const((LANES, 512)), const((1, 512)), const((1, 1024))],
        out_specs=pl.BlockSpec((None, L, 1024), lambda b, t: (b, t, 0)),
        out_shape=jax.ShapeDtypeStruct((bsz, s, BRANCH_WIDTH), BF16),
        scratch_shapes=[pltpu.VMEM((N_HEADS, HEAD_DIM, DK), F32)],
        compiler_params=_cparams("parallel", "arbitrary"),
        name="gla_mixer",
    )(u3, u3, u3, u3, u3, gw, gate_b.reshape(1, -1), norm_g.reshape(1, -1))


def _mlstm_body(qk_ref, v_ref, op_ref, misc_ref, cw_ref, cb_ref, ib_ref, fb_ref, ng_ref, o_ref,
                ext_ref, c_ref, n_ref, m_ref):
    L = CHUNK
    C2 = 2 * N_HEADS * DK

    @pl.when(pl.program_id(1) == 0)
    def _():
        ext_ref[0:8, :] = jnp.zeros((8, C2), F32)
        c_ref[...] = jnp.zeros_like(c_ref)
        n_ref[...] = jnp.zeros_like(n_ref)
        m_ref[...] = jnp.zeros_like(m_ref)

    x = qk_ref[...]
    ext_ref[8:8 + L, :] = x
    y = (cb_ref[...] + cw_ref[3:4, :] * x + cw_ref[2:3, :] * ext_ref[pl.ds(7, L), :]
         + cw_ref[1:2, :] * ext_ref[pl.ds(6, L), :] + cw_ref[0:1, :] * ext_ref[pl.ds(5, L), :])
    ext_ref[0:8, :] = x[L - 8:L, :]
    qk = _silu(y)
    q_all = qk[:, :N_HEADS * DK].astype(BF16)
    k_all = qk[:, N_HEADS * DK:] * (DK ** -0.5)

    misc = misc_ref[...]
    i_g = MLSTM_IGATE_CAP * jnp.tanh((misc + ib_ref[...]) / MLSTM_IGATE_CAP)
    log_f = _log_sigmoid(misc + fb_ref[...])
    row = lax.broadcasted_iota(jnp.int32, (L, L), 0)
    col = lax.broadcasted_iota(jnp.int32, (L, L), 1)
    causal = col <= row
    tri = jnp.where(causal, 1.0, 0.0).astype(F32)
    b = jnp.dot(tri, log_f, precision=HIGHEST, preferred_element_type=F32)
    it = pltpu.roll(i_g, MISC_MF - MISC_MI, 1)
    b_last = b[L - 1:L, :]
    m_s = m_ref[0:1, :]
    a_end = b_last - b + it
    m_new = jnp.maximum(b_last + m_s, jnp.max(a_end, axis=0, keepdims=True))
    w = jnp.exp(a_end - m_new)
    sc = jnp.exp(b_last + m_s - m_new)
    inter_log = b + m_s
    c = it - b
    rows_i = lax.broadcasted_iota(jnp.int32, c.shape, 0)
    cmax = c
    for sh in (1, 2, 4, 8, 16, 32):
        cmax = jnp.maximum(cmax, jnp.where(rows_i >= sh, pltpu.roll(cmax, sh, 0), -jnp.inf))
    m_i = jnp.maximum(inter_log, b + cmax)
    w_inter = jnp.exp(inter_log - m_i)
    e_neg = jnp.exp(-m_i)
    m_ref[0:1, :] = m_new

    stats = jnp.concatenate([b - m_i, w, w_inter, e_neg, jnp.broadcast_to(sc, (8, LANES))], axis=0)
    lane = lax.broadcasted_iota(jnp.int32, stats.shape, 1)
    stats = jnp.where((lane >= MISC_MF) & (lane < MISC_MF + N_HEADS), stats, 0.0)
    sel_r = lax.broadcasted_iota(jnp.int32, (LANES, N_HEADS * LANES), 0)
    sel_c = lax.broadcasted_iota(jnp.int32, (LANES, N_HEADS * LANES), 1) // LANES
    onehot = jnp.where(sel_r == sel_c + MISC_MF, 1.0, 0.0).astype(F32)
    spread = jnp.dot(stats, onehot, precision=HIGHEST, preferred_element_type=F32)

    def heads(x, width):
        return jnp.stack([x[:, h * width:(h + 1) * width] for h in range(N_HEADS)], axis=0)

    dcol3 = heads(spread[0:L], LANES)[:, :, 0:L]
    w3 = heads(spread[L:2 * L], LANES)[:, :, 0:DK]
    wi3 = heads(spread[2 * L:3 * L], LANES)
    en3 = heads(spread[3 * L:4 * L], LANES)
    sc3 = heads(spread[4 * L:4 * L + 1], LANES)
    c_t = c.T
    crow3 = jnp.stack([c_t[MISC_MF + h:MISC_MF + h + 1, :] for h in range(N_HEADS)], axis=0)

    q3 = heads(q_all, DK)
    k3 = heads(k_all, DK)
    v3 = heads(v_ref[...].astype(BF16), HEAD_DIM)
    c_s = c_ref[...]
    n_s = n_ref[...]
    decay3 = jnp.exp(jnp.where(causal[None], dcol3 + crow3, -jnp.inf))
    qk_s = jnp.einsum('hqd,hkd->hqk', q3, k3.astype(BF16), preferred_element_type=F32) * decay3
    num = (wi3 * jnp.einsum('hqd,hdv->hqv', q3, c_s.astype(BF16), preferred_element_type=F32)
           + jnp.einsum('hqk,hkv->hqv', qk_s.astype(BF16), v3, preferred_element_type=F32))
    qn = jnp.sum(q3.astype(F32) * n_s, axis=-1, keepdims=True)
    den = wi3[:, :, 0:1] * qn + jnp.sum(qk_s, axis=-1, keepdims=True)
    hout = num / jnp.maximum(jnp.abs(den), en3[:, :, 0:1])
    wk3 = w3 * k3
    c_ref[...] = sc3 * c_s + jnp.einsum('hkd,hkv->hdv', wk3.astype(BF16), v3, preferred_element_type=F32)
    n_ref[...] = sc3[:, :, 0:DK] * n_s + jnp.sum(wk3, axis=1, keepdims=True)
    hn = _head_norm(hout, heads(ng_ref[...], HEAD_DIM))
    out3 = (_sigmoid(heads(op_ref[...], HEAD_DIM)) * hn).astype(BF16)
    for h in range(N_HEADS):
        o_ref[:, h * HEAD_DIM:(h + 1) * HEAD_DIM] = out3[h]


def mlstm_mixer(u3, conv_w, conv_b, igate_b, fgate_b, norm_g):
    bsz, s, _ = u3.shape
    L = CHUNK
    ib = jnp.zeros((1, LANES), F32).at[0, MISC_MI:MISC_MI + N_HEADS].set(igate_b)
    fb = jnp.zeros((1, LANES), F32).at[0, MISC_MF:MISC_MF + N_HEADS].set(fgate_b)

    def ublock(width, off):
        return pl.BlockSpec((None, L, width), lambda b, t: (b, t, off // width))

    def const(shape):
        return pl.BlockSpec(shape, lambda b, t: (0,) * len(shape))

    return pl.pallas_call(
        _mlstm_body,
        grid=(bsz, s // L),
        in_specs=[ublock(1024, U_MQK), ublock(1024, U_MV), ublock(1024, U_MO), ublock(LANES, U_MISC),
                  const((MLSTM_CONV, 1024)), const((1, 1024)), const((1, LANES)), const((1, LANES)),
                  const((1, 1024))],
        out_specs=pl.BlockSpec((None, L, 1024), lambda b, t: (b, t, 0)),
        out_shape=jax.ShapeDtypeStruct((bsz, s, BRANCH_WIDTH), BF16),
        scratch_shapes=[pltpu.VMEM((L + 8, 1024), F32), pltpu.VMEM((N_HEADS, DK, HEAD_DIM), F32),
                        pltpu.VMEM((N_HEADS, 1, DK), F32), pltpu.VMEM((8, LANES), F32)],
        compiler_params=_cparams("parallel", "arbitrary"),
        name="mlstm_mixer",
    )(u3, u3, u3, u3, conv_w, conv_b.reshape(1, -1), ib, fb, norm_g.reshape(1, -1))


BIAS_DIST_RANGE = 4096


def bucket_thresholds(max_dist):
    lut = t5_bucket(jnp.arange(max_dist))
    return jnp.sum(lut[None, :] < jnp.arange(REL_BUCKETS)[:, None], axis=1).astype(jnp.int32)


def _bias_table_body(thr_ref, tab_ref, o_ref, *, kind):
    a = pl.program_id(0)
    c = pl.program_id(1)
    shape = o_ref.shape
    i = lax.broadcasted_iota(jnp.int32, shape, 0)
    j = lax.broadcasted_iota(jnp.int32, shape, 1)
    if kind == "dilated":
        dil = jnp.where(a == 0, DIL_PATTERNS[0][1], jnp.where(a == 1, DIL_PATTERNS[1][1], DIL_PATTERNS[2][1]))
        dist = (i + DIL_BLOCK - j) * dil
        head = a * N_HEADS + c
    elif kind in ("causal", "window"):
        dist = c * shape[0] + i - j
        head = 3 * N_HEADS + a
    else:
        dist = c * shape[0] + i - (j * NSA_CMP_STRIDE + NSA_CMP_BLOCK - 1)
        head = 3 * N_HEADS + a
    dist_c = jnp.maximum(dist, 0)
    acc = jnp.full(shape, tab_ref[head, 0], F32)
    for k in range(1, REL_BUCKETS):
        acc = jnp.where(dist_c >= thr_ref[k], tab_ref[head, k], acc)
    if kind in ("causal", "window"):
        limit = NSA_WINDOW if kind == "window" else 2 ** 30
        dist_v = jnp.where(c < pl.num_programs(1) - 1, dist, -1)
        acc = jnp.where((dist_v >= 0) & (dist_v < limit), acc, NEG)
    o_ref[...] = acc


def bias_table(thr, tab_t, kind, out_dims, block):
    smem = pl.BlockSpec(memory_space=pltpu.SMEM)
    return pl.pallas_call(
        functools.partial(_bias_table_body, kind=kind),
        grid=out_dims[:2] if kind != "compressed" else (out_dims[0], out_dims[1] // block[0]),
        in_specs=[smem, smem],
        out_specs=(pl.BlockSpec((None, None) + block, lambda a, c: (a, c, 0, 0)) if kind != "compressed"
                   else pl.BlockSpec((None,) + block, lambda a, c: (a, c, 0))),
        out_shape=jax.ShapeDtypeStruct(out_dims, F32),
        compiler_params=_cparams("parallel", "parallel"),
        name=f"bias_table_{kind}",
    )(thr, tab_t)


DIL_SPAN = 2048


def _dil_body(q0_ref, q1_ref, q2_ref, kp_ref, kc_ref, vp_ref, vc_ref, bias_ref, o_ref, m_scr, l_scr, acc_scr):
    P = DIL_BLOCK
    NB = DIL_SPAN // P
    q_refs = (q0_ref, q1_ref, q2_ref)
    row = lax.broadcasted_iota(jnp.int32, (P, P), 0)
    col = lax.broadcasted_iota(jnp.int32, (P, P), 1)
    mask_cur = jnp.where(col <= row, 0.0, NEG)
    mask_prev = jnp.where(col >= row, 0.0, NEG)
    no_prev = jnp.where(pl.program_id(1) > 0, 0, P)
    mask_prev_first = jnp.where((col - row) >= no_prev, 0.0, NEG)

    def rows(r, n, dil):
        return pl.ds(r, n, stride=dil) if dil > 1 else pl.ds(r, n)

    def stacked(ref, dil):
        per = DIL_SPAN // dil
        return jnp.concatenate([ref[rows(r, per, dil), :].reshape(per // P, P, HEAD_DIM) for r in range(dil)], axis=0)

    def stacked_prev(cur3, prev_ref, dil):
        nblk = NB // dil
        parts = []
        for r in range(dil):
            parts.append(prev_ref[rows(r + DIL_SPAN - dil * P, P, dil), :].reshape(1, P, HEAD_DIM))
            if nblk > 1:
                parts.append(cur3[r * nblk:(r + 1) * nblk - 1])
        return jnp.concatenate(parts, axis=0)

    for g, (_, dil) in enumerate(DIL_PATTERNS):
        nblk = NB // dil
        per = DIL_SPAN // dil
        q3 = (stacked(q_refs[g], dil) * (HEAD_DIM ** -0.5)).astype(BF16)
        kc3 = stacked(kc_ref, dil)
        vc3 = stacked(vc_ref, dil)
        kp3 = stacked_prev(kc3, kp_ref, dil).astype(BF16)
        vp3 = stacked_prev(vc3, vp_ref, dil).astype(BF16)
        kc3 = kc3.astype(BF16)
        vc3 = vc3.astype(BF16)
        bias_p = bias_ref[g, :, 0:P]
        bias_prev3 = jnp.concatenate(
            [(bias_p + (mask_prev_first if b % nblk == 0 else mask_prev)).reshape(1, P, P) for b in range(NB)], axis=0)
        s_p = jnp.einsum('nqd,nkd->nqk', q3, kp3, preferred_element_type=F32) + bias_prev3
        s_c = (jnp.einsum('nqd,nkd->nqk', q3, kc3, preferred_element_type=F32)
               + (bias_ref[g, :, P:2 * P] + mask_cur)[None])
        m = jnp.maximum(jnp.max(s_p, axis=-1, keepdims=True), jnp.max(s_c, axis=-1, keepdims=True))
        p_p = jnp.exp(s_p - m)
        p_c = jnp.exp(s_c - m)
        l = jnp.sum(p_p, axis=-1, keepdims=True) + jnp.sum(p_c, axis=-1, keepdims=True)
        acc = (jnp.einsum('nqk,nkd->nqd', p_p.astype(BF16), vp3, preferred_element_type=F32)
               + jnp.einsum('nqk,nkd->nqd', p_c.astype(BF16), vc3, preferred_element_type=F32))
        for r in range(dil):
            sl = rows(g * DIL_SPAN + r, per, dil)
            bs = slice(r * nblk, (r + 1) * nblk)
            m_scr[sl, :] = jnp.broadcast_to(m[bs].reshape(per, 1), (per, LANES))
            l_scr[sl, :] = jnp.broadcast_to(l[bs].reshape(per, 1), (per, LANES))
            acc_scr[sl, :] = acc[bs].reshape(per, HEAD_DIM)

    ms = [m_scr[g * DIL_SPAN:(g + 1) * DIL_SPAN, :] for g in range(len(DIL_PATTERNS))]
    m_all = jnp.maximum(jnp.maximum(ms[0], ms[1]), ms[2])
    num = jnp.zeros((DIL_SPAN, HEAD_DIM), F32)
    den = jnp.zeros((DIL_SPAN, LANES), F32)
    for g in range(len(DIL_PATTERNS)):
        w = jnp.exp(ms[g] - m_all)
        num = num + w * acc_scr[g * DIL_SPAN:(g + 1) * DIL_SPAN, :]
        den = den + w * l_scr[g * DIL_SPAN:(g + 1) * DIL_SPAN, :]
    o_ref[...] = (num / den).astype(BF16)


def dilated_bias(thr, tab_t):
    P = DIL_BLOCK
    return bias_table(thr, tab_t, "dilated", (len(DIL_PATTERNS), N_HEADS, P, 2 * P), (P, 2 * P))


def dilated_mixer(u3, bias):
    bsz, s, _ = u3.shape
    P = DIL_BLOCK
    assert s % DIL_SPAN == 0 and all(w // d == P and DIL_SPAN % (d * P) == 0 for w, d in DIL_PATTERNS)

    def ub(off, prev):
        def idx(b, t, h):
            return (b, jnp.maximum(t - 1, 0) if prev else t, off // HEAD_DIM + h)
        return pl.BlockSpec((None, DIL_SPAN, HEAD_DIM), idx)

    return pl.pallas_call(
        _dil_body,
        grid=(bsz, s // DIL_SPAN, N_HEADS),
        in_specs=[ub(U_DQ, False), ub(U_DQ + 1024, False), ub(U_DQ + 2048, False),
                  ub(U_DK, True), ub(U_DK, False), ub(U_DV, True), ub(U_DV, False),
                  pl.BlockSpec((len(DIL_PATTERNS), None, P, 2 * P), lambda b, t, h: (0, h, 0, 0))],
        out_specs=pl.BlockSpec((None, DIL_SPAN, HEAD_DIM), lambda b, t, h: (b, t, h)),
        out_shape=jax.ShapeDtypeStruct((bsz, s, BRANCH_WIDTH), BF16),
        scratch_shapes=[pltpu.VMEM((len(DIL_PATTERNS) * DIL_SPAN, LANES), F32),
                        pltpu.VMEM((len(DIL_PATTERNS) * DIL_SPAN, LANES), F32),
                        pltpu.VMEM((len(DIL_PATTERNS) * DIL_SPAN, HEAD_DIM), F32)],
        compiler_params=_cparams("parallel", "parallel", "arbitrary"),
        name="dilated_attn",
    )(u3, u3, u3, u3, u3, u3, u3, bias)


def _gelu_tanh(x):
    return 0.5 * x * (1.0 + jnp.tanh(math.sqrt(2.0 / math.pi) * (x + 0.044715 * (x * x * x))))


def _nsa_compress_body(t_ref, pe_ref, w1_ref, w2_ref, o_ref):
    nblk = t_ref.shape[0] // NSA_CMP_STRIDE
    half = NSA_CMP_STRIDE
    t1 = jnp.zeros((nblk, NSA_CMP_HIDDEN), F32)
    t2 = jnp.zeros((nblk, NSA_CMP_HIDDEN), F32)
    for p in range(half):
        xp = t_ref[pl.ds(p, nblk, stride=NSA_CMP_STRIDE), :]
        t1 = t1 + _dot((xp + pe_ref[p:p + 1, :]).astype(BF16), w1_ref[p * HEAD_DIM:(p + 1) * HEAD_DIM, :])
        t2 = t2 + _dot((xp + pe_ref[half + p:half + p + 1, :]).astype(BF16),
                       w1_ref[(half + p) * HEAD_DIM:(half + p + 1) * HEAD_DIM, :])
    hidden = t1 + pltpu.roll(t2, nblk - 1, 0)
    o_ref[...] = _dot(_gelu_tanh(hidden).astype(BF16), w2_ref[...])


def _nsa_compress(u3, off, pe, w1, w2, name):
    bsz, s, _ = u3.shape
    nblk = s // NSA_CMP_STRIDE
    G = NSA_KV_GROUPS
    return pl.pallas_call(
        _nsa_compress_body,
        grid=(bsz, G),
        in_specs=[pl.BlockSpec((None, s, HEAD_DIM), lambda b, g: (b, 0, off // HEAD_DIM + g)),
                  pl.BlockSpec((NSA_CMP_BLOCK, HEAD_DIM), lambda b, g: (0, 0)),
                  pl.BlockSpec((NSA_CMP_BLOCK * HEAD_DIM, NSA_CMP_HIDDEN), lambda b, g: (0, 0)),
                  pl.BlockSpec((NSA_CMP_HIDDEN, HEAD_DIM), lambda b, g: (0, 0))],
        out_specs=pl.BlockSpec((None, None, nblk, HEAD_DIM), lambda b, g: (b, g, 0, 0)),
        out_shape=jax.ShapeDtypeStruct((bsz, G, nblk, HEAD_DIM), F32),
        compiler_params=_cparams("parallel", "parallel"),
        name=name,
    )(u3, pe, w1.astype(BF16), w2.astype(BF16))


def _nsa_cmp_body(q_ref, kc_ref, vc_ref, bc_ref, agg_ref, oc_ref, sel_ref):
    TQ = q_ref.shape[0]
    ncb = kc_ref.shape[0]
    nsb = agg_ref.shape[1]
    t0 = pl.program_id(2) * TQ
    tpos = t0 + lax.broadcasted_iota(jnp.int32, (TQ, ncb), 0)
    ends = lax.broadcasted_iota(jnp.int32, (TQ, ncb), 1) * NSA_CMP_STRIDE + (NSA_CMP_BLOCK - 1)
    mask = ends <= tpos
    maskf = jnp.where(mask, 1.0, 0.0).astype(F32)
    kcb = kc_ref[...].astype(BF16)
    vcb = vc_ref[...].astype(BF16)
    psum = jnp.zeros((TQ, ncb), F32)
    for hg in range(NSA_HPG):
        hs = slice(hg * HEAD_DIM, (hg + 1) * HEAD_DIM)
        q = (q_ref[:, hs] * (HEAD_DIM ** -0.5)).astype(BF16)
        s = jnp.where(mask, _dot_t(q, kcb) + bc_ref[hg], NEG)
        p = jnp.exp(s - jnp.max(s, axis=-1, keepdims=True)) * maskf
        l = jnp.sum(p, axis=-1, keepdims=True)
        p = p / jnp.where(l > 0, l, 1.0)
        oc_ref[:, hs] = _dot(p.astype(BF16), vcb)
        psum = psum + p
    imp = jnp.dot(psum, agg_ref[...], precision=HIGHEST, preferred_element_type=F32)
    jblk = lax.broadcasted_iota(jnp.int32, (TQ, nsb), 1)
    tq = t0 + lax.broadcasted_iota(jnp.int32, (TQ, nsb), 0)
    cur = tq // NSA_SEL_BLOCK
    forced = (jblk == 0) | (jblk == cur) | (jblk == cur - 1)
    score = jnp.where(forced, BIG, jnp.where(jblk * NSA_SEL_BLOCK <= tq, imp, -BIG))
    jf = jblk.astype(F32)
    sel = jnp.zeros((TQ, nsb), F32)
    for _ in range(min(NSA_N_SEL, nsb)):
        mx = jnp.max(score, axis=-1, keepdims=True)
        first = jnp.min(jnp.where(score == mx, jf, float(nsb)), axis=-1, keepdims=True)
        pick = jf == first
        sel = jnp.where(pick, 1.0, sel)
        score = jnp.where(pick, -jnp.inf, score)
    sel_ref[...] = sel


def _sel_agg_matrix(n_cmp, n_sb):
    r = NSA_SEL_BLOCK // NSA_CMP_STRIDE
    c = NSA_CMP_BLOCK // NSA_CMP_STRIDE
    jj, aa, bb = np.meshgrid(np.arange(n_sb), np.arange(r), np.arange(c), indexing='ij')
    ii = r * jj + aa + bb - 1
    ok = (ii >= 0) & (ii < n_cmp)
    mat = np.zeros((n_cmp, n_sb), np.float32)
    np.add.at(mat, (ii[ok], jj[ok]), 1.0)
    return mat


def _nsa_selwin_body(q_ref, oc_ref, sel_ref, misc_ref, ks_ref, vs_ref, kw_ref, vw_ref, tt_ref, tw_ref, o_ref,
                     ksb_ref, vsb_ref, m_ref, l_ref, acc_ref, s_ref):
    TQ = q_ref.shape[0]
    TK = NSA_SEL_TK
    HG = NSA_HPG
    nsb = sel_ref.shape[1]
    sub = TK // TQ
    g = pl.program_id(1)
    qi = pl.program_id(2)

    @pl.when(qi == 0)
    def _():
        ksb_ref[:, 0:HEAD_DIM] = ks_ref[...].astype(BF16)
        blk = lax.broadcasted_iota(jnp.int32, (ksb_ref.shape[0], nsb), 0) // NSA_SEL_BLOCK
        ksb_ref[:, HEAD_DIM:HEAD_DIM + nsb] = jnp.where(
            blk == lax.broadcasted_iota(jnp.int32, (ksb_ref.shape[0], nsb), 1), 1.0, 0.0).astype(BF16)
        vsb_ref[...] = vs_ref[...].astype(BF16)

    q4 = jnp.concatenate([(q_ref[:, hg * HEAD_DIM:(hg + 1) * HEAD_DIM] * (HEAD_DIM ** -0.5)).astype(BF16)
                          for hg in range(HG)], axis=0)
    unselected = jnp.where(sel_ref[...] > 0.5, 0.0, NEG).astype(BF16)
    q4_aug = jnp.concatenate([q4, jnp.concatenate([unselected] * HG, axis=0)], axis=1)
    masked_tile = tt_ref.shape[1] - 1

    m_ref[...] = jnp.full_like(m_ref, 0.1 * NEG)
    l_ref[...] = jnp.zeros_like(l_ref)
    acc_ref[...] = jnp.zeros_like(acc_ref)

    last_tile = ksb_ref.shape[0] // TK - 1

    def logits(kj):
        k0 = pl.multiple_of(kj * TK, TK)

        def tile_of(w):
            d = qi - (kj * sub + w)
            return jnp.where(d >= 0, d, masked_tile)

        bias4 = jnp.concatenate([jnp.concatenate([tt_ref[hg, tile_of(w)] for w in range(sub)], axis=1)
                                 for hg in range(HG)], axis=0)
        return _dot_t(q4_aug, ksb_ref[pl.ds(k0, TK), :]) + bias4

    s_ref[...] = logits(0)

    def sel_step(kj, carry):
        s = s_ref[...]
        v_tile = vsb_ref[pl.ds(pl.multiple_of(kj * TK, TK), TK), :]
        m_old = m_ref[...]
        m_new = jnp.maximum(m_old, jnp.max(s, axis=-1, keepdims=True))
        alpha = jnp.exp(m_old - m_new)
        p = jnp.exp(s - m_new)
        l_ref[...] = alpha * l_ref[...] + jnp.sum(p, axis=-1, keepdims=True)
        acc_ref[...] = alpha * acc_ref[...] + _dot(p.astype(BF16), v_tile)
        m_ref[...] = m_new
        s_ref[...] = logits(jnp.minimum(kj + 1, last_tile))
        return carry

    lax.fori_loop(0, qi // sub + 1, sel_step, 0)

    nwin = NSA_WINDOW // TQ + 1
    first = jnp.maximum(qi - (nwin - 1), 0)
    w0 = pl.multiple_of(first * TQ, TQ)
    kw_tile = kw_ref[pl.ds(w0, nwin * TQ), :].astype(BF16)
    vw_tile = vw_ref[pl.ds(w0, nwin * TQ), :].astype(BF16)

    def wtile_of(w):
        d = qi - (first + w)
        return jnp.where(d >= 0, d, tw_ref.shape[1] - 1)

    wbias4 = jnp.concatenate([jnp.concatenate([tw_ref[hg, wtile_of(w)] for w in range(nwin)], axis=1)
                              for hg in range(HG)], axis=0)
    s = _dot_t(q4, kw_tile) + wbias4
    p = jnp.exp(s - jnp.max(s, axis=-1, keepdims=True))
    o_win = _dot(p.astype(BF16), vw_tile) / jnp.sum(p, axis=-1, keepdims=True)
    o_sel = acc_ref[...] / l_ref[...]

    misc = misc_ref[...]
    gpre = jnp.where(g == 0, misc[:, MISC_NG:MISC_NG + 3 * HG], misc[:, MISC_NG + 3 * HG:MISC_NG + 6 * HG])
    gates = _sigmoid(gpre)
    for hg in range(HG):
        hs = slice(hg * HEAD_DIM, (hg + 1) * HEAD_DIM)
        rs = slice(hg * TQ, (hg + 1) * TQ)
        y = (gates[:, 3 * hg:3 * hg + 1] * oc_ref[:, hs] + gates[:, 3 * hg + 1:3 * hg + 2] * o_sel[rs]
             + gates[:, 3 * hg + 2:3 * hg + 3] * o_win[rs])
        o_ref[:, hs] = y.astype(BF16)


NSA_TQ = 128
NSA_CMP_TQ = 512


def nsa_bias(thr, tab_t, s):
    nh = NSA_KV_GROUPS * NSA_HPG
    tile = (NSA_TQ, NSA_TQ)
    tt = bias_table(thr, tab_t, "causal", (nh, s // NSA_TQ + 1) + tile, tile)
    tw = bias_table(thr, tab_t, "window", (nh, NSA_WINDOW // NSA_TQ + 2) + tile, tile)
    bc = bias_table(thr, tab_t, "compressed", (nh, s, s // NSA_CMP_STRIDE), (NSA_TQ, s // NSA_CMP_STRIDE))
    return tt, tw, bc


def nsa_mixer(u3, tt, tw, bc, pe_k, pe_v, ck_w1, ck_w2, cv_w1, cv_w2):
    bsz, s, _ = u3.shape
    G, HG, dh = NSA_KV_GROUPS, NSA_HPG, HEAD_DIM
    TQ = NSA_TQ
    nqt = s // TQ
    assert NSA_KV_GROUPS == 2 and s % NSA_SEL_TK == 0 and s >= NSA_WINDOW + TQ
    nblk = s // NSA_CMP_STRIDE
    n_cmp = (s - NSA_CMP_BLOCK) // NSA_CMP_STRIDE + 1
    n_sb = s // NSA_SEL_BLOCK
    agg = np.zeros((nblk, n_sb), np.float32)
    agg[:n_cmp] = _sel_agg_matrix(n_cmp, n_sb)
    agg = jnp.asarray(agg)

    k_cmp = _nsa_compress(u3, U_NKC, pe_k, ck_w1, ck_w2, "nsa_compress_k")
    v_cmp = _nsa_compress(u3, U_NVC, pe_v, cv_w1, cv_w2, "nsa_compress_v")

    qspec = pl.BlockSpec((None, TQ, HG * dh), lambda b, g, i: (b, i, U_NQ // (HG * dh) + g))
    TC = NSA_CMP_TQ
    o_cmp, sel = pl.pallas_call(
        _nsa_cmp_body,
        grid=(bsz, G, s // TC),
        in_specs=[pl.BlockSpec((None, TC, HG * dh), lambda b, g, i: (b, i, U_NQ // (HG * dh) + g)),
                  pl.BlockSpec((None, None, nblk, dh), lambda b, g, i: (b, g, 0, 0)),
                  pl.BlockSpec((None, None, nblk, dh), lambda b, g, i: (b, g, 0, 0)),
                  pl.BlockSpec((HG, TC, nblk), lambda b, g, i: (g, i, 0)),
                  pl.BlockSpec((nblk, n_sb), lambda b, g, i: (0, 0))],
        out_specs=[pl.BlockSpec((None, TC, HG * dh), lambda b, g, i: (b, i, g)),
                   pl.BlockSpec((None, None, TC, n_sb), lambda b, g, i: (b, g, i, 0))],
        out_shape=[jax.ShapeDtypeStruct((bsz, s, G * HG * dh), F32),
                   jax.ShapeDtypeStruct((bsz, G, s, n_sb), F32)],
        compiler_params=_cparams("parallel", "parallel", "arbitrary"),
        name="nsa_compressed_attn",
    )(u3, k_cmp, v_cmp, bc, agg)

    def kv(off):
        return pl.BlockSpec((None, s, dh), lambda b, g, i: (b, 0, off // dh + g))

    y = pl.pallas_call(
        _nsa_selwin_body,
        grid=(bsz, G, nqt),
        in_specs=[qspec,
                  pl.BlockSpec((None, TQ, HG * dh), lambda b, g, i: (b, i, g)),
                  pl.BlockSpec((None, None, TQ, n_sb), lambda b, g, i: (b, g, i, 0)),
                  pl.BlockSpec((None, TQ, LANES), lambda b, g, i: (b, i, U_MISC // LANES)),
                  kv(U_NKS), kv(U_NVS), kv(U_NKW), kv(U_NVW),
                  pl.BlockSpec((HG,) + tt.shape[1:], lambda b, g, i: (g, 0, 0, 0)),
                  pl.BlockSpec((HG,) + tw.shape[1:], lambda b, g, i: (g, 0, 0, 0))],
        out_specs=pl.BlockSpec((None, TQ, HG * dh), lambda b, g, i: (b, i, g)),
        out_shape=jax.ShapeDtypeStruct((bsz, s, BRANCH_WIDTH), BF16),
        scratch_shapes=[pltpu.VMEM((s, dh + n_sb), BF16), pltpu.VMEM((s, dh), BF16),
                        pltpu.VMEM((HG * TQ, 1), F32), pltpu.VMEM((HG * TQ, 1), F32),
                        pltpu.VMEM((HG * TQ, dh), F32), pltpu.VMEM((HG * TQ, NSA_SEL_TK), F32)],
        compiler_params=_cparams("parallel", "parallel", "arbitrary"),
        name="nsa_selected_window_attn",
    )(u3, o_cmp, sel, u3, u3, u3, u3, u3, tt, tw)
    return y


def _merge_body(x_ref, wg0, wg1, wg2, wg3, y0, y1, y2, y3, wb_ref, o_ref):
    x = x_ref[...]
    acc = None
    for b, (wg, y) in enumerate(((wg0, y0), (wg1, y1), (wg2, y2), (wg3, y3))):
        gate = _sigmoid(_dot(x, wg[...]))
        term = gate * _dot(y[...], wb_ref[b])
        acc = term if acc is None else acc + term
    o_ref[...] = acc.astype(o_ref.dtype)


def merge_branches(hb, w_gates, ys, w_branch):
    n, d = hb.shape
    tm, tn = min(1024, n), min(256, d)
    nj = d // tn

    def wg(b):
        return pl.BlockSpec((d, tn), lambda i, j: (0, b * nj + j))

    yspec = pl.BlockSpec((tm, BRANCH_WIDTH), lambda i, j: (i, 0), pipeline_mode=pl.Buffered(1))
    return pl.pallas_call(
        _merge_body,
        grid=(n // tm, nj),
        in_specs=[pl.BlockSpec((tm, d), lambda i, j: (i, 0), pipeline_mode=pl.Buffered(1)),
                  wg(0), wg(1), wg(2), wg(3),
                  yspec, yspec, yspec, yspec,
                  pl.BlockSpec((N_BRANCH, BRANCH_WIDTH, tn), lambda i, j: (0, 0, j))],
        out_specs=pl.BlockSpec((tm, tn), lambda i, j: (i, j)),
        out_shape=jax.ShapeDtypeStruct((n, d), BF16),
        compiler_params=_cparams("parallel", "arbitrary"),
        name="merge_branches",
    )(hb, w_gates, w_gates, w_gates, w_gates, *ys, w_branch)


MOE_TN = 512
EXPERTS_PER_TILE = MOE_TN // EXPERT_FF


def _router_body(x_ref, w_ref, b_ref, comb_ref, combt_ref):
    logits = _dot(x_ref[...], w_ref[...]) + b_ref[...]
    lane = lax.broadcasted_iota(jnp.int32, logits.shape, 1).astype(F32)
    work = logits
    picks, vals = [], []
    for _ in range(TOP_K):
        mx = jnp.max(work, axis=-1, keepdims=True)
        first = jnp.min(jnp.where(work == mx, lane, float(LANES)), axis=-1, keepdims=True)
        pick = lane == first
        picks.append(pick)
        vals.append(mx)
        work = jnp.where(pick, -jnp.inf, work)
    exps = [jnp.exp(v - vals[0]) for v in vals]
    den = exps[0]
    for e in exps[1:]:
        den = den + e
    comb = jnp.zeros_like(logits)
    for pick, e in zip(picks, exps):
        comb = comb + jnp.where(pick, e / den, 0.0)
    comb_ref[...] = comb
    for t in range(N_EXPERTS // EXPERTS_PER_TILE):
        combt_ref[t] = comb[:, t * EXPERTS_PER_TILE:(t + 1) * EXPERTS_PER_TILE]


def moe_router(hb, router_w, router_b):
    n, d = hb.shape
    tm = min(512, n)
    w = jnp.zeros((d, LANES), F32).at[:, :N_EXPERTS].set(router_w).astype(BF16)
    b = jnp.full((1, LANES), NEG, F32).at[0, :N_EXPERTS].set(router_b)
    nt = N_EXPERTS // EXPERTS_PER_TILE
    return pl.pallas_call(
        _router_body,
        grid=(n // tm,),
        in_specs=[pl.BlockSpec((tm, d), lambda i: (i, 0)), pl.BlockSpec((d, LANES), lambda i: (0, 0)),
                  pl.BlockSpec((1, LANES), lambda i: (0, 0))],
        out_specs=[pl.BlockSpec((tm, LANES), lambda i: (i, 0)),
                   pl.BlockSpec((nt, tm, EXPERTS_PER_TILE), lambda i: (0, i, 0))],
        out_shape=[jax.ShapeDtypeStruct((n, LANES), F32), jax.ShapeDtypeStruct((nt, n, EXPERTS_PER_TILE), F32)],
        compiler_params=_cparams("parallel"),
        name="moe_router",
    )(hb, w, b)


def _moe_up_body(x_ref, wg_ref, wu_ref, bg_ref, bu_ref, comb_ref, a_ref):
    x = x_ref[...]
    gate = jnp.minimum(_dot(x, wg_ref[...]) + bg_ref[...], SWIGLU_LIMIT)
    up = jnp.clip(_dot(x, wu_ref[...]) + bu_ref[...], -SWIGLU_LIMIT, SWIGLU_LIMIT)
    act = gate * _sigmoid(SWIGLU_ALPHA * gate) * (up + 1.0)
    for e in range(EXPERTS_PER_TILE):
        es = slice(e * EXPERT_FF, (e + 1) * EXPERT_FF)
        a_ref[:, es] = (act[:, es] * comb_ref[:, e:e + 1]).astype(BF16)


def moe_up(hb, w_gate, w_up, b_gate, b_up, comb_t):
    n, d = hb.shape
    tm = min(1024, n)
    width = N_EXPERTS * EXPERT_FF
    wspec = pl.BlockSpec((d, MOE_TN), lambda i, j: (0, j))
    bspec = pl.BlockSpec((1, MOE_TN), lambda i, j: (0, j))
    return pl.pallas_call(
        _moe_up_body,
        grid=(n // tm, width // MOE_TN),
        in_specs=[pl.BlockSpec((tm, d), lambda i, j: (i, 0)), wspec, wspec, bspec, bspec,
                  pl.BlockSpec((None, tm, EXPERTS_PER_TILE), lambda i, j: (j, i, 0))],
        out_specs=pl.BlockSpec((tm, MOE_TN), lambda i, j: (i, j)),
        out_shape=jax.ShapeDtypeStruct((n, width), BF16),
        compiler_params=_cparams("parallel", "arbitrary"),
        name="moe_up",
    )(hb, w_gate, w_up, b_gate, b_up, comb_t)


def _moe_down_body(a_ref, w_ref, comb_ref, b2_ref, o_ref):
    o_ref[...] = (_dot(a_ref[...], w_ref[...]) + _dot(comb_ref[...].astype(BF16), b2_ref[...])).astype(o_ref.dtype)


def moe_down(a, w2, comb, b2):
    n, k = a.shape
    d = w2.shape[1]
    tm, tn = min(1024, n), min(1024, d)
    return pl.pallas_call(
        _moe_down_body,
        grid=(n // tm, d // tn),
        in_specs=[pl.BlockSpec((tm, k), lambda i, j: (i, 0)), pl.BlockSpec((k, tn), lambda i, j: (0, j)),
                  pl.BlockSpec((tm, LANES), lambda i, j: (i, 0)), pl.BlockSpec((LANES, tn), lambda i, j: (0, j))],
        out_specs=pl.BlockSpec((tm, tn), lambda i, j: (i, j)),
        out_shape=jax.ShapeDtypeStruct((n, d), BF16),
        compiler_params=_cparams("parallel", "arbitrary"),
        name="moe_down",
    )(a, w2, comb, b2)


def _regroup_w_in(w):
    d = w.shape[0]
    o = _SRC_OFF
    pad = U_WIDTH - U_MISC - (GLA_GATE_RANK + 2 * N_HEADS + 3 * N_HEADS)
    w_u = jnp.concatenate([
        w[:, o[0]:o[4]],
        w[:, o[5]:o[8]],
        w[:, o[10]:o[13]],
        w[:, o[13]:o[20]],
        w[:, o[4]:o[5]], w[:, o[8]:o[10]], w[:, o[20]:o[21]],
        jnp.zeros((d, pad), w.dtype),
    ], axis=1).astype(BF16)
    return w_u, w[:, SRC_GATES:].astype(BF16)


def kernel(x, rel_bias, w_in, gla_gate_w, gla_gate_b, gla_norm_g, mlstm_conv_w, mlstm_conv_b, mlstm_igate_b,
           mlstm_fgate_b, mlstm_norm_g, nsa_pe_k, nsa_pe_v, nsa_ck_w1, nsa_ck_w2, nsa_cv_w1, nsa_cv_w2, w_branch,
           w_out, ln1_g, ln1_b, router_w, router_b, exp_w1, exp_b1, exp_w2, exp_b2, ln2_g, ln2_b):
    out_dtype = x.dtype
    bsz, s, d = x.shape
    n = bsz * s
    h = x.astype(F32).reshape(n, d)
    hb = h.astype(BF16)
    thr = bucket_thresholds(max(s, BIAS_DIST_RANGE))
    tab_t = rel_bias.T
    dil_bias = dilated_bias(thr, tab_t)
    nsa_tt, nsa_tw, nsa_bc = nsa_bias(thr, tab_t, s)
    for l in range(w_in.shape[0]):
        w_u, w_gates = _regroup_w_in(w_in[l])
        u = matmul(hb, w_u, F32, 1024, 1024, "input_projection")
        u3 = u.reshape(bsz, s, U_WIDTH)
        y_a = gla_mixer(u3, gla_gate_w[l], gla_gate_b[l], gla_norm_g[l]).reshape(n, BRANCH_WIDTH)
        y_b = mlstm_mixer(u3, mlstm_conv_w[l], mlstm_conv_b[l], mlstm_igate_b[l], mlstm_fgate_b[l],
                          mlstm_norm_g[l]).reshape(n, BRANCH_WIDTH)
        y_c = dilated_mixer(u3, dil_bias).reshape(n, BRANCH_WIDTH)
        y_d = nsa_mixer(u3, nsa_tt, nsa_tw, nsa_bc, nsa_pe_k[l], nsa_pe_v[l], nsa_ck_w1[l], nsa_ck_w2[l], nsa_cv_w1[l],
                        nsa_cv_w2[l]).reshape(n, BRANCH_WIDTH)
        merged = merge_branches(hb, w_gates, (y_a, y_b, y_c, y_d), w_branch[l].astype(BF16))
        attn = matmul(merged, w_out[l].astype(BF16), BF16, 1024, 1024, "output_projection")
        h, hb = ln_residual(h, attn, ln1_g[l], ln1_b[l], "layer_norm_1")

        comb, comb_t = moe_router(hb, router_w[l], router_b[l])
        w1 = exp_w1[l]
        w_gate = w1[:, :, :EXPERT_FF].transpose(1, 0, 2).reshape(d, N_EXPERTS * EXPERT_FF).astype(BF16)
        w_up = w1[:, :, EXPERT_FF:].transpose(1, 0, 2).reshape(d, N_EXPERTS * EXPERT_FF).astype(BF16)
        b_gate = exp_b1[l][:, :EXPERT_FF].reshape(1, -1)
        b_up = exp_b1[l][:, EXPERT_FF:].reshape(1, -1)
        act = moe_up(hb, w_gate, w_up, b_gate, b_up, comb_t)
        b2 = jnp.zeros((LANES, d), F32).at[:N_EXPERTS].set(exp_b2[l]).astype(BF16)
        ffn = moe_down(act, exp_w2[l].reshape(N_EXPERTS * EXPERT_FF, d).astype(BF16), comb, b2)
        h, hb = ln_residual(h, ffn, ln2_g[l], ln2_b[l], "layer_norm_2")
    return h.reshape(bsz, s, d).astype(out_dtype)
```

```python
import functools
import math

import numpy as np
import jax
import jax.numpy as jnp
from jax import lax
from jax.experimental import pallas as pl
from jax.experimental.pallas import tpu as pltpu

F32 = jnp.float32
BF16 = jnp.bfloat16
HIGHEST = lax.Precision.HIGHEST

N_LAYERS_FOR_DEEPNORM = 4
HEAD_DIM = 128
BRANCH_WIDTH = 1024
N_BRANCH = 4
N_HEADS = 8
DK = 64
CHUNK = 64
GLA_GATE_RANK = 16
GLA_TAU = 16.0
MLSTM_CONV = 4
MLSTM_IGATE_CAP = 15.0
DIL_PATTERNS = ((128, 1), (512, 4), (2048, 16))
DIL_BLOCK = 128
NSA_KV_GROUPS = 2
NSA_HPG = 4
NSA_CMP_BLOCK = 32
NSA_CMP_STRIDE = 16
NSA_CMP_HIDDEN = 256
NSA_SEL_BLOCK = 64
NSA_N_SEL = 16
NSA_WINDOW = 512
NSA_SEL_TK = 512
REL_BUCKETS = 32
REL_MAX_DIST = 2048
N_EXPERTS = 32
TOP_K = 4
EXPERT_FF = 128
SWIGLU_LIMIT = 7.0
SWIGLU_ALPHA = 1.702
DEEPNORM_ALPHA = (2 * N_LAYERS_FOR_DEEPNORM) ** 0.25
LN_EPS = 1e-5
NEG = -1e30
BIG = 1e9

LANES = 128
VMEM_LIMIT_BYTES = 56 * 1024 * 1024

U_GQ, U_GK, U_GV, U_GR = 0, 512, 1024, 2048
U_MQK, U_MV, U_MO = 3072, 4096, 5120
U_DQ, U_DK, U_DV = 6144, 9216, 10240
U_NQ = 11264
U_NKC, U_NVC, U_NKS, U_NVS, U_NKW, U_NVW = 12288, 12544, 12800, 13056, 13312, 13568
U_MISC = 13824
U_WIDTH = 14336
MISC_GA, MISC_MI, MISC_MF, MISC_NG = 0, 16, 24, 32

_SRC_SIZES = (512, 512, 1024, 1024, 16, 1024, 1024, 1024, 8, 8, 3072, 1024, 1024, 1024,
              256, 256, 256, 256, 256, 256, 24)
_SRC_OFF = np.concatenate([[0], np.cumsum(_SRC_SIZES)]).tolist()
SRC_GATES = _SRC_OFF[-1]


def _cparams(*sem):
    return pltpu.CompilerParams(dimension_semantics=sem, vmem_limit_bytes=VMEM_LIMIT_BYTES)


def _log_sigmoid(x):
    return jnp.minimum(x, 0.0) - jnp.log1p(jnp.exp(-jnp.abs(x)))


def _sigmoid(x):
    return 1.0 / (1.0 + jnp.exp(-x))


def _silu(x):
    return x * _sigmoid(x)


def _dot(a, b):
    return jnp.dot(a, b, preferred_element_type=F32)


def _dot_t(a, b):
    return lax.dot_general(a, b, (((1,), (1,)), ((), ())), preferred_element_type=F32)


def _tdot(a, b):
    return lax.dot_general(a, b, (((0,), (0,)), ((), ())), preferred_element_type=F32)


def _head_norm(o, g_row):
    mu = jnp.mean(o, axis=-1, keepdims=True)
    d = o - mu
    var = jnp.mean(d * d, axis=-1, keepdims=True)
    return d * lax.rsqrt(var + LN_EPS) * g_row


def t5_bucket(dist):
    d = jnp.maximum(dist, 0)
    exact = REL_BUCKETS // 2
    df = jnp.maximum(d, 1).astype(jnp.float32)
    large = exact + (jnp.log(df / exact) / math.log(REL_MAX_DIST / exact) * (REL_BUCKETS - exact)).astype(jnp.int32)
    return jnp.where(d < exact, d, jnp.minimum(large, REL_BUCKETS - 1))


def _mm_body(x_ref, w_ref, o_ref):
    o_ref[...] = _dot(x_ref[...], w_ref[...]).astype(o_ref.dtype)


def matmul(x, w, out_dtype, tm, tn, name):
    m, k = x.shape
    n = w.shape[1]
    tm, tn = min(tm, m), min(tn, n)
    return pl.pallas_call(
        _mm_body,
        grid=(m // tm, n // tn),
        in_specs=[pl.BlockSpec((tm, k), lambda i, j: (i, 0)), pl.BlockSpec((k, tn), lambda i, j: (0, j))],
        out_specs=pl.BlockSpec((tm, tn), lambda i, j: (i, j)),
        out_shape=jax.ShapeDtypeStruct((m, n), out_dtype),
        compiler_params=_cparams("parallel", "arbitrary"),
        name=name,
    )(x, w)


def _ln_body(h_ref, d_ref, g_ref, b_ref, o_ref, ob_ref):
    z = DEEPNORM_ALPHA * h_ref[...] + d_ref[...]
    mu = jnp.mean(z, axis=-1, keepdims=True)
    zc = z - mu
    var = jnp.mean(zc * zc, axis=-1, keepdims=True)
    y = zc * lax.rsqrt(var + LN_EPS) * g_ref[...] + b_ref[...]
    o_ref[...] = y
    ob_ref[...] = y.astype(BF16)


def ln_residual(h, delta, g, b, name):
    n, d = h.shape
    tm = min(256, n)
    row = pl.BlockSpec((tm, d), lambda i: (i, 0))
    vec = pl.BlockSpec((1, d), lambda i: (0, 0))
    return pl.pallas_call(
        _ln_body,
        grid=(n // tm,),
        in_specs=[row, row, vec, vec],
        out_specs=[row, row],
        out_shape=[jax.ShapeDtypeStruct((n, d), F32), jax.ShapeDtypeStruct((n, d), BF16)],
        compiler_params=_cparams("parallel"),
        name=name,
    )(h, delta, g.reshape(1, d), b.reshape(1, d))


def _gla_body(q_ref, k_ref, v_ref, r_ref, misc_ref, gw_ref, gb_ref, ng_ref, o_ref, st_ref):
    L = CHUNK

    @pl.when(pl.program_id(1) == 0)
    def _():
        st_ref[...] = jnp.zeros_like(st_ref)

    pre = _dot(misc_ref[...].astype(BF16), gw_ref[...]) + gb_ref[...]
    log_a = _log_sigmoid(pre) / GLA_TAU
    row = lax.broadcasted_iota(jnp.int32, (L, L), 0)
    col = lax.broadcasted_iota(jnp.int32, (L, L), 1)
    causal = col <= row
    tri = jnp.where(causal, 1.0, 0.0).astype(F32)
    b = jnp.dot(tri, log_a, precision=HIGHEST, preferred_element_type=F32)
    b_last = b[L - 1:L, :]
    q_dec = (q_ref[...] * (DK ** -0.5) * jnp.exp(b)).astype(BF16)
    k_dec = (k_ref[...] * jnp.exp(-b)).astype(BF16)
    k_end = (k_ref[...] * jnp.exp(b_last - b)).astype(BF16)
    decay = jnp.exp(b_last)

    def heads(x, width):
        return jnp.stack([x[:, h * width:(h + 1) * width] for h in range(N_HEADS)], axis=0)

    q3, k3, ke3 = heads(q_dec, DK), heads(k_dec, DK), heads(k_end, DK)
    v3 = heads(v_ref[...].astype(BF16), HEAD_DIM)
    dec3 = heads(decay, DK)
    state_t = st_ref[...]
    att = jnp.where(causal[None], jnp.einsum('hqd,hkd->hqk', q3, k3, preferred_element_type=F32), 0.0)
    o3 = (jnp.einsum('hqk,hkv->hqv', att.astype(BF16), v3, preferred_element_type=F32)
          + jnp.einsum('hqd,hvd->hqv', q3, state_t.astype(BF16), preferred_element_type=F32))
    st_ref[...] = dec3 * state_t + jnp.einsum('hkv,hkd->hvd', v3, ke3, preferred_element_type=F32)
    on3 = _head_norm(o3, heads(ng_ref[...], HEAD_DIM))
    out3 = (_silu(heads(r_ref[...], HEAD_DIM)) * on3).astype(BF16)
    for h in range(N_HEADS):
        o_ref[:, h * HEAD_DIM:(h + 1) * HEAD_DIM] = out3[h]


def gla_mixer(u3, gate_w, gate_b, norm_g):
    bsz, s, _ = u3.shape
    L = CHUNK
    gw = jnp.zeros((LANES, N_HEADS * DK), F32).at[MISC_GA:MISC_GA + GLA_GATE_RANK].set(gate_w).astype(BF16)

    def ublock(width, off):
        return pl.BlockSpec((None, L, width), lambda b, t: (b, t, off // width))

    def const(shape):
        return pl.BlockSpec(shape, lambda b, t: (0,) * len(shape))

    return pl.pallas_call(
        _gla_body,
        grid=(bsz, s // L),
        in_specs=[ublock(512, U_GQ), ublock(512, U_GK), ublock(1024, U_GV), ublock(1024, U_GR),
                  ublock(LANES, U_MISC), const((LANES, 512)), const((1, 512)), const((1, 1024))],
        out_specs=pl.BlockSpec((None, L, 1024), lambda b, t: (b, t, 0)),
        out_shape=jax.ShapeDtypeStruct((bsz, s, BRANCH_WIDTH), BF16),
        scratch_shapes=[pltpu.VMEM((N_HEADS, HEAD_DIM, DK), F32)],
        compiler_params=_cparams("parallel", "arbitrary"),
        name="gla_mixer",
    )(u3, u3, u3, u3, u3, gw, gate_b.reshape(1, -1), norm_g.reshape(1, -1))


def _mlstm_body(qk_ref, v_ref, op_ref, misc_ref, cw_ref, cb_ref, ib_ref, fb_ref, ng_ref, o_ref,
                ext_ref, c_ref, n_ref, m_ref):
    L = CHUNK
    C2 = 2 * N_HEADS * DK

    @pl.when(pl.program_id(1) == 0)
    def _():
        ext_ref[0:8, :] = jnp.zeros((8, C2), F32)
        c_ref[...] = jnp.zeros_like(c_ref)
        n_ref[...] = jnp.zeros_like(n_ref)
        m_ref[...] = jnp.zeros_like(m_ref)

    x = qk_ref[...]
    ext_ref[8:8 + L, :] = x
    y = (cb_ref[...] + cw_ref[3:4, :] * x + cw_ref[2:3, :] * ext_ref[pl.ds(7, L), :]
         + cw_ref[1:2, :] * ext_ref[pl.ds(6, L), :] + cw_ref[0:1, :] * ext_ref[pl.ds(5, L), :])
    ext_ref[0:8, :] = x[L - 8:L, :]
    qk = _silu(y)
    q_all = qk[:, :N_HEADS * DK].astype(BF16)
    k_all = qk[:, N_HEADS * DK:] * (DK ** -0.5)

    misc = misc_ref[...]
    i_g = MLSTM_IGATE_CAP * jnp.tanh((misc + ib_ref[...]) / MLSTM_IGATE_CAP)
    log_f = _log_sigmoid(misc + fb_ref[...])
    row = lax.broadcasted_iota(jnp.int32, (L, L), 0)
    col = lax.broadcasted_iota(jnp.int32, (L, L), 1)
    causal = col <= row
    tri = jnp.where(causal, 1.0, 0.0).astype(F32)
    b = jnp.dot(tri, log_f, precision=HIGHEST, preferred_element_type=F32)
    it = pltpu.roll(i_g, MISC_MF - MISC_MI, 1)
    b_last = b[L - 1:L, :]
    m_s = m_ref[0:1, :]
    a_end = b_last - b + it
    m_new = jnp.maximum(b_last + m_s, jnp.max(a_end, axis=0, keepdims=True))
    w = jnp.exp(a_end - m_new)
    sc = jnp.exp(b_last + m_s - m_new)
    inter_log = b + m_s
    c = it - b
    rows_i = lax.broadcasted_iota(jnp.int32, c.shape, 0)
    cmax = c
    for sh in (1, 2, 4, 8, 16, 32):
        cmax = jnp.maximum(cmax, jnp.where(rows_i >= sh, pltpu.roll(cmax, sh, 0), -jnp.inf))
    m_i = jnp.maximum(inter_log, b + cmax)
    w_inter = jnp.exp(inter_log - m_i)
    e_neg = jnp.exp(-m_i)
    m_ref[0:1, :] = m_new

    stats = jnp.concatenate([b - m_i, w, w_inter, e_neg, jnp.broadcast_to(sc, (8, LANES))], axis=0)
    lane = lax.broadcasted_iota(jnp.int32, stats.shape, 1)
    stats = jnp.where((lane >= MISC_MF) & (lane < MISC_MF + N_HEADS), stats, 0.0)
    sel_r = lax.broadcasted_iota(jnp.int32, (LANES, N_HEADS * LANES), 0)
    sel_c = lax.broadcasted_iota(jnp.int32, (LANES, N_HEADS * LANES), 1) // LANES
    onehot = jnp.where(sel_r == sel_c + MISC_MF, 1.0, 0.0).astype(F32)
    spread = jnp.dot(stats, onehot, precision=HIGHEST, preferred_element_type=F32)

    def heads(x, width):
        return jnp.stack([x[:, h * width:(h + 1) * width] for h in range(N_HEADS)], axis=0)

    dcol3 = heads(spread[0:L], LANES)[:, :, 0:L]
    w3 = heads(spread[L:2 * L], LANES)[:, :, 0:DK]
    wi3 = heads(spread[2 * L:3 * L], LANES)
    en3 = heads(spread[3 * L:4 * L], LANES)
    sc3 = heads(spread[4 * L:4 * L + 1], LANES)
    c_t = c.T
    crow3 = jnp.stack([c_t[MISC_MF + h:MISC_MF + h + 1, :] for h in range(N_HEADS)], axis=0)

    q3 = heads(q_all, DK)
    k3 = heads(k_all, DK)
    v3 = heads(v_ref[...].astype(BF16), HEAD_DIM)
    c_s = c_ref[...]
    n_s = n_ref[...]
    decay3 = jnp.exp(jnp.where(causal[None], dcol3 + crow3, -jnp.inf))
    qk_s = jnp.einsum('hqd,hkd->hqk', q3, k3.astype(BF16), preferred_element_type=F32) * decay3
    num = (wi3 * jnp.einsum('hqd,hdv->hqv', q3, c_s.astype(BF16), preferred_element_type=F32)
           + jnp.einsum('hqk,hkv->hqv', qk_s.astype(BF16), v3, preferred_element_type=F32))
    qn = jnp.sum(q3.astype(F32) * n_s, axis=-1, keepdims=True)
    den = wi3[:, :, 0:1] * qn + jnp.sum(qk_s, axis=-1, keepdims=True)
    hout = num / jnp.maximum(jnp.abs(den), en3[:, :, 0:1])
    wk3 = w3 * k3
    c_ref[...] = sc3 * c_s + jnp.einsum('hkd,hkv->hdv', wk3.astype(BF16), v3, preferred_element_type=F32)
    n_ref[...] = sc3[:, :, 0:DK] * n_s + jnp.sum(wk3, axis=1, keepdims=True)
    hn = _head_norm(hout, heads(ng_ref[...], HEAD_DIM))
    out3 = (_sigmoid(heads(op_ref[...], HEAD_DIM)) * hn).astype(BF16)
    for h in range(N_HEADS):
        o_ref[:, h * HEAD_DIM:(h + 1) * HEAD_DIM] = out3[h]


def mlstm_mixer(u3, conv_w, conv_b, igate_b, fgate_b, norm_g):
    bsz, s, _ = u3.shape
    L = CHUNK
    ib = jnp.zeros((1, LANES), F32).at[0, MISC_MI:MISC_MI + N_HEADS].set(igate_b)
    fb = jnp.zeros((1, LANES), F32).at[0, MISC_MF:MISC_MF + N_HEADS].set(fgate_b)

    def ublock(width, off):
        return pl.BlockSpec((None, L, width), lambda b, t: (b, t, off // width))

    def const(shape):
        return pl.BlockSpec(shape, lambda b, t: (0,) * len(shape))

    return pl.pallas_call(
        _mlstm_body,
        grid=(bsz, s // L),
        in_specs=[ublock(1024, U_MQK), ublock(1024, U_MV), ublock(1024, U_MO), ublock(LANES, U_MISC),
                  const((MLSTM_CONV, 1024)), const((1, 1024)), const((1, LANES)), const((1, LANES)),
                  const((1, 1024))],
        out_specs=pl.BlockSpec((None, L, 1024), lambda b, t: (b, t, 0)),
        out_shape=jax.ShapeDtypeStruct((bsz, s, BRANCH_WIDTH), BF16),
        scratch_shapes=[pltpu.VMEM((L + 8, 1024), F32), pltpu.VMEM((N_HEADS, DK, HEAD_DIM), F32),
                        pltpu.VMEM((N_HEADS, 1, DK), F32), pltpu.VMEM((8, LANES), F32)],
        compiler_params=_cparams("parallel", "arbitrary"),
        name="mlstm_mixer",
    )(u3, u3, u3, u3, conv_w, conv_b.reshape(1, -1), ib, fb, norm_g.reshape(1, -1))


BIAS_DIST_RANGE = 4096


def bucket_thresholds(max_dist):
    lut = t5_bucket(jnp.arange(max_dist))
    return jnp.sum(lut[None, :] < jnp.arange(REL_BUCKETS)[:, None], axis=1).astype(jnp.int32)


def _bias_table_body(thr_ref, tab_ref, o_ref, *, kind):
    a = pl.program_id(0)
    c = pl.program_id(1)
    shape = o_ref.shape
    i = lax.broadcasted_iota(jnp.int32, shape, 0)
    j = lax.broadcasted_iota(jnp.int32, shape, 1)
    if kind == "dilated":
        dil = jnp.where(a == 0, DIL_PATTERNS[0][1], jnp.where(a == 1, DIL_PATTERNS[1][1], DIL_PATTERNS[2][1]))
        dist = (i + DIL_BLOCK - j) * dil
        head = a * N_HEADS + c
    elif kind in ("causal", "window"):
        dist = c * shape[0] + i - j
        head = 3 * N_HEADS + a
    else:
        dist = c * shape[0] + i - (j * NSA_CMP_STRIDE + NSA_CMP_BLOCK - 1)
        head = 3 * N_HEADS + a
    dist_c = jnp.maximum(dist, 0)
    acc = jnp.full(shape, tab_ref[head, 0], F32)
    for k in range(1, REL_BUCKETS):
        acc = jnp.where(dist_c >= thr_ref[k], tab_ref[head, k], acc)
    if kind in ("causal", "window"):
        limit = NSA_WINDOW if kind == "window" else 2 ** 30
        dist_v = jnp.where(c < pl.num_programs(1) - 1, dist, -1)
        acc = jnp.where((dist_v >= 0) & (dist_v < limit), acc, NEG)
    o_ref[...] = acc


def bias_table(thr, tab_t, kind, out_dims, block):
    smem = pl.BlockSpec(memory_space=pltpu.SMEM)
    return pl.pallas_call(
        functools.partial(_bias_table_body, kind=kind),
        grid=out_dims[:2] if kind != "compressed" else (out_dims[0], out_dims[1] // block[0]),
        in_specs=[smem, smem],
        out_specs=(pl.BlockSpec((None, None) + block, lambda a, c: (a, c, 0, 0)) if kind != "compressed"
                   else pl.BlockSpec((None,) + block, lambda a, c: (a, c, 0))),
        out_shape=jax.ShapeDtypeStruct(out_dims, F32),
        compiler_params=_cparams("parallel", "parallel"),
        name=f"bias_table_{kind}",
    )(thr, tab_t)


DIL_SPAN = 2048


def _dil_body(q0_ref, q1_ref, q2_ref, kp_ref, kc_ref, vp_ref, vc_ref, bias_ref, o_ref, m_scr, l_scr, acc_scr):
    P = DIL_BLOCK
    NB = DIL_SPAN // P
    q_refs = (q0_ref, q1_ref, q2_ref)
    row = lax.broadcasted_iota(jnp.int32, (P, P), 0)
    col = lax.broadcasted_iota(jnp.int32, (P, P), 1)
    mask_cur = jnp.where(col <= row, 0.0, NEG)
    mask_prev = jnp.where(col >= row, 0.0, NEG)
    no_prev = jnp.where(pl.program_id(1) > 0, 0, P)
    mask_prev_first = jnp.where((col - row) >= no_prev, 0.0, NEG)

    def rows(r, n, dil):
        return pl.ds(r, n, stride=dil) if dil > 1 else pl.ds(r, n)

    def stacked(ref, dil):
        per = DIL_SPAN // dil
        return jnp.concatenate([ref[rows(r, per, dil), :].reshape(per // P, P, HEAD_DIM) for r in range(dil)], axis=0)

    def stacked_prev(cur3, prev_ref, dil):
        nblk = NB // dil
        parts = []
        for r in range(dil):
            parts.append(prev_ref[rows(r + DIL_SPAN - dil * P, P, dil), :].reshape(1, P, HEAD_DIM))
            if nblk > 1:
                parts.append(cur3[r * nblk:(r + 1) * nblk - 1])
        return jnp.concatenate(parts, axis=0)

    for g, (_, dil) in enumerate(DIL_PATTERNS):
        nblk = NB // dil
        per = DIL_SPAN // dil
        q3 = (stacked(q_refs[g], dil) * (HEAD_DIM ** -0.5)).astype(BF16)
        kc3 = stacked(kc_ref, dil)
        vc3 = stacked(vc_ref, dil)
        kp3 = stacked_prev(kc3, kp_ref, dil).astype(BF16)
        vp3 = stacked_prev(vc3, vp_ref, dil).astype(BF16)
        kc3 = kc3.astype(BF16)
        vc3 = vc3.astype(BF16)
        bias_p = bias_ref[g, :, 0:P]
        bias_prev3 = jnp.concatenate(
            [(bias_p + (mask_prev_first if b % nblk == 0 else mask_prev)).reshape(1, P, P) for b in range(NB)], axis=0)
        s_p = jnp.einsum('nqd,nkd->nqk', q3, kp3, preferred_element_type=F32) + bias_prev3
        s_c = (jnp.einsum('nqd,nkd->nqk', q3, kc3, preferred_element_type=F32)
               + (bias_ref[g, :, P:2 * P] + mask_cur)[None])
        m = jnp.maximum(jnp.max(s_p, axis=-1, keepdims=True), jnp.max(s_c, axis=-1, keepdims=True))
        p_p = jnp.exp(s_p - m)
        p_c = jnp.exp(s_c - m)
        l = jnp.sum(p_p, axis=-1, keepdims=True) + jnp.sum(p_c, axis=-1, keepdims=True)
        acc = (jnp.einsum('nqk,nkd->nqd', p_p.astype(BF16), vp3, preferred_element_type=F32)
               + jnp.einsum('nqk,nkd->nqd', p_c.astype(BF16), vc3, preferred_element_type=F32))
        for r in range(dil):
            sl = rows(g * DIL_SPAN + r, per, dil)
            bs = slice(r * nblk, (r + 1) * nblk)
            m_scr[sl, :] = jnp.broadcast_to(m[bs].reshape(per, 1), (per, LANES))
            l_scr[sl, :] = jnp.broadcast_to(l[bs].reshape(per, 1), (per, LANES))
            acc_scr[sl, :] = acc[bs].reshape(per, HEAD_DIM)

    ms = [m_scr[g * DIL_SPAN:(g + 1) * DIL_SPAN, :] for g in range(len(DIL_PATTERNS))]
    m_all = jnp.maximum(jnp.maximum(ms[0], ms[1]), ms[2])
    num = jnp.zeros((DIL_SPAN, HEAD_DIM), F32)
    den = jnp.zeros((DIL_SPAN, LANES), F32)
    for g in range(len(DIL_PATTERNS)):
        w = jnp.exp(ms[g] - m_all)
        num = num + w * acc_scr[g * DIL_SPAN:(g + 1) * DIL_SPAN, :]
        den = den + w * l_scr[g * DIL_SPAN:(g + 1) * DIL_SPAN, :]
    o_ref[...] = (num / den).astype(BF16)


def dilated_bias(thr, tab_t):
    P = DIL_BLOCK
    return bias_table(thr, tab_t, "dilated", (len(DIL_PATTERNS), N_HEADS, P, 2 * P), (P, 2 * P))


def dilated_mixer(u3, bias):
    bsz, s, _ = u3.shape
    P = DIL_BLOCK
    assert s % DIL_SPAN == 0 and all(w // d == P and DIL_SPAN % (d * P) == 0 for w, d in DIL_PATTERNS)

    def ub(off, prev):
        def idx(b, t, h):
            return (b, jnp.maximum(t - 1, 0) if prev else t, off // HEAD_DIM + h)
        return pl.BlockSpec((None, DIL_SPAN, HEAD_DIM), idx)

    return pl.pallas_call(
        _dil_body,
        grid=(bsz, s // DIL_SPAN, N_HEADS),
        in_specs=[ub(U_DQ, False), ub(U_DQ + 1024, False), ub(U_DQ + 2048, False),
                  ub(U_DK, True), ub(U_DK, False), ub(U_DV, True), ub(U_DV, False),
                  pl.BlockSpec((len(DIL_PATTERNS), None, P, 2 * P), lambda b, t, h: (0, h, 0, 0))],
        out_specs=pl.BlockSpec((None, DIL_SPAN, HEAD_DIM), lambda b, t, h: (b, t, h)),
        out_shape=jax.ShapeDtypeStruct((bsz, s, BRANCH_WIDTH), BF16),
        scratch_shapes=[pltpu.VMEM((len(DIL_PATTERNS) * DIL_SPAN, LANES), F32),
                        pltpu.VMEM((len(DIL_PATTERNS) * DIL_SPAN, LANES), F32),
                        pltpu.VMEM((len(DIL_PATTERNS) * DIL_SPAN, HEAD_DIM), F32)],
        compiler_params=_cparams("parallel", "parallel", "arbitrary"),
        name="dilated_attn",
    )(u3, u3, u3, u3, u3, u3, u3, bias)


def _gelu_tanh(x):
    return 0.5 * x * (1.0 + jnp.tanh(math.sqrt(2.0 / math.pi) * (x + 0.044715 * (x * x * x))))


def _nsa_compress_body(t_ref, pe_ref, w1_ref, w2_ref, o_ref):
    nblk = t_ref.shape[0] // NSA_CMP_STRIDE
    half = NSA_CMP_STRIDE
    t1 = jnp.zeros((nblk, NSA_CMP_HIDDEN), F32)
    t2 = jnp.zeros((nblk, NSA_CMP_HIDDEN), F32)
    for p in range(half):
        xp = t_ref[pl.ds(p, nblk, stride=NSA_CMP_STRIDE), :]
        t1 = t1 + _dot((xp + pe_ref[p:p + 1, :]).astype(BF16), w1_ref[p * HEAD_DIM:(p + 1) * HEAD_DIM, :])
        t2 = t2 + _dot((xp + pe_ref[half + p:half + p + 1, :]).astype(BF16),
                       w1_ref[(half + p) * HEAD_DIM:(half + p + 1) * HEAD_DIM, :])
    hidden = t1 + pltpu.roll(t2, nblk - 1, 0)
    o_ref[...] = _dot(_gelu_tanh(hidden).astype(BF16), w2_ref[...])


def _nsa_compress(u3, off, pe, w1, w2, name):
    bsz, s, _ = u3.shape
    nblk = s // NSA_CMP_STRIDE
    G = NSA_KV_GROUPS
    return pl.pallas_call(
        _nsa_compress_body,
        grid=(bsz, G),
        in_specs=[pl.BlockSpec((None, s, HEAD_DIM), lambda b, g: (b, 0, off // HEAD_DIM + g)),
                  pl.BlockSpec((NSA_CMP_BLOCK, HEAD_DIM), lambda b, g: (0, 0)),
                  pl.BlockSpec((NSA_CMP_BLOCK * HEAD_DIM, NSA_CMP_HIDDEN), lambda b, g: (0, 0)),
                  pl.BlockSpec((NSA_CMP_HIDDEN, HEAD_DIM), lambda b, g: (0, 0))],
        out_specs=pl.BlockSpec((None, None, nblk, HEAD_DIM), lambda b, g: (b, g, 0, 0)),
        out_shape=jax.ShapeDtypeStruct((bsz, G, nblk, HEAD_DIM), F32),
        compiler_params=_cparams("parallel", "parallel"),
        name=name,
    )(u3, pe, w1.astype(BF16), w2.astype(BF16))


def _nsa_cmp_body(q_ref, kc_ref, vc_ref, bc_ref, agg_ref, oc_ref, sel_ref):
    TQ = q_ref.shape[0]
    ncb = kc_ref.shape[0]
    nsb = agg_ref.shape[1]
    t0 = pl.program_id(2) * TQ
    tpos = t0 + lax.broadcasted_iota(jnp.int32, (TQ, ncb), 0)
    ends = lax.broadcasted_iota(jnp.int32, (TQ, ncb), 1) * NSA_CMP_STRIDE + (NSA_CMP_BLOCK - 1)
    mask = ends <= tpos
    maskf = jnp.where(mask, 1.0, 0.0).astype(F32)
    kcb = kc_ref[...].astype(BF16)
    vcb = vc_ref[...].astype(BF16)
    psum = jnp.zeros((TQ, ncb), F32)
    for hg in range(NSA_HPG):
        hs = slice(hg * HEAD_DIM, (hg + 1) * HEAD_DIM)
        q = (q_ref[:, hs] * (HEAD_DIM ** -0.5)).astype(BF16)
        s = jnp.where(mask, _dot_t(q, kcb) + bc_ref[hg], NEG)
        p = jnp.exp(s - jnp.max(s, axis=-1, keepdims=True)) * maskf
        l = jnp.sum(p, axis=-1, keepdims=True)
        p = p / jnp.where(l > 0, l, 1.0)
        oc_ref[:, hs] = _dot(p.astype(BF16), vcb)
        psum = psum + p
    imp = jnp.dot(psum, agg_ref[...], precision=HIGHEST, preferred_element_type=F32)
    jblk = lax.broadcasted_iota(jnp.int32, (TQ, nsb), 1)
    tq = t0 + lax.broadcasted_iota(jnp.int32, (TQ, nsb), 0)
    cur = tq // NSA_SEL_BLOCK
    forced = (jblk == 0) | (jblk == cur) | (jblk == cur - 1)
    score = jnp.where(forced, BIG, jnp.where(jblk * NSA_SEL_BLOCK <= tq, imp, -BIG))
    jf = jblk.astype(F32)
    sel = jnp.zeros((TQ, nsb), F32)
    for _ in range(min(NSA_N_SEL, nsb)):
        mx = jnp.max(score, axis=-1, keepdims=True)
        first = jnp.min(jnp.where(score == mx, jf, float(nsb)), axis=-1, keepdims=True)
        pick = jf == first
        sel = jnp.where(pick, 1.0, sel)
        score = jnp.where(pick, -jnp.inf, score)
    sel_ref[...] = sel


def _sel_agg_matrix(n_cmp, n_sb):
    r = NSA_SEL_BLOCK // NSA_CMP_STRIDE
    c = NSA_CMP_BLOCK // NSA_CMP_STRIDE
    jj, aa, bb = np.meshgrid(np.arange(n_sb), np.arange(r), np.arange(c), indexing='ij')
    ii = r * jj + aa + bb - 1
    ok = (ii >= 0) & (ii < n_cmp)
    mat = np.zeros((n_cmp, n_sb), np.float32)
    np.add.at(mat, (ii[ok], jj[ok]), 1.0)
    return mat


def _nsa_selwin_body(q_ref, oc_ref, sel_ref, misc_ref, ks_ref, vs_ref, kw_ref, vw_ref, tt_ref, tw_ref, o_ref,
                     ksb_ref, vsb_ref, m_ref, l_ref, acc_ref, s_ref):
    TQ = q_ref.shape[0]
    TK = NSA_SEL_TK
    HG = NSA_HPG
    nsb = sel_ref.shape[1]
    sub = TK // TQ
    g = pl.program_id(1)
    qi = pl.program_id(2)

    @pl.when(qi == 0)
    def _():
        ksb_ref[:, 0:HEAD_DIM] = ks_ref[...].astype(BF16)
        blk = lax.broadcasted_iota(jnp.int32, (ksb_ref.shape[0], nsb), 0) // NSA_SEL_BLOCK
        ksb_ref[:, HEAD_DIM:HEAD_DIM + nsb] = jnp.where(
            blk == lax.broadcasted_iota(jnp.int32, (ksb_ref.shape[0], nsb), 1), 1.0, 0.0).astype(BF16)
        vsb_ref[...] = vs_ref[...].astype(BF16)

    q4 = jnp.concatenate([(q_ref[:, hg * HEAD_DIM:(hg + 1) * HEAD_DIM] * (HEAD_DIM ** -0.5)).astype(BF16)
                          for hg in range(HG)], axis=0)
    unselected = jnp.where(sel_ref[...] > 0.5, 0.0, NEG).astype(BF16)
    q4_aug = jnp.concatenate([q4, jnp.concatenate([unselected] * HG, axis=0)], axis=1)
    masked_tile = tt_ref.shape[1] - 1

    m_ref[...] = jnp.full_like(m_ref, 0.1 * NEG)
    l_ref[...] = jnp.zeros_like(l_ref)
    acc_ref[...] = jnp.zeros_like(acc_ref)

    def logits(kj):
        k0 = pl.multiple_of(kj * TK, TK)

        def tile_of(w):
            d = qi - (kj * sub + w)
            return jnp.where(d >= 0, d, masked_tile)

        bias4 = jnp.concatenate([jnp.concatenate([tt_ref[hg, tile_of(w)] for w in range(sub)], axis=1)
                                 for hg in range(HG)], axis=0)
        return _dot_t(q4_aug, ksb_ref[pl.ds(k0, TK), :]) + bias4

    s_ref[...] = logits(0)

    def consume(kj):
        s = s_ref[...]
        v_tile = vsb_ref[pl.ds(pl.multiple_of(kj * TK, TK), TK), :]
        m_old = m_ref[...]
        m_new = jnp.maximum(m_old, jnp.max(s, axis=-1, keepdims=True))
        alpha = jnp.exp(m_old - m_new)
        p = jnp.exp(s - m_new)
        l_ref[...] = alpha * l_ref[...] + jnp.sum(p, axis=-1, keepdims=True)
        acc_ref[...] = alpha * acc_ref[...] + _dot(p.astype(BF16), v_tile)
        m_ref[...] = m_new

    def sel_step(kj, carry):
        consume(kj)
        s_ref[...] = logits(kj + 1)
        return carry

    n_last = qi // sub
    lax.fori_loop(0, n_last, sel_step, 0)
    consume(n_last)

    nwin = NSA_WINDOW // TQ + 1
    first = jnp.maximum(qi - (nwin - 1), 0)
    w0 = pl.multiple_of(first * TQ, TQ)
    kw_tile = kw_ref[pl.ds(w0, nwin * TQ), :].astype(BF16)
    vw_tile = vw_ref[pl.ds(w0, nwin * TQ), :].astype(BF16)

    def wtile_of(w):
        d = qi - (first + w)
        return jnp.where(d >= 0, d, tw_ref.shape[1] - 1)

    wbias4 = jnp.concatenate([jnp.concatenate([tw_ref[hg, wtile_of(w)] for w in range(nwin)], axis=1)
                              for hg in range(HG)], axis=0)
    s = _dot_t(q4, kw_tile) + wbias4
    p = jnp.exp(s - jnp.max(s, axis=-1, keepdims=True))
    o_win = _dot(p.astype(BF16), vw_tile) / jnp.sum(p, axis=-1, keepdims=True)
    o_sel = acc_ref[...] / l_ref[...]

    misc = misc_ref[...]
    gpre = jnp.where(g == 0, misc[:, MISC_NG:MISC_NG + 3 * HG], misc[:, MISC_NG + 3 * HG:MISC_NG + 6 * HG])
    gates = _sigmoid(gpre)
    for hg in range(HG):
        hs = slice(hg * HEAD_DIM, (hg + 1) * HEAD_DIM)
        rs = slice(hg * TQ, (hg + 1) * TQ)
        y = (gates[:, 3 * hg:3 * hg + 1] * oc_ref[:, hs] + gates[:, 3 * hg + 1:3 * hg + 2] * o_sel[rs]
             + gates[:, 3 * hg + 2:3 * hg + 3] * o_win[rs])
        o_ref[:, hs] = y.astype(BF16)


NSA_TQ = 128
NSA_CMP_TQ = 512


def nsa_bias(thr, tab_t, s):
    nh = NSA_KV_GROUPS * NSA_HPG
    tile = (NSA_TQ, NSA_TQ)
    tt = bias_table(thr, tab_t, "causal", (nh, s // NSA_TQ + 1) + tile, tile)
    tw = bias_table(thr, tab_t, "window", (nh, NSA_WINDOW // NSA_TQ + 2) + tile, tile)
    bc = bias_table(thr, tab_t, "compressed", (nh, s, s // NSA_CMP_STRIDE), (NSA_TQ, s // NSA_CMP_STRIDE))
    return tt, tw, bc


def nsa_mixer(u3, tt, tw, bc, pe_k, pe_v, ck_w1, ck_w2, cv_w1, cv_w2):
    bsz, s, _ = u3.shape
    G, HG, dh = NSA_KV_GROUPS, NSA_HPG, HEAD_DIM
    TQ = NSA_TQ
    nqt = s // TQ
    assert NSA_KV_GROUPS == 2 and s % NSA_SEL_TK == 0 and s >= NSA_WINDOW + TQ
    nblk = s // NSA_CMP_STRIDE
    n_cmp = (s - NSA_CMP_BLOCK) // NSA_CMP_STRIDE + 1
    n_sb = s // NSA_SEL_BLOCK
    agg = np.zeros((nblk, n_sb), np.float32)
    agg[:n_cmp] = _sel_agg_matrix(n_cmp, n_sb)
    agg = jnp.asarray(agg)

    k_cmp = _nsa_compress(u3, U_NKC, pe_k, ck_w1, ck_w2, "nsa_compress_k")
    v_cmp = _nsa_compress(u3, U_NVC, pe_v, cv_w1, cv_w2, "nsa_compress_v")

    qspec = pl.BlockSpec((None, TQ, HG * dh), lambda b, g, i: (b, i, U_NQ // (HG * dh) + g))
    TC = NSA_CMP_TQ
    o_cmp, sel = pl.pallas_call(
        _nsa_cmp_body,
        grid=(bsz, G, s // TC),
        in_specs=[pl.BlockSpec((None, TC, HG * dh), lambda b, g, i: (b, i, U_NQ // (HG * dh) + g)),
                  pl.BlockSpec((None, None, nblk, dh), lambda b, g, i: (b, g, 0, 0)),
                  pl.BlockSpec((None, None, nblk, dh), lambda b, g, i: (b, g, 0, 0)),
                  pl.BlockSpec((HG, TC, nblk), lambda b, g, i: (g, i, 0)),
                  pl.BlockSpec((nblk, n_sb), lambda b, g, i: (0, 0))],
        out_specs=[pl.BlockSpec((None, TC, HG * dh), lambda b, g, i: (b, i, g)),
                   pl.BlockSpec((None, None, TC, n_sb), lambda b, g, i: (b, g, i, 0))],
        out_shape=[jax.ShapeDtypeStruct((bsz, s, G * HG * dh), F32),
                   jax.ShapeDtypeStruct((bsz, G, s, n_sb), F32)],
        compiler_params=_cparams("parallel", "parallel", "arbitrary"),
        name="nsa_compressed_attn",
    )(u3, k_cmp, v_cmp, bc, agg)

    def kv(off):
        return pl.BlockSpec((None, s, dh), lambda b, g, i: (b, 0, off // dh + g))

    y = pl.pallas_call(
        _nsa_selwin_body,
        grid=(bsz, G, nqt),
        in_specs=[qspec,
                  pl.BlockSpec((None, TQ, HG * dh), lambda b, g, i: (b, i, g)),
                  pl.BlockSpec((None, None, TQ, n_sb), lambda b, g, i: (b, g, i, 0)),
                  pl.BlockSpec((None, TQ, LANES), lambda b, g, i: (b, i, U_MISC // LANES)),
                  kv(U_NKS), kv(U_NVS), kv(U_NKW), kv(U_NVW),
                  pl.BlockSpec((HG,) + tt.shape[1:], lambda b, g, i: (g, 0, 0, 0)),
                  pl.BlockSpec((HG,) + tw.shape[1:], lambda b, g, i: (g, 0, 0, 0))],
        out_specs=pl.BlockSpec((None, TQ, HG * dh), lambda b, g, i: (b, i, g)),
        out_shape=jax.ShapeDtypeStruct((bsz, s, BRANCH_WIDTH), BF16),
        scratch_shapes=[pltpu.VMEM((s, dh + n_sb), BF16), pltpu.VMEM((s, dh), BF16),
                        pltpu.VMEM((HG * TQ, 1), F32), pltpu.VMEM((HG * TQ, 1), F32),
                        pltpu.VMEM((HG * TQ, dh), F32), pltpu.VMEM((HG * TQ, NSA_SEL_TK), F32)],
        compiler_params=_cparams("parallel", "parallel", "arbitrary"),
        name="nsa_selected_window_attn",
    )(u3, o_cmp, sel, u3, u3, u3, u3, u3, tt, tw)
    return y


def _merge_body(x_ref, wg0, wg1, wg2, wg3, y0, y1, y2, y3, wb_ref, o_ref):
    x = x_ref[...]
    acc = None
    for b, (wg, y) in enumerate(((wg0, y0), (wg1, y1), (wg2, y2), (wg3, y3))):
        gate = _sigmoid(_dot(x, wg[...]))
        term = gate * _dot(y[...], wb_ref[b])
        acc = term if acc is None else acc + term
    o_ref[...] = acc.astype(o_ref.dtype)


def merge_branches(hb, w_gates, ys, w_branch):
    n, d = hb.shape
    tm, tn = min(1024, n), min(256, d)
    nj = d // tn

    def wg(b):
        return pl.BlockSpec((d, tn), lambda i, j: (0, b * nj + j))

    yspec = pl.BlockSpec((tm, BRANCH_WIDTH), lambda i, j: (i, 0), pipeline_mode=pl.Buffered(1))
    return pl.pallas_call(
        _merge_body,
        grid=(n // tm, nj),
        in_specs=[pl.BlockSpec((tm, d), lambda i, j: (i, 0), pipeline_mode=pl.Buffered(1)),
                  wg(0), wg(1), wg(2), wg(3),
                  yspec, yspec, yspec, yspec,
                  pl.BlockSpec((N_BRANCH, BRANCH_WIDTH, tn), lambda i, j: (0, 0, j))],
        out_specs=pl.BlockSpec((tm, tn), lambda i, j: (i, j)),
        out_shape=jax.ShapeDtypeStruct((n, d), BF16),
        compiler_params=_cparams("parallel", "arbitrary"),
        name="merge_branches",
    )(hb, w_gates, w_gates, w_gates, w_gates, *ys, w_branch)


MOE_TN = 512
EXPERTS_PER_TILE = MOE_TN // EXPERT_FF


def _router_body(x_ref, w_ref, b_ref, comb_ref, combt_ref):
    logits = _dot(x_ref[...], w_ref[...]) + b_ref[...]
    lane = lax.broadcasted_iota(jnp.int32, logits.shape, 1).astype(F32)
    work = logits
    picks, vals = [], []
    for _ in range(TOP_K):
        mx = jnp.max(work, axis=-1, keepdims=True)
        first = jnp.min(jnp.where(work == mx, lane, float(LANES)), axis=-1, keepdims=True)
        pick = lane == first
        picks.append(pick)
        vals.append(mx)
        work = jnp.where(pick, -jnp.inf, work)
    exps = [jnp.exp(v - vals[0]) for v in vals]
    den = exps[0]
    for e in exps[1:]:
        den = den + e
    comb = jnp.zeros_like(logits)
    for pick, e in zip(picks, exps):
        comb = comb + jnp.where(pick, e / den, 0.0)
    comb_ref[...] = comb
    for t in range(N_EXPERTS // EXPERTS_PER_TILE):
        combt_ref[t] = comb[:, t * EXPERTS_PER_TILE:(t + 1) * EXPERTS_PER_TILE]


def moe_router(hb, router_w, router_b):
    n, d = hb.shape
    tm = min(512, n)
    w = jnp.zeros((d, LANES), F32).at[:, :N_EXPERTS].set(router_w).astype(BF16)
    b = jnp.full((1, LANES), NEG, F32).at[0, :N_EXPERTS].set(router_b)
    nt = N_EXPERTS // EXPERTS_PER_TILE
    return pl.pallas_call(
        _router_body,
        grid=(n // tm,),
        in_specs=[pl.BlockSpec((tm, d), lambda i: (i, 0)), pl.BlockSpec((d, LANES), lambda i: (0, 0)),
                  pl.BlockSpec((1, LANES), lambda i: (0, 0))],
        out_specs=[pl.BlockSpec((tm, LANES), lambda i: (i, 0)),
                   pl.BlockSpec((nt, tm, EXPERTS_PER_TILE), lambda i: (0, i, 0))],
        out_shape=[jax.ShapeDtypeStruct((n, LANES), F32), jax.ShapeDtypeStruct((nt, n, EXPERTS_PER_TILE), F32)],
        compiler_params=_cparams("parallel"),
        name="moe_router",
    )(hb, w, b)


def _moe_up_body(x_ref, wg_ref, wu_ref, bg_ref, bu_ref, comb_ref, a_ref):
    x = x_ref[...]
    gate = jnp.minimum(_dot(x, wg_ref[...]) + bg_ref[...], SWIGLU_LIMIT)
    up = jnp.clip(_dot(x, wu_ref[...]) + bu_ref[...], -SWIGLU_LIMIT, SWIGLU_LIMIT)
    act = gate * _sigmoid(SWIGLU_ALPHA * gate) * (up + 1.0)
    for e in range(EXPERTS_PER_TILE):
        es = slice(e * EXPERT_FF, (e + 1) * EXPERT_FF)
        a_ref[:, es] = (act[:, es] * comb_ref[:, e:e + 1]).astype(BF16)


def moe_up(hb, w_gate, w_up, b_gate, b_up, comb_t):
    n, d = hb.shape
    tm = min(1024, n)
    width = N_EXPERTS * EXPERT_FF
    wspec = pl.BlockSpec((d, MOE_TN), lambda i, j: (0, j))
    bspec = pl.BlockSpec((1, MOE_TN), lambda i, j: (0, j))
    return pl.pallas_call(
        _moe_up_body,
        grid=(n // tm, width // MOE_TN),
        in_specs=[pl.BlockSpec((tm, d), lambda i, j: (i, 0)), wspec, wspec, bspec, bspec,
                  pl.BlockSpec((None, tm, EXPERTS_PER_TILE), lambda i, j: (j, i, 0))],
        out_specs=pl.BlockSpec((tm, MOE_TN), lambda i, j: (i, j)),
        out_shape=jax.ShapeDtypeStruct((n, width), BF16),
        compiler_params=_cparams("parallel", "arbitrary"),
        name="moe_up",
    )(hb, w_gate, w_up, b_gate, b_up, comb_t)


def _moe_down_body(a_ref, w_ref, comb_ref, b2_ref, o_ref):
    o_ref[...] = (_dot(a_ref[...], w_ref[...]) + _dot(comb_ref[...].astype(BF16), b2_ref[...])).astype(o_ref.dtype)


def moe_down(a, w2, comb, b2):
    n, k = a.shape
    d = w2.shape[1]
    tm, tn = min(1024, n), min(1024, d)
    return pl.pallas_call(
        _moe_down_body,
        grid=(n // tm, d // tn),
        in_specs=[pl.BlockSpec((tm, k), lambda i, j: (i, 0)), pl.BlockSpec((k, tn), lambda i, j: (0, j)),
                  pl.BlockSpec((tm, LANES), lambda i, j: (i, 0)), pl.BlockSpec((LANES, tn), lambda i, j: (0, j))],
        out_specs=pl.BlockSpec((tm, tn), lambda i, j: (i, j)),
        out_shape=jax.ShapeDtypeStruct((n, d), BF16),
        compiler_params=_cparams("parallel", "arbitrary"),
        name="moe_down",
    )(a, w2, comb, b2)


def _regroup_body(w_ref, u_ref, g_ref):
    o = _SRC_OFF
    segments = ((o[0], o[4]),
                (o[5], o[8]),
                (o[10], o[13]),
                (o[13], o[20]),
                (o[4], o[5]), (o[8], o[10]), (o[20], o[21]))
    pos = 0
    for a, b in segments:
        u_ref[:, pos:pos + (b - a)] = w_ref[:, a:b]
        pos += b - a
    u_ref[:, pos:] = jnp.zeros((u_ref.shape[0], U_WIDTH - pos), BF16)
    g_ref[...] = w_ref[:, SRC_GATES:]


def _regroup_w_in(w):
    d, width = w.shape
    rows = LANES
    return pl.pallas_call(
        _regroup_body,
        grid=(d // rows,),
        in_specs=[pl.BlockSpec((rows, width), lambda i: (i, 0))],
        out_specs=[pl.BlockSpec((rows, U_WIDTH), lambda i: (i, 0)),
                   pl.BlockSpec((rows, width - SRC_GATES), lambda i: (i, 0))],
        out_shape=[jax.ShapeDtypeStruct((d, U_WIDTH), BF16), jax.ShapeDtypeStruct((d, width - SRC_GATES), BF16)],
        compiler_params=_cparams("parallel"),
        name="regroup_w_in",
    )(w)


def kernel(x, rel_bias, w_in, gla_gate_w, gla_gate_b, gla_norm_g, mlstm_conv_w, mlstm_conv_b, mlstm_igate_b,
           mlstm_fgate_b, mlstm_norm_g, nsa_pe_k, nsa_pe_v, nsa_ck_w1, nsa_ck_w2, nsa_cv_w1, nsa_cv_w2, w_branch,
           w_out, ln1_g, ln1_b, router_w, router_b, exp_w1, exp_b1, exp_w2, exp_b2, ln2_g, ln2_b):
    out_dtype = x.dtype
    bsz, s, d = x.shape
    n = bsz * s
    h = x.astype(F32).reshape(n, d)
    hb = h.astype(BF16)
    thr = bucket_thresholds(max(s, BIAS_DIST_RANGE))
    tab_t = rel_bias.T
    dil_bias = dilated_bias(thr, tab_t)
    nsa_tt, nsa_tw, nsa_bc = nsa_bias(thr, tab_t, s)
    for l in range(w_in.shape[0]):
        w_u, w_gates = _regroup_w_in(w_in[l].astype(BF16))
        u = matmul(hb, w_u, F32, 1024, 1024, "input_projection")
        u3 = u.reshape(bsz, s, U_WIDTH)
        y_a = gla_mixer(u3, gla_gate_w[l], gla_gate_b[l], gla_norm_g[l]).reshape(n, BRANCH_WIDTH)
        y_b = mlstm_mixer(u3, mlstm_conv_w[l], mlstm_conv_b[l], mlstm_igate_b[l], mlstm_fgate_b[l],
                          mlstm_norm_g[l]).reshape(n, BRANCH_WIDTH)
        y_c = dilated_mixer(u3, dil_bias).reshape(n, BRANCH_WIDTH)
        y_d = nsa_mixer(u3, nsa_tt, nsa_tw, nsa_bc, nsa_pe_k[l], nsa_pe_v[l], nsa_ck_w1[l], nsa_ck_w2[l], nsa_cv_w1[l],
                        nsa_cv_w2[l]).reshape(n, BRANCH_WIDTH)
        merged = merge_branches(hb, w_gates, (y_a, y_b, y_c, y_d), w_branch[l].astype(BF16))
        attn = matmul(merged, w_out[l].astype(BF16), BF16, 1024, 1024, "output_projection")
        h, hb = ln_residual(h, attn, ln1_g[l], ln1_b[l], "layer_norm_1")

        comb, comb_t = moe_router(hb, router_w[l], router_b[l])
        w1 = exp_w1[l]
        w_gate = w1[:, :, :EXPERT_FF].transpose(1, 0, 2).reshape(d, N_EXPERTS * EXPERT_FF).astype(BF16)
        w_up = w1[:, :, EXPERT_FF:].transpose(1, 0, 2).reshape(d, N_EXPERTS * EXPERT_FF).astype(BF16)
        b_gate = exp_b1[l][:, :EXPERT_FF].reshape(1, -1)
        b_up = exp_b1[l][:, EXPERT_FF:].reshape(1, -1)
        act = moe_up(hb, w_gate, w_up, b_gate, b_up, comb_t)
        b2 = jnp.zeros((LANES, d), F32).at[:N_EXPERTS].set(exp_b2[l]).astype(BF16)
        ffn = moe_down(act, exp_w2[l].reshape(N_EXPERTS * EXPERT_FF, d).astype(BF16), comb, b2)
        h, hb = ln_residual(h, ffn, ln2_g[l], ln2_b[l], "layer_norm_2")
    return h.reshape(bsz, s, d).astype(out_dtype)
```
